```python
import math
import jax, jax.numpy as jnp
from jax import lax
import numpy as np

D_MODEL = 4096
BATCH = 16
SEQ = 256
DEPTH = 2
DEC_BATCH = 2
DEC_SEQ = 4096
PAST_LEN = 256

GRID_W = 64
BRANCH_WIDTH = D_MODEL // 4
ATT_HEAD_DIM = 128
ATT_HEADS = BRANCH_WIDTH // ATT_HEAD_DIM
ATT_KV_HEADS = 2
ATT_Q_PER_KV = ATT_HEADS // ATT_KV_HEADS
KV_WIDTH = ATT_KV_HEADS * ATT_HEAD_DIM
WINDOW = 128
ATT_BLOCK = 128
ROPE_BASE = 10000.0
SSD_HEAD_DIM = 64
SSD_HEADS = BRANCH_WIDTH // SSD_HEAD_DIM
SSD_GROUPS = 2
SSD_HEADS_PER_GROUP = SSD_HEADS // SSD_GROUPS
SSD_STATE = 128
SSD_CONV = 4
SSD_CHUNK = 128
SSD_CONV_CH = BRANCH_WIDTH + 2 * SSD_GROUPS * SSD_STATE
LRU_HEADS = 8
LRU_BLOCK = BRANCH_WIDTH // LRU_HEADS
LRU_CONV = 4
LRU_C = 8.0
POOL_WINDOWS = (2, 4, 8, 16)
POOL_GROUPS = 4
POOL_GROUP_DIM = BRANCH_WIDTH // POOL_GROUPS
N_EXPERTS = 16
N_EXPERT_GROUPS = 4
EXPERTS_PER_GROUP = N_EXPERTS // N_EXPERT_GROUPS
TOP_K = 2
D_FF = D_MODEL // 4
MOE_BLOCK = 256
ALPHA = (2 * DEPTH) ** 0.25
BETA = (8 * DEPTH) ** -0.25
EPS = 1e-5
IN_SPLITS = (BRANCH_WIDTH, KV_WIDTH, KV_WIDTH,
             BRANCH_WIDTH, SSD_CONV_CH, 2 * SSD_HEADS,
             BRANCH_WIDTH, BRANCH_WIDTH,
             BRANCH_WIDTH)
IN_COLS = sum(IN_SPLITS)

kernel_name = 'hybrid_diffusion_ctx_prefix_step'


def split_in_proj(proj):
    parts, start = [], 0
    for width in IN_SPLITS:
        parts.append(proj[..., start:start + width])
        start += width
    return parts


def layer_norm(x, g, b):
    xf = x.astype(jnp.float32)
    mu = jnp.mean(xf, -1, keepdims=True)
    var = jnp.mean(jnp.square(xf - mu), -1, keepdims=True)
    return ((xf - mu) * lax.rsqrt(var + EPS) * g + b).astype(x.dtype)


def conv_centred(x, w, b):
    width = w.shape[0]
    left = (width - 1) // 2
    L = x.shape[1]
    xp = jnp.pad(x, ((0, 0), (left, width - 1 - left), (0, 0)))
    out = xp[:, 0:L] * w[0]
    for tap in range(1, width):
        out = out + xp[:, tap:tap + L] * w[tap]
    return out + b


def rope_1d(x, pos):
    nf = x.shape[-1] // 2
    inv = ROPE_BASE ** (-jnp.arange(nf, dtype=jnp.float32) / nf)
    ang = pos.astype(jnp.float32)[:, None] * inv[None, :]
    cos = jnp.cos(ang)[None, :, None, :]
    sin = jnp.sin(ang)[None, :, None, :]
    xf = x.astype(jnp.float32)
    x1, x2 = xf[..., :nf], xf[..., nf:]
    return jnp.concatenate([x1 * cos - x2 * sin, x2 * cos + x1 * sin], -1).astype(x.dtype)


def axial_rope(x, row_pos, col_pos):
    half = x.shape[-1] // 2
    return jnp.concatenate([rope_1d(x[..., :half], row_pos), rope_1d(x[..., half:], col_pos)], -1)


def context_attention(q, k, v, sink):
    b, Lc = q.shape[:2]
    nb = Lc // ATT_BLOCK
    scale = ATT_HEAD_DIM ** -0.5
    qb = q.reshape(b, nb, ATT_BLOCK, ATT_KV_HEADS, ATT_Q_PER_KV, ATT_HEAD_DIM).swapaxes(0, 1)
    sink_l = sink.astype(jnp.float32).reshape(1, ATT_KV_HEADS, ATT_Q_PER_KV, 1, 1)

    def block(qblk):
        s = jnp.einsum('bqkgd,bskd->bkgqs', qblk, k).astype(jnp.float32) * scale
        logits = jnp.concatenate([jnp.broadcast_to(sink_l, s.shape[:-1] + (1,)), s], -1)
        p = jax.nn.softmax(logits, axis=-1)[..., 1:].astype(v.dtype)
        return jnp.einsum('bkgqs,bskd->bqkgd', p, v)

    out = lax.map(block, qb)
    return out.swapaxes(0, 1).reshape(b, Lc, ATT_HEADS * ATT_HEAD_DIM)


def latent_attention(q, k, v, k_ctx, v_ctx, sink):
    b, L = q.shape[:2]
    nb = L // ATT_BLOCK
    scale = ATT_HEAD_DIM ** -0.5
    qb = q.reshape(b, nb, ATT_BLOCK, ATT_KV_HEADS, ATT_Q_PER_KV, ATT_HEAD_DIM)

    def band(t):
        tp = jnp.pad(t, ((0, 0), (ATT_BLOCK, ATT_BLOCK), (0, 0), (0, 0)))
        tp = tp.reshape(b, nb + 2, ATT_BLOCK, ATT_KV_HEADS, ATT_HEAD_DIM)
        return jnp.concatenate([tp[:, :-2], tp[:, 1:-1], tp[:, 2:]], axis=2)

    kb, vb = band(k), band(v)
    s_loc = jnp.einsum('bnqkgd,bnskd->bnkgqs', qb, kb).astype(jnp.float32) * scale
    s_ctx = jnp.einsum('bnqkgd,bskd->bnkgqs', qb, k_ctx).astype(jnp.float32) * scale
    blk = jnp.arange(nb)[:, None, None]
    qi = jnp.arange(ATT_BLOCK)[None, :, None]
    sj = jnp.arange(3 * ATT_BLOCK)[None, None, :]
    kpos = (blk - 1) * ATT_BLOCK + sj
    valid = (jnp.abs(sj - ATT_BLOCK - qi) <= WINDOW) & (kpos >= 0) & (kpos < L)
    s_loc = jnp.where(valid[None, :, None, None], s_loc, -jnp.inf)
    sink_l = jnp.broadcast_to(sink.astype(jnp.float32).reshape(1, 1, ATT_KV_HEADS, ATT_Q_PER_KV, 1, 1),
                              s_loc.shape[:-1] + (1,))
    p = jax.nn.softmax(jnp.concatenate([sink_l, s_loc, s_ctx], -1), axis=-1).astype(v.dtype)
    n_loc = 3 * ATT_BLOCK
    out = (jnp.einsum('bnkgqs,bnskd->bnqkgd', p[..., 1:1 + n_loc], vb)
           + jnp.einsum('bnkgqs,bskd->bnqkgd', p[..., 1 + n_loc:], v_ctx))
    return out.reshape(b, L, ATT_HEADS * ATT_HEAD_DIM)


def ssd_scan(x, dt, A, Bm, Cm, h0):
    b, L = x.shape[:2]
    nc = L // SSD_CHUNK
    Q, G, Hg = SSD_CHUNK, SSD_GROUPS, SSD_HEADS_PER_GROUP
    xc = x.reshape(b, nc, Q, G, Hg, SSD_HEAD_DIM)
    dtc = dt.reshape(b, nc, Q, G, Hg)
    Bc = Bm.reshape(b, nc, Q, G, SSD_STATE)
    Cc = Cm.reshape(b, nc, Q, G, SSD_STATE)
    cum = jnp.cumsum(dtc * A.reshape(G, Hg), axis=2)
    causal = jnp.tril(jnp.ones((Q, Q), bool))[None, None, :, :, None, None]
    seg = cum[:, :, :, None] - cum[:, :, None, :]
    decay = jnp.exp(jnp.where(causal, seg, -jnp.inf))
    cb = jnp.einsum('bcign,bcjgn->bcijg', Cc, Bc)
    y_diag = jnp.einsum('bcijgh,bcjghp->bcighp', cb[..., None] * decay * dtc[:, :, None], xc)
    w_end = jnp.exp(cum[:, :, -1:] - cum) * dtc
    states = jnp.einsum('bcqgn,bcqgh,bcqghp->bcghpn', Bc, w_end, xc)
    chunk_decay = jnp.exp(cum[:, :, -1])

    def step(h, inp):
        st, dec = inp
        return h * dec[..., None, None] + st, h

    h_init = h0.reshape(b, G, Hg, SSD_HEAD_DIM, SSD_STATE)
    h_fin, h_in = lax.scan(step, h_init, (jnp.moveaxis(states, 1, 0), jnp.moveaxis(chunk_decay, 1, 0)))
    h_in = jnp.moveaxis(h_in, 0, 1)
    y_off = jnp.einsum('bcqgn,bcghpn->bcqghp', Cc, h_in) * jnp.exp(cum)[..., None]
    y = (y_diag + y_off).reshape(b, L, SSD_HEADS, SSD_HEAD_DIM)
    return y, h_fin.reshape(b, SSD_HEADS, SSD_HEAD_DIM, SSD_STATE)


def ssd_mixer(z, xbc, dt_raw, conv_w, conv_b, a_log, dt_bias, d_skip, norm_g, h0):
    b, L = z.shape[:2]
    gn = SSD_GROUPS * SSD_STATE
    xbc = jax.nn.silu(conv_centred(xbc, conv_w, conv_b)).astype(jnp.float32)
    xs = xbc[..., :BRANCH_WIDTH].reshape(b, L, SSD_HEADS, SSD_HEAD_DIM)
    Bm = xbc[..., BRANCH_WIDTH:BRANCH_WIDTH + gn].reshape(b, L, SSD_GROUPS, SSD_STATE)
    Cm = xbc[..., BRANCH_WIDTH + gn:].reshape(b, L, SSD_GROUPS, SSD_STATE)
    dt = jax.nn.softplus(dt_raw.astype(jnp.float32).reshape(b, L, 2, SSD_HEADS) + dt_bias.astype(jnp.float32))
    A = -jnp.exp(a_log.astype(jnp.float32))
    h0 = h0.astype(jnp.float32)
    flip = lambda t: jnp.flip(t, axis=1)
    y_f, s_f = ssd_scan(xs, dt[:, :, 0], A[0], Bm, Cm, h0[:, 0])
    y_b, s_b = ssd_scan(flip(xs), flip(dt[:, :, 1]), A[1], flip(Bm), flip(Cm), h0[:, 1])
    y = y_f + flip(y_b) + d_skip.astype(jnp.float32)[:, None] * xs
    y = y.reshape(b, L, BRANCH_WIDTH) * jax.nn.silu(z.astype(jnp.float32))
    yg = y.reshape(b, L, SSD_GROUPS, BRANCH_WIDTH // SSD_GROUPS)
    yg = yg * lax.rsqrt(jnp.mean(yg * yg, -1, keepdims=True) + EPS)
    y = yg.reshape(b, L, BRANCH_WIDTH) * norm_g
    return y.astype(z.dtype), jnp.stack([s_f, s_b], axis=1)


def linear_combine(left, right):
    a_l, b_l = left
    a_r, b_r = right
    return a_l * a_r, a_r * b_l + b_r


def rglru_mixer(xr, gate, conv_w, conv_b, lam, w_r, b_r, w_i, b_i, h0):
    b, L = xr.shape[:2]
    xf = conv_centred(xr, conv_w, conv_b).astype(jnp.float32)
    xh = xf.reshape(b, L, LRU_HEADS, LRU_BLOCK)
    h0 = h0.astype(jnp.float32)
    outs, finals = [], []
    for d, reverse in ((0, False), (1, True)):
        r = jax.nn.sigmoid(jnp.einsum('blhi,hij->blhj', xh, w_r[d].astype(jnp.float32)).reshape(b, L, BRANCH_WIDTH) + b_r[d])
        i = jax.nn.sigmoid(jnp.einsum('blhi,hij->blhj', xh, w_i[d].astype(jnp.float32)).reshape(b, L, BRANCH_WIDTH) + b_i[d])
        log_a = -LRU_C * r * jax.nn.softplus(-lam[d].astype(jnp.float32))
        a = jnp.exp(log_a)
        u = jnp.sqrt(-jnp.expm1(2.0 * log_a)) * (i * xf)
        edge = -1 if reverse else 0
        u = u.at[:, edge].add(a[:, edge] * h0[:, d])
        _, h = lax.associative_scan(linear_combine, (a, u), reverse=reverse, axis=1)
        outs.append(h)
        finals.append(h[:, 0] if reverse else h[:, -1])
    y = (outs[0] + outs[1]) * jax.nn.gelu(gate.astype(jnp.float32))
    return y.astype(xr.dtype), jnp.stack(finals, axis=1)


def pool_mixer(xp, w_pool, b_pool, scale):
    b, L, _ = xp.shape
    xg = xp.astype(jnp.float32).reshape(b, L, POOL_GROUPS, POOL_GROUP_DIM)
    csum = jnp.concatenate([jnp.zeros_like(xg[:, :1]), jnp.cumsum(xg, axis=1)], axis=1)
    t = np.arange(L)
    means = []
    for gi, win in enumerate(POOL_WINDOWS):
        lo = np.clip(t - win // 2, 0, L)
        hi = np.clip(t + win - win // 2, 0, L)
        total = csum[:, hi, gi] - csum[:, lo, gi]
        means.append(total / (hi - lo).astype(np.float32)[None, :, None])
    pooled = jnp.stack(means, axis=2) - xg
    y = jnp.einsum('blgi,gij->blgj', pooled, w_pool.astype(jnp.float32)) + b_pool.astype(jnp.float32).reshape(POOL_GROUPS, POOL_GROUP_DIM)
    return (y.reshape(b, L, BRANCH_WIDTH) * scale).astype(xp.dtype)


def moe_ffn(h, w_router, router_bias, w_gate, w_up, w_down):
    b, L, D = h.shape
    T = b * L
    x = h.reshape(T, D)
    s = jax.nn.sigmoid(jnp.dot(x, w_router).astype(jnp.float32))
    sel = s + router_bias.astype(jnp.float32)
    grp_score = lax.top_k(sel.reshape(T, N_EXPERT_GROUPS, EXPERTS_PER_GROUP), TOP_K)[0].sum(-1)
    grp = jnp.argmax(grp_score, axis=-1)
    in_grp = (jnp.arange(N_EXPERTS) // EXPERTS_PER_GROUP)[None, :] == grp[:, None]
    _, idx = lax.top_k(jnp.where(in_grp, sel, -jnp.inf), TOP_K)
    wts = jnp.take_along_axis(s, idx, axis=-1)
    wts = wts / jnp.sum(wts, -1, keepdims=True)
    A = T * TOP_K
    e_flat = idx.reshape(A)
    tok_flat = jnp.repeat(jnp.arange(T), TOP_K)
    w_flat = wts.reshape(A)
    order = jnp.argsort(e_flat)
    e_s, tok_s, w_s = e_flat[order], tok_flat[order], w_flat[order]
    counts = jnp.bincount(e_flat, length=N_EXPERTS)
    padded = (counts + MOE_BLOCK - 1) // MOE_BLOCK * MOE_BLOCK
    start = jnp.cumsum(counts) - counts
    pend = jnp.cumsum(padded)
    pstart = pend - padded
    dest = pstart[e_s] + jnp.arange(A) - start[e_s]
    n_blocks = -(-A // MOE_BLOCK) + N_EXPERTS
    buf = jnp.zeros((n_blocks * MOE_BLOCK, D), x.dtype).at[dest].set(x[tok_s])
    block_e = jnp.minimum(jnp.searchsorted(pend, jnp.arange(n_blocks) * MOE_BLOCK, side='right'), N_EXPERTS - 1)

    def expert_block(args):
        xb, e = args
        g = jax.nn.silu(jnp.dot(xb, w_gate[e])) * jnp.dot(xb, w_up[e])
        return jnp.dot(g, w_down[e])

    y_buf = lax.map(expert_block, (buf.reshape(n_blocks, MOE_BLOCK, D), block_e)).reshape(n_blocks * MOE_BLOCK, D)
    out = jnp.zeros((T, D), jnp.float32).at[tok_s].add(w_s[:, None] * y_buf[dest].astype(jnp.float32))
    return out.reshape(b, L, D).astype(h.dtype)


def trunk_layer(x, cond, lw, w_router, router_bias, pos, k_ctx, v_ctx, ssd0, lru0):
    b, L, _ = x.shape
    mod = jnp.dot(jax.nn.silu(cond), lw['w_ada']) + lw['b_ada']
    sh1, sc1, g1, sh2, sc2, g2 = jnp.split(mod[:, None, :], 6, axis=-1)
    h = x * (1 + sc1) + sh1
    q, k, v, z, xbc, dt_raw, xr, gr, xpool = split_in_proj(jnp.dot(h, lw['w_in']))
    q = q.reshape(b, L, ATT_HEADS, ATT_HEAD_DIM)
    k = k.reshape(b, L, ATT_KV_HEADS, ATT_HEAD_DIM)
    v = v.reshape(b, L, ATT_KV_HEADS, ATT_HEAD_DIM)
    if pos is None:
        att = context_attention(q, k, v, lw['attn_sink'])
    else:
        att = latent_attention(axial_rope(q, *pos), axial_rope(k, *pos), v, k_ctx, v_ctx, lw['attn_sink'])
    y_ssd, s_ssd = ssd_mixer(z, xbc, dt_raw, lw['ssd_conv_w'], lw['ssd_conv_b'], lw['ssd_a_log'],
                             lw['ssd_dt_bias'], lw['ssd_d'], lw['ssd_norm_g'], ssd0)
    y_lru, s_lru = rglru_mixer(xr, gr, lw['lru_conv_w'], lw['lru_conv_b'], lw['lru_lambda'],
                               lw['lru_w_r'], lw['lru_b_r'], lw['lru_w_i'], lw['lru_b_i'], lru0)
    y_pool = pool_mixer(xpool, lw['pool_w'], lw['pool_b'], lw['pool_scale'])
    mix = jnp.dot(jnp.concatenate([att, y_ssd, y_lru, y_pool], axis=-1), lw['w_out'])
    x = layer_norm(ALPHA * x + g1 * mix, lw['ln1_g'], lw['ln1_b'])
    h = x * (1 + sc2) + sh2
    ffn = moe_ffn(h, w_router, router_bias, lw['moe_w_gate'], lw['moe_w_up'], lw['moe_w_down'])
    x = layer_norm(ALPHA * x + g2 * ffn, lw['ln2_g'], lw['ln2_b'])
    return x, k, v, s_ssd, s_lru


def setup_inputs(seed: int = 0) -> dict:
    key = jax.random.key(seed)
    keys = iter(jax.random.split(key, 48))

    def normal(shape, std):
        return std * jax.random.normal(next(keys), shape, jnp.float32)

    def uniform(shape, lo, hi):
        return jax.random.uniform(next(keys), shape, jnp.float32, lo, hi)

    dt0 = jnp.exp(uniform((DEPTH, 2, SSD_HEADS), math.log(1e-3), math.log(1e-1)))
    a0 = uniform((DEPTH, 2, BRANCH_WIDTH), 0.9, 0.999)
    s0 = a0 ** (1.0 / LRU_C)
    return {
        'x_prompt': normal((BATCH, SEQ, D_MODEL), 1.0),
        'x_sample': normal((DEC_BATCH, DEC_SEQ, D_MODEL), 1.0),
        'cache_k': normal((DEC_BATCH, DEPTH, PAST_LEN, ATT_KV_HEADS, ATT_HEAD_DIM), 1.0),
        'cache_v': normal((DEC_BATCH, DEPTH, PAST_LEN, ATT_KV_HEADS, ATT_HEAD_DIM), 1.0),
        'state_ssd': normal((DEC_BATCH, DEPTH, 2, SSD_HEADS, SSD_HEAD_DIM, SSD_STATE), 0.1),
        'state_lru': normal((DEC_BATCH, DEPTH, 2, BRANCH_WIDTH), 0.5),
        'c': normal((DEC_BATCH, D_MODEL), 1.0),
        'c_ctx': normal((D_MODEL,), 1.0),
        'w_ada': normal((DEPTH, D_MODEL, 6 * D_MODEL), 0.5 * D_MODEL ** -0.5),
        'b_ada': normal((DEPTH, 6 * D_MODEL), 0.01),
        'w_in': normal((DEPTH, D_MODEL, IN_COLS), D_MODEL ** -0.5),
        'w_out': normal((DEPTH, D_MODEL, D_MODEL), BETA * D_MODEL ** -0.5),
        'attn_sink': normal((DEPTH, ATT_HEADS), 0.5),
        'ssd_conv_w': normal((DEPTH, SSD_CONV, SSD_CONV_CH), SSD_CONV ** -0.5),
        'ssd_conv_b': normal((DEPTH, SSD_CONV_CH), 0.01),
        'ssd_a_log': jnp.log(uniform((DEPTH, 2, SSD_HEADS), 1.0, 16.0)),
        'ssd_dt_bias': dt0 + jnp.log(-jnp.expm1(-dt0)),
        'ssd_d': 1.0 + normal((DEPTH, SSD_HEADS), 0.1),
        'ssd_norm_g': 1.0 + normal((DEPTH, BRANCH_WIDTH), 0.01),
        'lru_conv_w': normal((DEPTH, LRU_CONV, BRANCH_WIDTH), LRU_CONV ** -0.5),
        'lru_conv_b': normal((DEPTH, BRANCH_WIDTH), 0.01),
        'lru_lambda': jnp.log(s0) - jnp.log1p(-s0),
        'lru_w_r': normal((DEPTH, 2, LRU_HEADS, LRU_BLOCK, LRU_BLOCK), LRU_BLOCK ** -0.5),
        'lru_b_r': normal((DEPTH, 2, BRANCH_WIDTH), 0.01),
        'lru_w_i': normal((DEPTH, 2, LRU_HEADS, LRU_BLOCK, LRU_BLOCK), LRU_BLOCK ** -0.5),
        'lru_b_i': normal((DEPTH, 2, BRANCH_WIDTH), 0.01),
        'pool_w': normal((DEPTH, POOL_GROUPS, POOL_GROUP_DIM, POOL_GROUP_DIM), POOL_GROUP_DIM ** -0.5),
        'pool_b': normal((DEPTH, BRANCH_WIDTH), 0.01),
        'pool_scale': 1.0 + normal((DEPTH, BRANCH_WIDTH), 0.1),
        'ln1_g': 1.0 + normal((DEPTH, D_MODEL), 0.01),
        'ln1_b': normal((DEPTH, D_MODEL), 0.01),
        'ln2_g': 1.0 + normal((DEPTH, D_MODEL), 0.01),
        'ln2_b': normal((DEPTH, D_MODEL), 0.01),
        'w_router': normal((D_MODEL, N_EXPERTS), D_MODEL ** -0.5),
        'router_bias': normal((N_EXPERTS,), 0.01),
        'moe_w_gate': normal((DEPTH, N_EXPERTS, D_MODEL, D_FF), D_MODEL ** -0.5),
        'moe_w_up': normal((DEPTH, N_EXPERTS, D_MODEL, D_FF), D_MODEL ** -0.5),
        'moe_w_down': normal((DEPTH, N_EXPERTS, D_FF, D_MODEL), BETA * D_FF ** -0.5),
    }


def reference(x_prompt, x_sample, cache_k, cache_v, state_ssd, state_lru, c, c_ctx,
              w_ada, b_ada, w_in, w_out, attn_sink,
              ssd_conv_w, ssd_conv_b, ssd_a_log, ssd_dt_bias, ssd_d, ssd_norm_g,
              lru_conv_w, lru_conv_b, lru_lambda, lru_w_r, lru_b_r, lru_w_i, lru_b_i,
              pool_w, pool_b, pool_scale, ln1_g, ln1_b, ln2_g, ln2_b,
              w_router, router_bias, moe_w_gate, moe_w_up, moe_w_down):
    n_prompt = x_prompt.shape[0]
    rows = x_sample.shape[1] // GRID_W
    pos = (jnp.repeat(jnp.arange(rows), GRID_W), jnp.tile(jnp.arange(GRID_W), rows))
    zero_ssd = jnp.zeros((n_prompt, 2, SSD_HEADS, SSD_HEAD_DIM, SSD_STATE), jnp.float32)
    zero_lru = jnp.zeros((n_prompt, 2, BRANCH_WIDTH), jnp.float32)
    c_context = c_ctx[None, :]
    xp, xs = x_prompt, x_sample
    ks, vs, ssds, lrus = [], [], [], []
    for l in range(DEPTH):
        lw = {
            'w_ada': w_ada[l], 'b_ada': b_ada[l], 'w_in': w_in[l], 'w_out': w_out[l], 'attn_sink': attn_sink[l],
            'ssd_conv_w': ssd_conv_w[l], 'ssd_conv_b': ssd_conv_b[l], 'ssd_a_log': ssd_a_log[l],
            'ssd_dt_bias': ssd_dt_bias[l], 'ssd_d': ssd_d[l], 'ssd_norm_g': ssd_norm_g[l],
            'lru_conv_w': lru_conv_w[l], 'lru_conv_b': lru_conv_b[l], 'lru_lambda': lru_lambda[l],
            'lru_w_r': lru_w_r[l], 'lru_b_r': lru_b_r[l], 'lru_w_i': lru_w_i[l], 'lru_b_i': lru_b_i[l],
            'pool_w': pool_w[l], 'pool_b': pool_b[l], 'pool_scale': pool_scale[l],
            'ln1_g': ln1_g[l], 'ln1_b': ln1_b[l], 'ln2_g': ln2_g[l], 'ln2_b': ln2_b[l],
            'moe_w_gate': moe_w_gate[l], 'moe_w_up': moe_w_up[l], 'moe_w_down': moe_w_down[l],
        }
        xp, k_l, v_l, s_ssd, s_lru = trunk_layer(xp, c_context, lw, w_router, router_bias, None,
                                                 None, None, zero_ssd, zero_lru)
        ks.append(k_l)
        vs.append(v_l)
        ssds.append(s_ssd)
        lrus.append(s_lru)
        xs = trunk_layer(xs, c, lw, w_router, router_bias, pos,
                         cache_k[:, l], cache_v[:, l], state_ssd[:, l], state_lru[:, l])[0]
    new_cache_k = jnp.stack(ks, axis=1)
    new_cache_v = jnp.stack(vs, axis=1)
    new_state_ssd = jnp.stack(ssds, axis=1)
    new_state_lru = jnp.stack(lrus, axis=1)
    return (xp, xs, new_cache_k, new_cache_v, new_state_ssd, new_state_lru)
```

```python
import functools
import math

import jax
import jax.numpy as jnp
import numpy as np
from jax import lax
from jax.experimental import pallas as pl
from jax.experimental.pallas import tpu as pltpu

_F32 = jnp.float32
_BF16 = jnp.bfloat16

GRID_W = 64
ATT_HEAD_DIM = 128
ATT_BLOCK = 128
ROPE_BASE = 10000.0
SSD_HEAD_DIM = 64
SSD_GROUPS = 2
SSD_CHUNK = 128
LRU_C = 8.0
POOL_WINDOWS = (2, 4, 8, 16)
N_EXPERT_GROUPS = 4
TOP_K = 2
EPS = 1e-5

_LANES = 128
_SUBLANES = 8
_VMEM_BYTES_V7X = 64 * 1024 * 1024
_VMEM_LIMIT = _VMEM_BYTES_V7X * 7 // 8

_ROW_BLOCK = 256
_MOE_TM = 256
_NEG = -1e30


def _call(body, *, grid, in_specs, out_specs, out_shape, scratch=(), nprefetch=0, aliases=None, name=None):
    grid_spec = pltpu.PrefetchScalarGridSpec(
        num_scalar_prefetch=nprefetch, grid=grid, in_specs=in_specs, out_specs=out_specs,
        scratch_shapes=list(scratch))
    return pl.pallas_call(
        body, grid_spec=grid_spec, out_shape=out_shape,
        compiler_params=pltpu.CompilerParams(
            dimension_semantics=("arbitrary",) * len(grid), vmem_limit_bytes=_VMEM_LIMIT),
        input_output_aliases=aliases or {}, name=name)


def _softplus(x):
    return jnp.maximum(x, 0.0) + jnp.log1p(jnp.exp(-jnp.abs(x)))


def _silu(x):
    return x * jax.nn.sigmoid(x)


class _Dims:
    def __init__(self, x_prompt, x_sample, cache_k, state_ssd, lru_w_r, w_router, moe_w_gate, w_ada):
        self.B, self.Lc, self.D = x_prompt.shape
        self.DB, self.Ll, _ = x_sample.shape
        self.depth = w_ada.shape[0]
        self.Tc, self.Tl = self.B * self.Lc, self.DB * self.Ll
        self.T = self.Tc + self.Tl
        self.Bw = self.D // 4
        self.past, self.KV = cache_k.shape[2], cache_k.shape[3]
        self.H = self.Bw // ATT_HEAD_DIM
        self.G = self.H // self.KV
        self.KVw = self.KV * ATT_HEAD_DIM
        self.SH = self.Bw // SSD_HEAD_DIM
        self.N = state_ssd.shape[-1]
        self.GN = SSD_GROUPS * self.N
        self.LH = lru_w_r.shape[2]
        self.E = w_router.shape[1]
        self.F = moe_w_gate.shape[-1]
        Bw, KVw, GN = self.Bw, self.KVw, self.GN
        self.s_q, self.s_z, self.s_xs, self.s_xr, self.s_gr, self.s_xp = (i * Bw for i in range(6))
        self.s_k = 6 * Bw
        self.s_v = self.s_k + KVw
        self.s_B = self.s_v + KVw
        self.s_C = self.s_B + GN
        self.s_dt = self.s_C + GN
        used = self.s_dt + _LANES
        self.slab_w = -(-used // 512) * 512
        assert 2 * self.SH <= _LANES and self.N == _LANES and ATT_HEAD_DIM == _LANES
        assert self.s_k % KVw == 0 and self.s_B % (2 * GN) == 0 and self.Bw % (2 * _LANES) == 0
        assert self.Lc % _ROW_BLOCK == 0 and self.Ll % _ROW_BLOCK == 0
        assert self.Lc % SSD_CHUNK == 0 and self.Ll % SSD_CHUNK == 0 and self.Ll % ATT_BLOCK == 0
        assert self.Bw // self.LH == _LANES and self.Bw % len(POOL_WINDOWS) == 0

    def seg_of_row(self, r0):
        return jnp.where(r0 < self.Tc, 0, 1 + (r0 - self.Tc) // self.Ll)

    def seq_pos(self, r0):
        is_ctx = r0 < self.Tc
        pos = jnp.where(is_ctx, r0 % self.Lc, (r0 - self.Tc) % self.Ll)
        return pos, jnp.where(is_ctx, self.Lc, self.Ll)


def _ada_body(c_ref, w_ref, b_ref, o_ref):
    s = _silu(c_ref[...]).astype(_BF16)
    o_ref[...] = jnp.dot(s, w_ref[...].astype(_BF16), preferred_element_type=_F32) + b_ref[...]


def _ada(cond, w_ada, b_ada):
    depth, D, n6 = w_ada.shape
    rows = cond.shape[0]
    tn = 512
    return _call(
        _ada_body, grid=(depth, n6 // tn),
        in_specs=[pl.BlockSpec((rows, D), lambda l, j: (0, 0)),
                  pl.BlockSpec((None, D, tn), lambda l, j: (l, 0, j)),
                  pl.BlockSpec((None, 1, tn), lambda l, j: (l, 0, j))],
        out_specs=pl.BlockSpec((None, rows, tn), lambda l, j: (l, 0, j)),
        out_shape=jax.ShapeDtypeStruct((depth, rows, n6), _F32), name="ada")(
            cond, w_ada, b_ada.reshape(depth, 1, n6))


def _mod_spec(dm, l, k, tm):
    return pl.BlockSpec((None, None, None, 1, dm.D), lambda i, *_: (l, dm.seg_of_row(i * tm), k, 0, 0))


def _modulate_body(x_ref, sh_ref, sc_ref, o_ref):
    o_ref[...] = (x_ref[...] * (1.0 + sc_ref[...]) + sh_ref[...]).astype(o_ref.dtype)


def _modulate(dm, x, mod5, l):
    tm = 512
    return _call(
        _modulate_body, grid=(dm.T // tm,),
        in_specs=[pl.BlockSpec((tm, dm.D), lambda i: (i, 0)), _mod_spec(dm, l, 0, tm), _mod_spec(dm, l, 1, tm)],
        out_specs=pl.BlockSpec((tm, dm.D), lambda i: (i, 0)),
        out_shape=jax.ShapeDtypeStruct((dm.T, dm.D), _BF16), name="modulate")(x, mod5, mod5)


def _mm_body(x_ref, w_ref, o_ref):
    o_ref[...] = jnp.dot(x_ref[...], w_ref[...], preferred_element_type=_F32).astype(o_ref.dtype)


def _pick_tile(n, candidates):
    for c in candidates:
        if n % c == 0:
            return c
    raise ValueError(f"no tile for {n}")


def _matmul(x, w, out_dtype=_F32):
    M, K = x.shape
    N = w.shape[1]
    tm = _pick_tile(M, (512, 256, 128))
    tn = _pick_tile(N, (1536, 1280, 1024, 768, 512))
    return _call(
        _mm_body, grid=(N // tn, M // tm),
        in_specs=[pl.BlockSpec((tm, K), lambda j, i: (i, 0)), pl.BlockSpec((K, tn), lambda j, i: (0, j))],
        out_specs=pl.BlockSpec((tm, tn), lambda j, i: (i, j)),
        out_shape=jax.ShapeDtypeStruct((M, N), out_dtype), name="in_proj")(x, w)


def _rope_body(q_ref, k_ref, cos_ref, sin_ref, qo_ref, ko_ref):
    cos, sin = cos_ref[...], sin_ref[...]
    quarter = ATT_HEAD_DIM // 4
    lane = lax.broadcasted_iota(jnp.int32, cos.shape, 1)
    first = (lane % (2 * quarter)) < quarter

    def rot(x):
        partner = jnp.where(first, pltpu.roll(x, ATT_HEAD_DIM - quarter, 1), pltpu.roll(x, quarter, 1))
        return x * cos + partner * sin

    for ref, out in ((q_ref, qo_ref), (k_ref, ko_ref)):
        for h in range(ref.shape[1] // ATT_HEAD_DIM):
            sl = slice(h * ATT_HEAD_DIM, (h + 1) * ATT_HEAD_DIM)
            out[:, sl] = rot(ref[:, sl]).astype(out.dtype)


def _rope_tables(L):
    quarter = ATT_HEAD_DIM // 4
    t = jnp.arange(L)
    inv = ROPE_BASE ** (-jnp.arange(quarter, dtype=_F32) / quarter)
    ang_r = (t // GRID_W).astype(_F32)[:, None] * inv[None, :]
    ang_c = (t % GRID_W).astype(_F32)[:, None] * inv[None, :]
    cos = jnp.concatenate([jnp.cos(ang_r)] * 2 + [jnp.cos(ang_c)] * 2, -1)
    sin = jnp.concatenate([-jnp.sin(ang_r), jnp.sin(ang_r), -jnp.sin(ang_c), jnp.sin(ang_c)], -1)
    return cos, sin


def _rope(dm, slab, cos, sin):
    tr = _ROW_BLOCK
    r0, per_seq = dm.Tc // tr, dm.Ll // tr
    return _call(
        _rope_body, grid=(dm.Tl // tr,),
        in_specs=[pl.BlockSpec((tr, dm.Bw), lambda i: (r0 + i, dm.s_q // dm.Bw)),
                  pl.BlockSpec((tr, dm.KVw), lambda i: (r0 + i, dm.s_k // dm.KVw)),
                  pl.BlockSpec((tr, _LANES), lambda i: (i % per_seq, 0)),
                  pl.BlockSpec((tr, _LANES), lambda i: (i % per_seq, 0))],
        out_specs=[pl.BlockSpec((tr, dm.Bw), lambda i: (i, 0)), pl.BlockSpec((tr, dm.KVw), lambda i: (i, 0))],
        out_shape=[jax.ShapeDtypeStruct((dm.Tl, dm.Bw), _BF16), jax.ShapeDtypeStruct((dm.Tl, dm.KVw), _BF16)],
        name="rope")(slab, slab, cos, sin)


def _attend(sink_ref, q, k_all, v_all, bias, o_ref, kv, G):
    Q = q.shape[0]
    q_st = jnp.concatenate([q[:, g * ATT_HEAD_DIM:(g + 1) * ATT_HEAD_DIM] for g in range(G)], axis=0)
    s = lax.dot_general(q_st, k_all, (((1,), (1,)), ((), ())), preferred_element_type=_F32)
    s = s * (ATT_HEAD_DIM ** -0.5)
    if bias is not None:
        s = s + bias
    sk = jnp.concatenate([jnp.full((Q, 1), sink_ref[kv * G + g], _F32) for g in range(G)], axis=0)
    m = jnp.maximum(jnp.max(s, axis=-1, keepdims=True), sk)
    p = jnp.exp(s - m)
    denom = jnp.sum(p, axis=-1, keepdims=True) + jnp.exp(sk - m)
    o = jnp.dot(p.astype(_BF16), v_all, preferred_element_type=_F32) / denom
    for g in range(G):
        h = kv * G + g
        o_ref[:, h * ATT_HEAD_DIM:(h + 1) * ATT_HEAD_DIM] = o[g * Q:(g + 1) * Q].astype(o_ref.dtype)


def _attn_ctx_body(sink_ref, q_ref, k_ref, v_ref, o_ref, *, KV, G):
    for kv in range(KV):
        hs = slice(kv * ATT_HEAD_DIM, (kv + 1) * ATT_HEAD_DIM)
        q = q_ref[:, kv * G * ATT_HEAD_DIM:(kv + 1) * G * ATT_HEAD_DIM].astype(_BF16)
        _attend(sink_ref, q, k_ref[:, hs].astype(_BF16), v_ref[:, hs].astype(_BF16), None, o_ref, kv, G)


def _attn_ctx(dm, slab, sink):
    Lc = dm.Lc
    return _call(
        functools.partial(_attn_ctx_body, KV=dm.KV, G=dm.G), grid=(dm.B,),
        in_specs=[pl.BlockSpec(memory_space=pltpu.SMEM),
                  pl.BlockSpec((Lc, dm.Bw), lambda b: (b, dm.s_q // dm.Bw)),
                  pl.BlockSpec((Lc, dm.KVw), lambda b: (b, dm.s_k // dm.KVw)),
                  pl.BlockSpec((Lc, dm.KVw), lambda b: (b, dm.s_v // dm.KVw))],
        out_specs=pl.BlockSpec((Lc, dm.Bw), lambda b: (b, 0)),
        out_shape=jax.ShapeDtypeStruct((dm.Tc, dm.Bw), _BF16), name="attn_ctx")(sink, slab, slab, slab)


def _attn_lat_body(sink_ref, q_ref, kp_ref, kc_ref, kn_ref, vp_ref, vc_ref, vn_ref, ck_ref, cv_ref,
                   o_ref, *, KV, G, nb):
    n = pl.program_id(1)
    Q = ATT_BLOCK
    S = 3 * Q + ck_ref.shape[0]
    rows = G * Q
    c_i = lax.broadcasted_iota(jnp.int32, (rows, S), 1)
    qi = lax.broadcasted_iota(jnp.int32, (rows, S), 0) % Q
    bad_prev = c_i < jnp.maximum(qi, jnp.where(n == 0, Q, 0))
    bad_next = jnp.logical_and(c_i - 2 * Q > qi - jnp.where(n == nb - 1, Q, 0), c_i < 3 * Q)
    bias = jnp.where(jnp.logical_or(bad_prev, bad_next), _NEG, 0.0)
    for kv in range(KV):
        hs = slice(kv * ATT_HEAD_DIM, (kv + 1) * ATT_HEAD_DIM)
        k_all = jnp.concatenate([kp_ref[:, hs], kc_ref[:, hs], kn_ref[:, hs], ck_ref[:, hs].astype(_BF16)], axis=0)
        v_all = jnp.concatenate([vp_ref[:, hs].astype(_BF16), vc_ref[:, hs].astype(_BF16),
                                 vn_ref[:, hs].astype(_BF16), cv_ref[:, hs].astype(_BF16)], axis=0)
        q = q_ref[:, kv * G * ATT_HEAD_DIM:(kv + 1) * G * ATT_HEAD_DIM]
        _attend(sink_ref, q, k_all, v_all, bias, o_ref, kv, G)


def _attn_lat(dm, slab, qr, kr, cache_k4, cache_v4, sink, l):
    Q = ATT_BLOCK
    nb = dm.Ll // Q
    r0 = dm.Tc // Q
    vcol = dm.s_v // dm.KVw
    prev = lambda n: jnp.maximum(n - 1, 0)
    nxt = lambda n: jnp.minimum(n + 1, nb - 1)
    cache_spec = pl.BlockSpec((None, None, dm.past, dm.KVw), lambda b, n: (b, l, 0, 0))
    return _call(
        functools.partial(_attn_lat_body, KV=dm.KV, G=dm.G, nb=nb), grid=(dm.DB, nb),
        in_specs=[pl.BlockSpec(memory_space=pltpu.SMEM),
                  pl.BlockSpec((Q, dm.Bw), lambda b, n: (b * nb + n, 0)),
                  pl.BlockSpec((Q, dm.KVw), lambda b, n: (b * nb + prev(n), 0)),
                  pl.BlockSpec((Q, dm.KVw), lambda b, n: (b * nb + n, 0)),
                  pl.BlockSpec((Q, dm.KVw), lambda b, n: (b * nb + nxt(n), 0)),
                  pl.BlockSpec((Q, dm.KVw), lambda b, n: (r0 + b * nb + prev(n), vcol)),
                  pl.BlockSpec((Q, dm.KVw), lambda b, n: (r0 + b * nb + n, vcol)),
                  pl.BlockSpec((Q, dm.KVw), lambda b, n: (r0 + b * nb + nxt(n), vcol)),
                  cache_spec, cache_spec],
        out_specs=pl.BlockSpec((Q, dm.Bw), lambda b, n: (b * nb + n, 0)),
        out_shape=jax.ShapeDtypeStruct((dm.Tl, dm.Bw), _BF16), name="attn_lat")(
            sink, qr, kr, kr, kr, slab, slab, slab, cache_k4, cache_v4)


def _halo_specs(dm, cw, colblk):
    R = _ROW_BLOCK
    per = R // _SUBLANES
    last = dm.T // _SUBLANES - 1
    return [pl.BlockSpec((R, cw), lambda i, *_: (i, colblk)),
            pl.BlockSpec((_SUBLANES, cw), lambda i, *_: (jnp.maximum(i * per - 1, 0), colblk)),
            pl.BlockSpec((_SUBLANES, cw), lambda i, *_: (jnp.minimum((i + 1) * per, last), colblk))]


def _fill_padded(dm, pad_ref, x_ref, prev_ref, next_ref):
    R = _ROW_BLOCK
    r0 = pl.program_id(0) * R
    pos, L = dm.seq_pos(r0)
    pad_ref[pl.ds(_SUBLANES, R), :] = x_ref[...].astype(_F32)
    pad_ref[pl.ds(0, _SUBLANES), :] = jnp.where(pos == 0, 0.0, prev_ref[...].astype(_F32))
    pad_ref[pl.ds(_SUBLANES + R, _SUBLANES), :] = jnp.where(pos + R == L, 0.0, next_ref[...].astype(_F32))
    return pos, L


def _conv_body(x_ref, prev_ref, next_ref, w_ref, b_ref, o_ref, pad_ref, *, dm, act):
    _fill_padded(dm, pad_ref, x_ref, prev_ref, next_ref)
    R = _ROW_BLOCK
    width = w_ref.shape[0]
    left = (width - 1) // 2
    out = pad_ref[pl.ds(_SUBLANES - left, R), :] * w_ref[0:1, :]
    for tap in range(1, width):
        out = out + pad_ref[pl.ds(_SUBLANES - left + tap, R), :] * w_ref[tap:tap + 1, :]
    out = out + b_ref[...]
    if act:
        out = _silu(out)
    o_ref[...] = out.astype(o_ref.dtype)


def _conv(dm, slab, col, cw, w, b, act, out_dtype):
    R = _ROW_BLOCK
    width = w.shape[0]
    return _call(
        functools.partial(_conv_body, dm=dm, act=act), grid=(dm.T // R,),
        in_specs=_halo_specs(dm, cw, col // cw) + [pl.BlockSpec((width, cw), lambda i: (0, 0)),
                                                   pl.BlockSpec((1, cw), lambda i: (0, 0))],
        out_specs=pl.BlockSpec((R, cw), lambda i: (i, 0)),
        out_shape=jax.ShapeDtypeStruct((dm.T, cw), out_dtype),
        scratch=[pltpu.VMEM((R + 2 * _SUBLANES, cw), _F32)], name="conv")(slab, slab, slab, w, b.reshape(1, cw))


def _pool_body(x_ref, prev_ref, next_ref, w_ref, b_ref, sc_ref, o_ref, pad_ref, *, dm):
    pos, L = _fill_padded(dm, pad_ref, x_ref, prev_ref, next_ref)
    R = _ROW_BLOCK
    gd = w_ref.shape[1]
    t = pos + lax.broadcasted_iota(jnp.int32, (R, 1), 0)
    for gi, win in enumerate(POOL_WINDOWS):
        cs = slice(gi * gd, (gi + 1) * gd)
        half = win // 2
        total = pad_ref[pl.ds(_SUBLANES - half, R), cs]
        for k in range(1, win):
            total = total + pad_ref[pl.ds(_SUBLANES - half + k, R), cs]
        cnt = (jnp.minimum(t + (win - half), L) - jnp.maximum(t - half, 0)).astype(_F32)
        pooled = total / cnt - pad_ref[pl.ds(_SUBLANES, R), cs]
        y = jnp.dot(pooled.astype(_BF16), w_ref[gi], preferred_element_type=_F32) + b_ref[:, cs]
        o_ref[:, cs] = (y * sc_ref[:, cs]).astype(o_ref.dtype)


def _pool(dm, slab, w, b, scale):
    R = _ROW_BLOCK
    ng, gd, _ = w.shape
    assert max(POOL_WINDOWS) // 2 <= _SUBLANES
    row = pl.BlockSpec((1, dm.Bw), lambda i: (0, 0))
    return _call(
        functools.partial(_pool_body, dm=dm), grid=(dm.T // R,),
        in_specs=_halo_specs(dm, dm.Bw, dm.s_xp // dm.Bw) + [pl.BlockSpec((ng, gd, gd), lambda i: (0, 0, 0)), row, row],
        out_specs=pl.BlockSpec((R, dm.Bw), lambda i: (i, 0)),
        out_shape=jax.ShapeDtypeStruct((dm.T, dm.Bw), _BF16),
        scratch=[pltpu.VMEM((R + 2 * _SUBLANES, dm.Bw), _F32)], name="pool")(
            slab, slab, slab, w.astype(_BF16), b.reshape(1, -1), scale.reshape(1, -1))


def _chunk_pos(dm, ci):
    ncc, ncl = dm.Lc // SSD_CHUNK, dm.Ll // SSD_CHUNK
    nct = dm.Tc // SSD_CHUNK
    is_ctx = ci < nct
    pos = jnp.where(is_ctx, ci % ncc, (ci - nct) % ncl)
    nseq = jnp.where(is_ctx, ncc, ncl)
    lat = jnp.clip((ci - nct) // ncl, 0, dm.DB - 1)
    return is_ctx, pos, nseq, lat


def _ssd_dir(dm, ci, rev, x_ref, bc_ref, dt_ref, bias_ref, a_ref, h0_ref, y_ref, s_ref, h_scr):
    Q, N, SH, P = SSD_CHUNK, dm.N, dm.SH, SSD_HEAD_DIM
    hg = SH // SSD_GROUPS
    gw = hg * P
    is_ctx, pos, nseq, _ = _chunk_pos(dm, ci)
    first = pos == (nseq - 1 if rev else 0)
    last = pos == (0 if rev else nseq - 1)

    @pl.when(jnp.logical_and(first, is_ctx))
    def _():
        h_scr[...] = jnp.zeros_like(h_scr)

    @pl.when(jnp.logical_and(first, jnp.logical_not(is_ctx)))
    def _():
        for k in range(dm.Bw // _LANES):
            h_scr[:, k * _LANES:(k + 1) * _LANES] = h0_ref[k * _LANES:(k + 1) * _LANES, :].T

    lane0 = (SH if rev else 0)
    dt = _softplus(dt_ref[...] + bias_ref[...])
    a = dt * a_ref[...]
    ri = lax.broadcasted_iota(jnp.int32, (Q, Q), 0)
    cj = lax.broadcasted_iota(jnp.int32, (Q, Q), 1)
    tri = (ri <= cj) if rev else (ri >= cj)
    cum = jnp.dot(tri.astype(_F32), a, preferred_element_type=_F32, precision=lax.Precision.HIGHEST)
    cum_t, dt_t = cum.T, dt.T
    ecum = jnp.exp(cum)
    end = 0 if rev else Q - 1
    wend = jnp.exp(cum[end:end + 1, :] - cum) * dt
    lane = lax.broadcasted_iota(jnp.int32, (Q, _LANES), 1)
    low = lane < P

    for g in range(SSD_GROUPS):
        b_g = bc_ref[:, g * N:(g + 1) * N]
        c_g = bc_ref[:, dm.GN + g * N:dm.GN + (g + 1) * N]
        cb = lax.dot_general(c_g, b_g, (((1,), (1,)), ((), ())), preferred_element_type=_F32)
        gs = slice(g * gw, (g + 1) * gw)
        y_diag, e_exp, w_exp = [], [], []
        for p in range(hg // 2):
            xs = slice(g * gw + p * _LANES, g * gw + (p + 1) * _LANES)
            xp = x_ref[:, xs].astype(_BF16)
            ys = []
            for s in range(2):
                ln = lane0 + g * hg + 2 * p + s
                seg = cum[:, ln:ln + 1] - cum_t[ln:ln + 1, :]
                m = cb * jnp.exp(jnp.where(tri, seg, _NEG)) * dt_t[ln:ln + 1, :]
                ys.append(jnp.dot(m.astype(_BF16), xp, preferred_element_type=_F32))
            ln = lane0 + g * hg + 2 * p
            y_diag.append(jnp.where(low, ys[0], ys[1]))
            e_exp.append(jnp.where(low, ecum[:, ln:ln + 1], ecum[:, ln + 1:ln + 2]))
            w_exp.append(jnp.where(low, wend[:, ln:ln + 1], wend[:, ln + 1:ln + 2]))
        y_diag = jnp.concatenate(y_diag, axis=1)
        e_exp = jnp.concatenate(e_exp, axis=1)
        w_exp = jnp.concatenate(w_exp, axis=1)
        h_in = h_scr[:, gs]
        y_off = jnp.dot(c_g, h_in.astype(_BF16), preferred_element_type=_F32) * e_exp
        y_ref[:, gs] = y_diag + y_off
        xw = (x_ref[:, gs] * w_exp).astype(_BF16)
        st = lax.dot_general(b_g, xw, (((0,), (0,)), ((), ())), preferred_element_type=_F32)
        h_scr[:, gs] = h_in * e_exp[end:end + 1, :] + st

    @pl.when(jnp.logical_and(last, is_ctx))
    def _():
        for k in range(dm.Bw // _LANES):
            s_ref[k * _LANES:(k + 1) * _LANES, :] = h_scr[:, k * _LANES:(k + 1) * _LANES].T


def _ssd_body(xf_ref, bcf_ref, dtf_ref, xb_ref, bcb_ref, dtb_ref, bias_ref, a_ref, h0f_ref, h0b_ref,
              yf_ref, yb_ref, sf_ref, sb_ref, hf_scr, hb_scr, *, dm):
    i = pl.program_id(0)
    n = pl.num_programs(0)
    _ssd_dir(dm, i, False, xf_ref, bcf_ref, dtf_ref, bias_ref, a_ref, h0f_ref, yf_ref, sf_ref, hf_scr)
    _ssd_dir(dm, n - 1 - i, True, xb_ref, bcb_ref, dtb_ref, bias_ref, a_ref, h0b_ref, yb_ref, sb_ref, hb_scr)


def _ssd(dm, slab, xs, bc, dt_bias, a_neg, state_ssd5, l):
    Q = SSD_CHUNK
    nchunks = dm.T // Q
    ncc = dm.Lc // Q
    HP = dm.SH * SSD_HEAD_DIM
    dcol = dm.s_dt // _LANES
    fwd = lambda i: i
    bwd = lambda i: nchunks - 1 - i

    def seq_c(ci):
        return jnp.minimum(ci // ncc, dm.B - 1)

    def lat_of(ci):
        return _chunk_pos(dm, ci)[3]

    def specs(order):
        return [pl.BlockSpec((Q, dm.Bw), lambda i: (order(i), 0)),
                pl.BlockSpec((Q, 2 * dm.GN), lambda i: (order(i), 0)),
                pl.BlockSpec((Q, _LANES), lambda i: (order(i), dcol))]

    row = pl.BlockSpec((1, _LANES), lambda i: (0, 0))
    h0 = lambda order, d: pl.BlockSpec((None, None, None, HP, dm.N), lambda i: (lat_of(order(i)), l, d, 0, 0))
    st = lambda order: pl.BlockSpec((None, HP, dm.N), lambda i: (seq_c(order(i)), 0, 0))
    ysd = jax.ShapeDtypeStruct((dm.T, dm.Bw), _F32)
    ssd = jax.ShapeDtypeStruct((dm.B, HP, dm.N), _F32)
    return _call(
        functools.partial(_ssd_body, dm=dm), grid=(nchunks,),
        in_specs=specs(fwd) + specs(bwd) + [row, row, h0(fwd, 0), h0(bwd, 1)],
        out_specs=[pl.BlockSpec((Q, dm.Bw), lambda i: (fwd(i), 0)), pl.BlockSpec((Q, dm.Bw), lambda i: (bwd(i), 0)),
                   st(fwd), st(bwd)],
        out_shape=[ysd, ysd, ssd, ssd],
        scratch=[pltpu.VMEM((dm.N, dm.Bw), _F32), pltpu.VMEM((dm.N, dm.Bw), _F32)], name="ssd")(
            xs, bc, slab, xs, bc, slab, dt_bias, a_neg, state_ssd5, state_ssd5)


def _lru_gates(dm, d, x_ref, wr_ref, wi_ref, br_ref, bi_ref, lam_ref, a_scr, u_scr):
    sp = _softplus(-lam_ref[d:d + 1, :])
    for h in range(dm.LH):
        hs = slice(h * _LANES, (h + 1) * _LANES)
        x = x_ref[:, hs]
        xb = x.astype(_BF16)
        r = jax.nn.sigmoid(jnp.dot(xb, wr_ref[d, h], preferred_element_type=_F32) + br_ref[d:d + 1, hs])
        ig = jax.nn.sigmoid(jnp.dot(xb, wi_ref[d, h], preferred_element_type=_F32) + bi_ref[d:d + 1, hs])
        log_a = -LRU_C * r * sp[:, hs]
        a = jnp.exp(log_a)
        a_scr[:, hs] = a
        u_scr[:, hs] = jnp.sqrt(-jnp.tanh(log_a) * (a * a + 1.0)) * (ig * x)


def _lru_body(xf_ref, xb_ref, wr_ref, wi_ref, br_ref, bi_ref, lam_ref, h0f_ref, h0b_ref,
              hf_ref, hb_ref, finf_ref, finb_ref, af, uf, ab, ub, carry, *, dm):
    R = _ROW_BLOCK
    i = pl.program_id(0)
    n = pl.num_programs(0)
    _lru_gates(dm, 0, xf_ref, wr_ref, wi_ref, br_ref, bi_ref, lam_ref, af, uf)
    _lru_gates(dm, 1, xb_ref, wr_ref, wi_ref, br_ref, bi_ref, lam_ref, ab, ub)

    posf, Lf = dm.seq_pos(i * R)
    posb, Lb = dm.seq_pos((n - 1 - i) * R)
    ctx_f = i * R < dm.Tc
    ctx_b = (n - 1 - i) * R < dm.Tc
    for d, start, is_ctx, h0_ref in ((0, posf == 0, ctx_f, h0f_ref), (1, posb + R == Lb, ctx_b, h0b_ref)):
        @pl.when(jnp.logical_and(start, is_ctx))
        def _():
            carry[d:d + 1, :] = jnp.zeros((1, dm.Bw), _F32)

        @pl.when(jnp.logical_and(start, jnp.logical_not(is_ctx)))
        def _():
            carry[d:d + 1, :] = h0_ref[d:d + 1, :]

    def tile(t, hs):
        hf, hb = hs
        base_f = pl.multiple_of(t * _SUBLANES, _SUBLANES)
        base_b = pl.multiple_of(R - _SUBLANES - t * _SUBLANES, _SUBLANES)
        for r in range(_SUBLANES):
            rf = base_f + r
            hf = af[pl.ds(rf, 1), :] * hf + uf[pl.ds(rf, 1), :]
            hf_ref[pl.ds(rf, 1), :] = hf
            rb = base_b + (_SUBLANES - 1 - r)
            hb = ab[pl.ds(rb, 1), :] * hb + ub[pl.ds(rb, 1), :]
            hb_ref[pl.ds(rb, 1), :] = hb
        return hf, hb

    hf, hb = lax.fori_loop(0, R // _SUBLANES, tile, (carry[0:1, :], carry[1:2, :]))
    carry[0:1, :] = hf
    carry[1:2, :] = hb

    @pl.when(jnp.logical_and(posf + R == Lf, ctx_f))
    def _():
        finf_ref[...] = hf

    @pl.when(jnp.logical_and(posb == 0, ctx_b))
    def _():
        finb_ref[...] = hb


def _lru(dm, xf, w_r, w_i, b_r, b_i, lam, state_lru, l):
    R = _ROW_BLOCK
    nblk = dm.T // R
    fwd = lambda i: i
    bwd = lambda i: nblk - 1 - i
    wspec = pl.BlockSpec((2, dm.LH, _LANES, _LANES), lambda i: (0, 0, 0, 0))
    vspec = pl.BlockSpec((2, dm.Bw), lambda i: (0, 0))
    lat_seq = lambda blk: jnp.clip((blk * R - dm.Tc) // dm.Ll, 0, dm.DB - 1)
    ctx_seq = lambda blk: jnp.minimum(blk * R // dm.Lc, dm.B - 1)
    h0 = lambda order: pl.BlockSpec((None, None, 2, dm.Bw), lambda i: (lat_seq(order(i)), l, 0, 0))
    fin = lambda order: pl.BlockSpec((None, 1, dm.Bw), lambda i: (ctx_seq(order(i)), 0, 0))
    hsd = jax.ShapeDtypeStruct((dm.T, dm.Bw), _F32)
    fsd = jax.ShapeDtypeStruct((dm.B, 1, dm.Bw), _F32)
    return _call(
        functools.partial(_lru_body, dm=dm), grid=(nblk,),
        in_specs=[pl.BlockSpec((R, dm.Bw), lambda i: (fwd(i), 0)), pl.BlockSpec((R, dm.Bw), lambda i: (bwd(i), 0)),
                  wspec, wspec, vspec, vspec, vspec, h0(fwd), h0(bwd)],
        out_specs=[pl.BlockSpec((R, dm.Bw), lambda i: (fwd(i), 0)), pl.BlockSpec((R, dm.Bw), lambda i: (bwd(i), 0)),
                   fin(fwd), fin(bwd)],
        out_shape=[hsd, hsd, fsd, fsd],
        scratch=[pltpu.VMEM((R, dm.Bw), _F32)] * 4 + [pltpu.VMEM((2, dm.Bw), _F32)], name="lru")(
            xf, xf, w_r, w_i, b_r, b_i, lam, state_lru, state_lru)


def _finish_body(yf_ref, yb_ref, xs_ref, z_ref, d_ref, ng_ref, hf_ref, hb_ref, gr_ref, ys_ref, yl_ref):
    y = (yf_ref[...] + yb_ref[...] + d_ref[...] * xs_ref[...]) * _silu(z_ref[...])
    gw = y.shape[1] // SSD_GROUPS
    for g in range(SSD_GROUPS):
        gs = slice(g * gw, (g + 1) * gw)
        yg = y[:, gs]
        yg = yg * lax.rsqrt(jnp.mean(yg * yg, axis=-1, keepdims=True) + EPS)
        ys_ref[:, gs] = (yg * ng_ref[:, gs]).astype(ys_ref.dtype)
    yl_ref[...] = ((hf_ref[...] + hb_ref[...]) * jax.nn.gelu(gr_ref[...], approximate=True)).astype(yl_ref.dtype)


def _finish(dm, slab, yf, yb, xs, d_exp, norm_g, hf, hb):
    R = _ROW_BLOCK
    blk = pl.BlockSpec((R, dm.Bw), lambda i: (i, 0))
    row = pl.BlockSpec((1, dm.Bw), lambda i: (0, 0))
    osd = jax.ShapeDtypeStruct((dm.T, dm.Bw), _BF16)
    return _call(
        _finish_body, grid=(dm.T // R,),
        in_specs=[blk, blk, blk, pl.BlockSpec((R, dm.Bw), lambda i: (i, dm.s_z // dm.Bw)), row, row,
                  blk, blk, pl.BlockSpec((R, dm.Bw), lambda i: (i, dm.s_gr // dm.Bw))],
        out_specs=[blk, blk], out_shape=[osd, osd], name="finish")(
            yf, yb, xs, slab, d_exp, norm_g.reshape(1, -1), hf, hb, slab)


def _layer_norm(r, g, b):
    mu = jnp.mean(r, axis=-1, keepdims=True)
    c = r - mu
    var = jnp.mean(c * c, axis=-1, keepdims=True)
    return c * lax.rsqrt(var + EPS) * g + b


def _split_bf16(x):
    hi = x.astype(_BF16)
    return hi, (x - hi.astype(_F32)).astype(_BF16)


def _outproj_body(a0c, a0l, a1, a2, a3, w_ref, x_ref, g1_ref, lg_ref, lb_ref, sc_ref, sh_ref, wr_ref,
                  x1_ref, h_ref, s_ref, acc, *, alpha, ctx_tiles):
    i = pl.program_id(0)
    k = pl.program_id(1)
    is_ctx = i < ctx_tiles
    branches = ((jnp.logical_and(k == 0, is_ctx), a0c, True), (jnp.logical_and(k == 0, jnp.logical_not(is_ctx)), a0l, True),
                (k == 1, a1, False), (k == 2, a2, False), (k == 3, a3, False))
    for cond, a_ref, init in branches:
        @pl.when(cond)
        def _():
            d = jnp.dot(a_ref[...], w_ref[...], preferred_element_type=_F32)
            if init:
                acc[...] = d
            else:
                acc[...] += d

    @pl.when(k == 3)
    def _():
        x1 = _layer_norm(alpha * x_ref[...] + g1_ref[...] * acc[...], lg_ref[...], lb_ref[...])
        x1_ref[...] = x1
        h = x1 * (1.0 + sc_ref[...]) + sh_ref[...]
        h_ref[...] = h
        hh, hl = _split_bf16(h)
        wh, wl = _split_bf16(wr_ref[...])
        logits = (jnp.dot(hh, wh, preferred_element_type=_F32) + jnp.dot(hh, wl, preferred_element_type=_F32)
                  + jnp.dot(hl, wh, preferred_element_type=_F32))
        s_ref[...] = jax.nn.sigmoid(logits)


def _outproj(dm, branches, w_out, x, mod5, ln_g, ln_b, w_router_pad, l, alpha):
    tm = 256
    D, Bw = dm.D, dm.Bw
    a_spec = pl.BlockSpec((tm, Bw), lambda i, k: (i, 0))
    ctx_tiles = dm.Tc // tm
    att_c = pl.BlockSpec((tm, Bw), lambda i, k: (jnp.minimum(i, ctx_tiles - 1), 0))
    att_l = pl.BlockSpec((tm, Bw), lambda i, k: (jnp.maximum(i - ctx_tiles, 0), 0))
    tile = pl.BlockSpec((tm, D), lambda i, k: (i, 0))
    row = pl.BlockSpec((1, D), lambda i, k: (0, 0))
    return _call(
        functools.partial(_outproj_body, alpha=alpha, ctx_tiles=ctx_tiles), grid=(dm.T // tm, 4),
        in_specs=[att_c, att_l] + [a_spec] * 3 + [pl.BlockSpec((Bw, D), lambda i, k: (k, 0)), tile, _mod_spec(dm, l, 2, tm), row, row,
                                 _mod_spec(dm, l, 4, tm), _mod_spec(dm, l, 3, tm),
                                 pl.BlockSpec((D, _LANES), lambda i, k: (0, 0))],
        out_specs=[tile, tile, pl.BlockSpec((tm, _LANES), lambda i, k: (i, 0))],
        out_shape=[jax.ShapeDtypeStruct((dm.T, D), _F32), jax.ShapeDtypeStruct((dm.T, D), _F32),
                   jax.ShapeDtypeStruct((dm.T, _LANES), _F32)],
        scratch=[pltpu.VMEM((tm, D), _F32)], name="outproj")(
            *branches, w_out, x, mod5, ln_g.reshape(1, D), ln_b.reshape(1, D), mod5, mod5, w_router_pad)


def _route(s, router_bias, tm):
    T, E = s.shape
    per = E // N_EXPERT_GROUPS
    sel = s + router_bias.astype(_F32)
    grp_score = lax.top_k(sel.reshape(T, N_EXPERT_GROUPS, per), TOP_K)[0].sum(-1)
    grp = jnp.argmax(grp_score, axis=-1)
    in_grp = (jnp.arange(E) // per)[None, :] == grp[:, None]
    _, idx = lax.top_k(jnp.where(in_grp, sel, -jnp.inf), TOP_K)
    wts = jnp.take_along_axis(s, idx, axis=-1)
    wts = wts / jnp.sum(wts, -1, keepdims=True)
    A = T * TOP_K
    e_flat = idx.reshape(A).astype(jnp.int32)
    order = jnp.argsort(e_flat).astype(jnp.int32)
    e_s = e_flat[order]
    counts = jnp.bincount(e_flat, length=E).astype(jnp.int32)
    padded = (counts + tm - 1) // tm * tm
    start = jnp.cumsum(counts) - counts
    pend = jnp.cumsum(padded)
    pstart = pend - padded
    dest = (pstart[e_s] + jnp.arange(A, dtype=jnp.int32) - start[e_s]).astype(jnp.int32)
    n_blocks = -(-A // tm) + E
    src_tok = jnp.zeros((n_blocks * tm,), jnp.int32).at[dest].set(order // TOP_K)
    block_e = jnp.minimum(jnp.searchsorted(pend, jnp.arange(n_blocks, dtype=jnp.int32) * tm, side='right'),
                          E - 1).astype(jnp.int32)
    n_used = (pend[-1] // tm).astype(jnp.int32).reshape(1)
    pos = jnp.zeros((A,), jnp.int32).at[order].set(dest)
    return src_tok, block_e, n_used, pos, wts


def _row_copy(src_hbm, row, dst, dst_row, sem):
    return pltpu.make_async_copy(src_hbm.at[pl.ds(row, 1)], dst.at[pl.ds(dst_row, 1)], sem)


def _gather_body(src_ref, h_hbm, o_ref, buf, sem):
    tm = o_ref.shape[0]
    i = pl.program_id(0)
    n = pl.num_programs(0)

    def issue(step, slot):
        def body(r, c):
            _row_copy(h_hbm, src_ref[step * tm + r], buf.at[slot], r, sem.at[slot]).start()
            return c
        lax.fori_loop(0, tm, body, 0)

    @pl.when(i == 0)
    def _():
        issue(0, 0)

    @pl.when(i + 1 < n)
    def _():
        issue(i + 1, (i + 1) % 2)

    slot = i % 2

    def wait(r, c):
        _row_copy(h_hbm, 0, buf.at[slot], r, sem.at[slot]).wait()
        return c
    lax.fori_loop(0, tm, wait, 0)
    o_ref[...] = buf[slot].astype(o_ref.dtype)


def _gather(h, src_tok, tm):
    D = h.shape[1]
    n_blocks = src_tok.shape[0] // tm
    return _call(
        _gather_body, grid=(n_blocks,), nprefetch=1,
        in_specs=[pl.BlockSpec(memory_space=pl.ANY)],
        out_specs=pl.BlockSpec((tm, D), lambda i, src: (i, 0)),
        out_shape=jax.ShapeDtypeStruct((n_blocks * tm, D), _BF16),
        scratch=[pltpu.VMEM((2, tm, D), _F32), pltpu.SemaphoreType.DMA((2,))], name="moe_gather")(src_tok, h)


def _expert_changed(be_ref, m):
    return jnp.logical_or(m == 0, be_ref[m] != be_ref[jnp.maximum(m - 1, 0)])


def _gateup_body(be_ref, nu_ref, x_ref, wg_ref, wu_ref, g_ref, wg_s, wu_s):
    m = pl.program_id(1)

    @pl.when(_expert_changed(be_ref, m))
    def _():
        wg_s[...] = wg_ref[...].astype(_BF16)
        wu_s[...] = wu_ref[...].astype(_BF16)

    @pl.when(m < nu_ref[0])
    def _():
        x = x_ref[...]
        gate = jnp.dot(x, wg_s[...], preferred_element_type=_F32)
        up = jnp.dot(x, wu_s[...], preferred_element_type=_F32)
        g_ref[...] = (_silu(gate) * up).astype(g_ref.dtype)

    @pl.when(m >= nu_ref[0])
    def _():
        g_ref[...] = jnp.zeros_like(g_ref)


def _gateup(xs, block_e, n_used, w_gate, w_up, tm):
    A_pad, D = xs.shape
    F = w_gate.shape[-1]
    tf = _pick_tile(F, (512, 256, 128))
    wspec = pl.BlockSpec((None, D, tf), lambda j, m, be, nu: (be[m], 0, j))
    return _call(
        _gateup_body, grid=(F // tf, A_pad // tm), nprefetch=2,
        in_specs=[pl.BlockSpec((tm, D), lambda j, m, be, nu: (m, 0)), wspec, wspec],
        out_specs=pl.BlockSpec((tm, tf), lambda j, m, be, nu: (m, j)),
        out_shape=jax.ShapeDtypeStruct((A_pad, F), _BF16),
        scratch=[pltpu.VMEM((D, tf), _BF16)] * 2, name="moe_gateup")(block_e, n_used, xs, w_gate, w_up)


def _down_body(be_ref, nu_ref, g_ref, wd_ref, y_ref, wd_s):
    m = pl.program_id(1)

    @pl.when(_expert_changed(be_ref, m))
    def _():
        wd_s[...] = wd_ref[...].astype(_BF16)

    @pl.when(m < nu_ref[0])
    def _():
        y_ref[...] = jnp.dot(g_ref[...], wd_s[...], preferred_element_type=_F32)

    @pl.when(m >= nu_ref[0])
    def _():
        y_ref[...] = jnp.zeros_like(y_ref)


def _down(g, block_e, n_used, w_down, tm):
    A_pad, F = g.shape
    D = w_down.shape[-1]
    tn = _pick_tile(D, (1024, 512, 256, 128))
    return _call(
        _down_body, grid=(D // tn, A_pad // tm), nprefetch=2,
        in_specs=[pl.BlockSpec((tm, F), lambda j, m, be, nu: (m, 0)),
                  pl.BlockSpec((None, F, tn), lambda j, m, be, nu: (be[m], 0, j))],
        out_specs=pl.BlockSpec((tm, tn), lambda j, m, be, nu: (m, j)),
        out_shape=jax.ShapeDtypeStruct((A_pad, D), _F32),
        scratch=[pltpu.VMEM((F, tn), _BF16)], name="moe_down")(block_e, n_used, g, w_down)


def _combine_body(pos_ref, y_hbm, x1_ref, w_ref, g2_ref, lg_ref, lb_ref, sc_ref, sh_ref, x2_ref, hn_ref,
                  buf, sem, *, alpha):
    tm = x1_ref.shape[0]
    i = pl.program_id(0)
    n = pl.num_programs(0)

    def issue(step, slot):
        def body(r, c):
            for k in range(TOP_K):
                _row_copy(y_hbm, pos_ref[(step * tm + r) * TOP_K + k], buf.at[slot, k], r, sem.at[slot]).start()
            return c
        lax.fori_loop(0, tm, body, 0)

    @pl.when(i == 0)
    def _():
        issue(0, 0)

    @pl.when(i + 1 < n)
    def _():
        issue(i + 1, (i + 1) % 2)

    slot = i % 2

    def wait(r, c):
        for k in range(TOP_K):
            _row_copy(y_hbm, 0, buf.at[slot, k], r, sem.at[slot]).wait()
        return c
    lax.fori_loop(0, tm, wait, 0)
    w = w_ref[...]
    ffn = w[:, 0:1] * buf[slot, 0]
    for k in range(1, TOP_K):
        ffn = ffn + w[:, k:k + 1] * buf[slot, k]
    x2 = _layer_norm(alpha * x1_ref[...] + g2_ref[...] * ffn, lg_ref[...], lb_ref[...])
    x2_ref[...] = x2
    hn_ref[...] = (x2 * (1.0 + sc_ref[...]) + sh_ref[...]).astype(hn_ref.dtype)


def _combine(dm, y, pos, wts, x1, mod5, ln_g, ln_b, l, l_next, alpha):
    tm = 256
    D = dm.D
    tile = pl.BlockSpec((tm, D), lambda i, p: (i, 0))
    row = pl.BlockSpec((1, D), lambda i, p: (0, 0))
    return _call(
        functools.partial(_combine_body, alpha=alpha), grid=(dm.T // tm,), nprefetch=1,
        in_specs=[pl.BlockSpec(memory_space=pl.ANY), tile, pl.BlockSpec((tm, TOP_K), lambda i, p: (i, 0)),
                  _mod_spec(dm, l, 5, tm), row, row, _mod_spec(dm, l_next, 1, tm), _mod_spec(dm, l_next, 0, tm)],
        out_specs=[tile, tile],
        out_shape=[jax.ShapeDtypeStruct((dm.T, D), _F32), jax.ShapeDtypeStruct((dm.T, D), _BF16)],
        scratch=[pltpu.VMEM((2, TOP_K, tm, D), _F32), pltpu.SemaphoreType.DMA((2,))], name="moe_combine")(
            pos, y, x1, wts, mod5, ln_g.reshape(1, D), ln_b.reshape(1, D), mod5, mod5)


def kernel(x_prompt, x_sample, cache_k, cache_v, state_ssd, state_lru, c, c_ctx, w_ada, b_ada, w_in, w_out, attn_sink, ssd_conv_w, ssd_conv_b, ssd_a_log, ssd_dt_bias, ssd_d, ssd_norm_g, lru_conv_w, lru_conv_b, lru_lambda, lru_w_r, lru_b_r, lru_w_i, lru_b_i, pool_w, pool_b, pool_scale, ln1_g, ln1_b, ln2_g, ln2_b, w_router, router_bias, moe_w_gate, moe_w_up, moe_w_down):
    dm = _Dims(x_prompt, x_sample, cache_k, state_ssd, lru_w_r, w_router, moe_w_gate, w_ada)
    depth, D, Bw, KVw, GN, SH = dm.depth, dm.D, dm.Bw, dm.KVw, dm.GN, dm.SH
    alpha = (2 * depth) ** 0.25

    rows = -(-(1 + dm.DB) // _SUBLANES) * _SUBLANES
    cond = jnp.zeros((rows, D), _F32).at[0].set(c_ctx).at[1:1 + dm.DB].set(c)
    mod5 = _ada(cond, w_ada, b_ada).reshape(depth, rows, 6, 1, D)

    x = jnp.concatenate([x_prompt.reshape(dm.Tc, D), x_sample.reshape(dm.Tl, D)], axis=0)
    h = _modulate(dm, x, mod5, 0)

    cos, sin = _rope_tables(dm.Ll)
    cache_k4 = cache_k.reshape(dm.DB, depth, dm.past, KVw)
    cache_v4 = cache_v.reshape(dm.DB, depth, dm.past, KVw)
    state_ssd5 = state_ssd.reshape(dm.DB, depth, 2, SH * SSD_HEAD_DIM, dm.N)
    w_router_pad = jnp.zeros((D, _LANES), _F32).at[:, :dm.E].set(w_router)

    o_k, o_v, o_z = Bw, Bw + KVw, Bw + 2 * KVw
    o_xs = o_z + Bw
    o_B = o_xs + Bw
    o_dt = o_B + 2 * GN
    o_xr = o_dt + 2 * SH
    o_gr, o_xp = o_xr + Bw, o_xr + 2 * Bw

    ks, vs, ssds, lrus = [], [], [], []
    for l in range(depth):
        wi = w_in[l]
        cols = lambda o, w: wi[:, o:o + w]
        w_slab = jnp.concatenate(
            [cols(0, Bw), cols(o_z, Bw), cols(o_xs, Bw), cols(o_xr, Bw), cols(o_gr, Bw), cols(o_xp, Bw),
             cols(o_k, KVw), cols(o_v, KVw), cols(o_B, 2 * GN), cols(o_dt, 2 * SH),
             jnp.zeros((D, dm.slab_w - dm.s_dt - 2 * SH), wi.dtype)], axis=1).astype(_BF16)
        slab = _matmul(h, w_slab)

        att_c = _attn_ctx(dm, slab, attn_sink[l])
        qr, kr = _rope(dm, slab, cos, sin)
        att_l = _attn_lat(dm, slab, qr, kr, cache_k4, cache_v4, attn_sink[l], l)

        cw = ssd_conv_w[l]
        xs = _conv(dm, slab, dm.s_xs, Bw, cw[:, :Bw], ssd_conv_b[l][:Bw], True, _F32)
        bc = _conv(dm, slab, dm.s_B, 2 * GN, cw[:, Bw:], ssd_conv_b[l][Bw:], True, _BF16)
        pad = jnp.zeros((_LANES - 2 * SH,), _F32)
        dt_bias = jnp.concatenate([ssd_dt_bias[l].reshape(-1), pad]).reshape(1, _LANES)
        a_neg = jnp.concatenate([-jnp.exp(ssd_a_log[l].reshape(-1)), pad]).reshape(1, _LANES)
        yf, yb, s_f, s_b = _ssd(dm, slab, xs, bc, dt_bias, a_neg, state_ssd5, l)

        xf = _conv(dm, slab, dm.s_xr, Bw, lru_conv_w[l], lru_conv_b[l], False, _F32)
        hf, hb, fin_f, fin_b = _lru(dm, xf, lru_w_r[l].astype(_BF16), lru_w_i[l].astype(_BF16),
                                    lru_b_r[l], lru_b_i[l], lru_lambda[l], state_lru, l)

        d_exp = jnp.repeat(ssd_d[l], SSD_HEAD_DIM).reshape(1, Bw)
        y_ssd, y_lru = _finish(dm, slab, yf, yb, xs, d_exp, ssd_norm_g[l], hf, hb)
        y_pool = _pool(dm, slab, pool_w[l], pool_b[l], pool_scale[l])

        x1, h2, s_pad = _outproj(dm, (att_c, att_l, y_ssd, y_lru, y_pool), w_out[l].astype(_BF16), x, mod5,
                                 ln1_g[l], ln1_b[l], w_router_pad, l, alpha)

        src_tok, block_e, n_used, pos, wts = _route(s_pad[:, :dm.E], router_bias, _MOE_TM)
        xs_sorted = _gather(h2, src_tok, _MOE_TM)
        g = _gateup(xs_sorted, block_e, n_used, moe_w_gate[l], moe_w_up[l], _MOE_TM)
        y = _down(g, block_e, n_used, moe_w_down[l], _MOE_TM)
        x, h = _combine(dm, y, pos, wts, x1, mod5, ln2_g[l], ln2_b[l], l, min(l + 1, depth - 1), alpha)

        ks.append(slab[:dm.Tc, dm.s_k:dm.s_k + KVw].reshape(dm.B, dm.Lc, dm.KV, ATT_HEAD_DIM))
        vs.append(slab[:dm.Tc, dm.s_v:dm.s_v + KVw].reshape(dm.B, dm.Lc, dm.KV, ATT_HEAD_DIM))
        ssds.append(jnp.stack([s_f, s_b], axis=1).reshape(dm.B, 2, SH, SSD_HEAD_DIM, dm.N))
        lrus.append(jnp.concatenate([fin_f, fin_b], axis=1))

    y_prompt = x[:dm.Tc].reshape(dm.B, dm.Lc, D)
    y_sample = x[dm.Tc:].reshape(dm.DB, dm.Ll, D)
    return (y_prompt, y_sample, jnp.stack(ks, axis=1), jnp.stack(vs, axis=1),
            jnp.stack(ssds, axis=1), jnp.stack(lrus, axis=1))
```

```python
import functools
import math

import jax
import jax.numpy as jnp
import numpy as np
from jax import lax
from jax.experimental import pallas as pl
from jax.experimental.pallas import tpu as pltpu

_F32 = jnp.float32
_BF16 = jnp.bfloat16

GRID_W = 64
ATT_HEAD_DIM = 128
ATT_BLOCK = 128
ROPE_BASE = 10000.0
SSD_HEAD_DIM = 64
SSD_GROUPS = 2
SSD_CHUNK = 128
LRU_C = 8.0
POOL_WINDOWS = (2, 4, 8, 16)
N_EXPERT_GROUPS = 4
TOP_K = 2
EPS = 1e-5

_LANES = 128
_SUBLANES = 8
_VMEM_BYTES_V7X = 64 * 1024 * 1024
_VMEM_LIMIT = _VMEM_BYTES_V7X * 7 // 8

_ROW_BLOCK = 256
_MOE_TM = 256
_NEG = -1e30


def _call(body, *, grid, in_specs, out_specs, out_shape, scratch=(), nprefetch=0, aliases=None, name=None):
    grid_spec = pltpu.PrefetchScalarGridSpec(
        num_scalar_prefetch=nprefetch, grid=grid, in_specs=in_specs, out_specs=out_specs,
        scratch_shapes=list(scratch))
    return pl.pallas_call(
        body, grid_spec=grid_spec, out_shape=out_shape,
        compiler_params=pltpu.CompilerParams(
            dimension_semantics=("arbitrary",) * len(grid), vmem_limit_bytes=_VMEM_LIMIT),
        input_output_aliases=aliases or {}, name=name)


def _softplus(x):
    return jnp.maximum(x, 0.0) + jnp.log1p(jnp.exp(-jnp.abs(x)))


def _silu(x):
    return x * jax.nn.sigmoid(x)


class _Dims:
    def __init__(self, x_prompt, x_sample, cache_k, state_ssd, lru_w_r, w_router, moe_w_gate, w_ada):
        self.B, self.Lc, self.D = x_prompt.shape
        self.DB, self.Ll, _ = x_sample.shape
        self.depth = w_ada.shape[0]
        self.Tc, self.Tl = self.B * self.Lc, self.DB * self.Ll
        self.T = self.Tc + self.Tl
        self.Bw = self.D // 4
        self.past, self.KV = cache_k.shape[2], cache_k.shape[3]
        self.H = self.Bw // ATT_HEAD_DIM
        self.G = self.H // self.KV
        self.KVw = self.KV * ATT_HEAD_DIM
        self.SH = self.Bw // SSD_HEAD_DIM
        self.N = state_ssd.shape[-1]
        self.GN = SSD_GROUPS * self.N
        self.LH = lru_w_r.shape[2]
        self.E = w_router.shape[1]
        self.F = moe_w_gate.shape[-1]
        Bw, KVw, GN = self.Bw, self.KVw, self.GN
        self.s_q, self.s_z, self.s_xs, self.s_xr, self.s_gr, self.s_xp = (i * Bw for i in range(6))
        self.s_k = 6 * Bw
        self.s_v = self.s_k + KVw
        self.s_B = self.s_v + KVw
        self.s_C = self.s_B + GN
        self.s_dt = self.s_C + GN
        used = self.s_dt + _LANES
        self.slab_w = -(-used // 512) * 512
        assert 2 * self.SH <= _LANES and self.N == _LANES and ATT_HEAD_DIM == _LANES
        assert self.s_k % KVw == 0 and self.s_B % (2 * GN) == 0 and self.Bw % (2 * _LANES) == 0
        assert self.Lc % _ROW_BLOCK == 0 and self.Ll % _ROW_BLOCK == 0
        assert self.Lc % SSD_CHUNK == 0 and self.Ll % SSD_CHUNK == 0 and self.Ll % ATT_BLOCK == 0
        assert self.Bw // self.LH == _LANES and self.Bw % len(POOL_WINDOWS) == 0

    def seg_of_row(self, r0):
        return jnp.where(r0 < self.Tc, 0, 1 + (r0 - self.Tc) // self.Ll)

    def seq_pos(self, r0):
        is_ctx = r0 < self.Tc
        pos = jnp.where(is_ctx, r0 % self.Lc, (r0 - self.Tc) % self.Ll)
        return pos, jnp.where(is_ctx, self.Lc, self.Ll)


def _ada_body(c_ref, w_ref, b_ref, o_ref):
    s = _silu(c_ref[...]).astype(_BF16)
    o_ref[...] = jnp.dot(s, w_ref[...].astype(_BF16), preferred_element_type=_F32) + b_ref[...]


def _ada(cond, w_ada, b_ada):
    depth, D, n6 = w_ada.shape
    rows = cond.shape[0]
    tn = 512
    return _call(
        _ada_body, grid=(depth, n6 // tn),
        in_specs=[pl.BlockSpec((rows, D), lambda l, j: (0, 0)),
                  pl.BlockSpec((None, D, tn), lambda l, j: (l, 0, j)),
                  pl.BlockSpec((None, 1, tn), lambda l, j: (l, 0, j))],
        out_specs=pl.BlockSpec((None, rows, tn), lambda l, j: (l, 0, j)),
        out_shape=jax.ShapeDtypeStruct((depth, rows, n6), _F32), name="ada")(
            cond, w_ada, b_ada.reshape(depth, 1, n6))


def _mod_spec(dm, l, k, tm):
    return pl.BlockSpec((None, None, None, 1, dm.D), lambda i, *_: (l, dm.seg_of_row(i * tm), k, 0, 0))


def _modulate_body(x_ref, sh_ref, sc_ref, o_ref):
    o_ref[...] = (x_ref[...] * (1.0 + sc_ref[...]) + sh_ref[...]).astype(o_ref.dtype)


def _modulate(dm, x, mod5, l):
    tm = 512
    return _call(
        _modulate_body, grid=(dm.T // tm,),
        in_specs=[pl.BlockSpec((tm, dm.D), lambda i: (i, 0)), _mod_spec(dm, l, 0, tm), _mod_spec(dm, l, 1, tm)],
        out_specs=pl.BlockSpec((tm, dm.D), lambda i: (i, 0)),
        out_shape=jax.ShapeDtypeStruct((dm.T, dm.D), _BF16), name="modulate")(x, mod5, mod5)


def _mm_body(x_ref, w_ref, o_ref):
    o_ref[...] = jnp.dot(x_ref[...], w_ref[...], preferred_element_type=_F32).astype(o_ref.dtype)


def _pick_tile(n, candidates):
    for c in candidates:
        if n % c == 0:
            return c
    raise ValueError(f"no tile for {n}")


def _matmul(x, w, out_dtype=_F32):
    M, K = x.shape
    N = w.shape[1]
    tm = _pick_tile(M, (512, 256, 128))
    tn = _pick_tile(N, (1536, 1280, 1024, 768, 512))
    return _call(
        _mm_body, grid=(N // tn, M // tm),
        in_specs=[pl.BlockSpec((tm, K), lambda j, i: (i, 0)), pl.BlockSpec((K, tn), lambda j, i: (0, j))],
        out_specs=pl.BlockSpec((tm, tn), lambda j, i: (i, j)),
        out_shape=jax.ShapeDtypeStruct((M, N), out_dtype), name="in_proj")(x, w)


def _rope_body(q_ref, k_ref, cos_ref, sin_ref, qo_ref, ko_ref):
    cos, sin = cos_ref[...], sin_ref[...]
    quarter = ATT_HEAD_DIM // 4
    lane = lax.broadcasted_iota(jnp.int32, cos.shape, 1)
    first = (lane % (2 * quarter)) < quarter

    def rot(x):
        partner = jnp.where(first, pltpu.roll(x, ATT_HEAD_DIM - quarter, 1), pltpu.roll(x, quarter, 1))
        return x * cos + partner * sin

    for ref, out in ((q_ref, qo_ref), (k_ref, ko_ref)):
        for h in range(ref.shape[1] // ATT_HEAD_DIM):
            sl = slice(h * ATT_HEAD_DIM, (h + 1) * ATT_HEAD_DIM)
            out[:, sl] = rot(ref[:, sl]).astype(out.dtype)


def _rope_tables(L):
    quarter = ATT_HEAD_DIM // 4
    t = jnp.arange(L)
    inv = ROPE_BASE ** (-jnp.arange(quarter, dtype=_F32) / quarter)
    ang_r = (t // GRID_W).astype(_F32)[:, None] * inv[None, :]
    ang_c = (t % GRID_W).astype(_F32)[:, None] * inv[None, :]
    cos = jnp.concatenate([jnp.cos(ang_r)] * 2 + [jnp.cos(ang_c)] * 2, -1)
    sin = jnp.concatenate([-jnp.sin(ang_r), jnp.sin(ang_r), -jnp.sin(ang_c), jnp.sin(ang_c)], -1)
    return cos, sin


def _rope(dm, slab, cos, sin):
    tr = _ROW_BLOCK
    r0, per_seq = dm.Tc // tr, dm.Ll // tr
    return _call(
        _rope_body, grid=(dm.Tl // tr,),
        in_specs=[pl.BlockSpec((tr, dm.Bw), lambda i: (r0 + i, dm.s_q // dm.Bw)),
                  pl.BlockSpec((tr, dm.KVw), lambda i: (r0 + i, dm.s_k // dm.KVw)),
                  pl.BlockSpec((tr, _LANES), lambda i: (i % per_seq, 0)),
                  pl.BlockSpec((tr, _LANES), lambda i: (i % per_seq, 0))],
        out_specs=[pl.BlockSpec((tr, dm.Bw), lambda i: (i, 0)), pl.BlockSpec((tr, dm.KVw), lambda i: (i, 0))],
        out_shape=[jax.ShapeDtypeStruct((dm.Tl, dm.Bw), _BF16), jax.ShapeDtypeStruct((dm.Tl, dm.KVw), _BF16)],
        name="rope")(slab, slab, cos, sin)


def _attend(sink_ref, q, k_all, v_all, bias, o_ref, kv, G):
    Q = q.shape[0]
    q_st = jnp.concatenate([q[:, g * ATT_HEAD_DIM:(g + 1) * ATT_HEAD_DIM] for g in range(G)], axis=0)
    s = lax.dot_general(q_st, k_all, (((1,), (1,)), ((), ())), preferred_element_type=_F32)
    s = s * (ATT_HEAD_DIM ** -0.5)
    if bias is not None:
        s = s + bias
    sk = jnp.concatenate([jnp.full((Q, 1), sink_ref[kv * G + g], _F32) for g in range(G)], axis=0)
    m = jnp.maximum(jnp.max(s, axis=-1, keepdims=True), sk)
    p = jnp.exp(s - m)
    denom = jnp.sum(p, axis=-1, keepdims=True) + jnp.exp(sk - m)
    o = jnp.dot(p.astype(_BF16), v_all, preferred_element_type=_F32) / denom
    for g in range(G):
        h = kv * G + g
        o_ref[:, h * ATT_HEAD_DIM:(h + 1) * ATT_HEAD_DIM] = o[g * Q:(g + 1) * Q].astype(o_ref.dtype)


def _attn_ctx_body(sink_ref, q_ref, k_ref, v_ref, o_ref, *, KV, G):
    for kv in range(KV):
        hs = slice(kv * ATT_HEAD_DIM, (kv + 1) * ATT_HEAD_DIM)
        q = q_ref[:, kv * G * ATT_HEAD_DIM:(kv + 1) * G * ATT_HEAD_DIM].astype(_BF16)
        _attend(sink_ref, q, k_ref[:, hs].astype(_BF16), v_ref[:, hs].astype(_BF16), None, o_ref, kv, G)


def _attn_ctx(dm, slab, sink):
    Lc = dm.Lc
    return _call(
        functools.partial(_attn_ctx_body, KV=dm.KV, G=dm.G), grid=(dm.B,),
        in_specs=[pl.BlockSpec(memory_space=pltpu.SMEM),
                  pl.BlockSpec((Lc, dm.Bw), lambda b: (b, dm.s_q // dm.Bw)),
                  pl.BlockSpec((Lc, dm.KVw), lambda b: (b, dm.s_k // dm.KVw)),
                  pl.BlockSpec((Lc, dm.KVw), lambda b: (b, dm.s_v // dm.KVw))],
        out_specs=pl.BlockSpec((Lc, dm.Bw), lambda b: (b, 0)),
        out_shape=jax.ShapeDtypeStruct((dm.Tc, dm.Bw), _BF16), name="attn_ctx")(sink, slab, slab, slab)


def _attn_lat_body(sink_ref, q_ref, kp_ref, kc_ref, kn_ref, vp_ref, vc_ref, vn_ref, ck_ref, cv_ref,
                   o_ref, *, KV, G, nb):
    n = pl.program_id(1)
    Q = ATT_BLOCK
    S = 3 * Q + ck_ref.shape[0]
    rows = G * Q
    c_i = lax.broadcasted_iota(jnp.int32, (rows, S), 1)
    qi = lax.broadcasted_iota(jnp.int32, (rows, S), 0) % Q
    bad_prev = c_i < jnp.maximum(qi, jnp.where(n == 0, Q, 0))
    bad_next = jnp.logical_and(c_i - 2 * Q > qi - jnp.where(n == nb - 1, Q, 0), c_i < 3 * Q)
    bias = jnp.where(jnp.logical_or(bad_prev, bad_next), _NEG, 0.0)
    for kv in range(KV):
        hs = slice(kv * ATT_HEAD_DIM, (kv + 1) * ATT_HEAD_DIM)
        k_all = jnp.concatenate([kp_ref[:, hs], kc_ref[:, hs], kn_ref[:, hs], ck_ref[:, hs].astype(_BF16)], axis=0)
        v_all = jnp.concatenate([vp_ref[:, hs].astype(_BF16), vc_ref[:, hs].astype(_BF16),
                                 vn_ref[:, hs].astype(_BF16), cv_ref[:, hs].astype(_BF16)], axis=0)
        q = q_ref[:, kv * G * ATT_HEAD_DIM:(kv + 1) * G * ATT_HEAD_DIM]
        _attend(sink_ref, q, k_all, v_all, bias, o_ref, kv, G)


def _attn_lat(dm, slab, qr, kr, cache_k4, cache_v4, sink, l):
    Q = ATT_BLOCK
    nb = dm.Ll // Q
    r0 = dm.Tc // Q
    vcol = dm.s_v // dm.KVw
    prev = lambda n: jnp.maximum(n - 1, 0)
    nxt = lambda n: jnp.minimum(n + 1, nb - 1)
    cache_spec = pl.BlockSpec((None, None, dm.past, dm.KVw), lambda b, n: (b, l, 0, 0))
    return _call(
        functools.partial(_attn_lat_body, KV=dm.KV, G=dm.G, nb=nb), grid=(dm.DB, nb),
        in_specs=[pl.BlockSpec(memory_space=pltpu.SMEM),
                  pl.BlockSpec((Q, dm.Bw), lambda b, n: (b * nb + n, 0)),
                  pl.BlockSpec((Q, dm.KVw), lambda b, n: (b * nb + prev(n), 0)),
                  pl.BlockSpec((Q, dm.KVw), lambda b, n: (b * nb + n, 0)),
                  pl.BlockSpec((Q, dm.KVw), lambda b, n: (b * nb + nxt(n), 0)),
                  pl.BlockSpec((Q, dm.KVw), lambda b, n: (r0 + b * nb + prev(n), vcol)),
                  pl.BlockSpec((Q, dm.KVw), lambda b, n: (r0 + b * nb + n, vcol)),
                  pl.BlockSpec((Q, dm.KVw), lambda b, n: (r0 + b * nb + nxt(n), vcol)),
                  cache_spec, cache_spec],
        out_specs=pl.BlockSpec((Q, dm.Bw), lambda b, n: (b * nb + n, 0)),
        out_shape=jax.ShapeDtypeStruct((dm.Tl, dm.Bw), _BF16), name="attn_lat")(
            sink, qr, kr, kr, kr, slab, slab, slab, cache_k4, cache_v4)


def _halo_specs(dm, cw, colblk):
    R = _ROW_BLOCK
    per = R // _SUBLANES
    last = dm.T // _SUBLANES - 1
    return [pl.BlockSpec((R, cw), lambda i, *_: (i, colblk)),
            pl.BlockSpec((_SUBLANES, cw), lambda i, *_: (jnp.maximum(i * per - 1, 0), colblk)),
            pl.BlockSpec((_SUBLANES, cw), lambda i, *_: (jnp.minimum((i + 1) * per, last), colblk))]


def _fill_padded(dm, pad_ref, x_ref, prev_ref, next_ref):
    R = _ROW_BLOCK
    r0 = pl.program_id(0) * R
    pos, L = dm.seq_pos(r0)
    pad_ref[pl.ds(_SUBLANES, R), :] = x_ref[...].astype(_F32)
    pad_ref[pl.ds(0, _SUBLANES), :] = jnp.where(pos == 0, 0.0, prev_ref[...].astype(_F32))
    pad_ref[pl.ds(_SUBLANES + R, _SUBLANES), :] = jnp.where(pos + R == L, 0.0, next_ref[...].astype(_F32))
    return pos, L


def _conv_body(x_ref, prev_ref, next_ref, w_ref, b_ref, o_ref, pad_ref, *, dm, act):
    _fill_padded(dm, pad_ref, x_ref, prev_ref, next_ref)
    R = _ROW_BLOCK
    width = w_ref.shape[0]
    left = (width - 1) // 2
    out = pad_ref[pl.ds(_SUBLANES - left, R), :] * w_ref[0:1, :]
    for tap in range(1, width):
        out = out + pad_ref[pl.ds(_SUBLANES - left + tap, R), :] * w_ref[tap:tap + 1, :]
    out = out + b_ref[...]
    if act:
        out = _silu(out)
    o_ref[...] = out.astype(o_ref.dtype)


def _conv(dm, slab, col, cw, w, b, act, out_dtype):
    R = _ROW_BLOCK
    width = w.shape[0]
    return _call(
        functools.partial(_conv_body, dm=dm, act=act), grid=(dm.T // R,),
        in_specs=_halo_specs(dm, cw, col // cw) + [pl.BlockSpec((width, cw), lambda i: (0, 0)),
                                                   pl.BlockSpec((1, cw), lambda i: (0, 0))],
        out_specs=pl.BlockSpec((R, cw), lambda i: (i, 0)),
        out_shape=jax.ShapeDtypeStruct((dm.T, cw), out_dtype),
        scratch=[pltpu.VMEM((R + 2 * _SUBLANES, cw), _F32)], name="conv")(slab, slab, slab, w, b.reshape(1, cw))


def _pool_body(x_ref, prev_ref, next_ref, w_ref, b_ref, sc_ref, o_ref, pad_ref, *, dm):
    pos, L = _fill_padded(dm, pad_ref, x_ref, prev_ref, next_ref)
    R = _ROW_BLOCK
    gd = w_ref.shape[1]
    t = pos + lax.broadcasted_iota(jnp.int32, (R, 1), 0)
    for gi, win in enumerate(POOL_WINDOWS):
        cs = slice(gi * gd, (gi + 1) * gd)
        half = win // 2
        total = pad_ref[pl.ds(_SUBLANES - half, R), cs]
        for k in range(1, win):
            total = total + pad_ref[pl.ds(_SUBLANES - half + k, R), cs]
        cnt = (jnp.minimum(t + (win - half), L) - jnp.maximum(t - half, 0)).astype(_F32)
        pooled = total / cnt - pad_ref[pl.ds(_SUBLANES, R), cs]
        y = jnp.dot(pooled.astype(_BF16), w_ref[gi], preferred_element_type=_F32) + b_ref[:, cs]
        o_ref[:, cs] = (y * sc_ref[:, cs]).astype(o_ref.dtype)


def _pool(dm, slab, w, b, scale):
    R = _ROW_BLOCK
    ng, gd, _ = w.shape
    assert max(POOL_WINDOWS) // 2 <= _SUBLANES
    row = pl.BlockSpec((1, dm.Bw), lambda i: (0, 0))
    return _call(
        functools.partial(_pool_body, dm=dm), grid=(dm.T // R,),
        in_specs=_halo_specs(dm, dm.Bw, dm.s_xp // dm.Bw) + [pl.BlockSpec((ng, gd, gd), lambda i: (0, 0, 0)), row, row],
        out_specs=pl.BlockSpec((R, dm.Bw), lambda i: (i, 0)),
        out_shape=jax.ShapeDtypeStruct((dm.T, dm.Bw), _BF16),
        scratch=[pltpu.VMEM((R + 2 * _SUBLANES, dm.Bw), _F32)], name="pool")(
            slab, slab, slab, w.astype(_BF16), b.reshape(1, -1), scale.reshape(1, -1))


def _chunk_pos(dm, ci):
    ncc, ncl = dm.Lc // SSD_CHUNK, dm.Ll // SSD_CHUNK
    nct = dm.Tc // SSD_CHUNK
    is_ctx = ci < nct
    pos = jnp.where(is_ctx, ci % ncc, (ci - nct) % ncl)
    nseq = jnp.where(is_ctx, ncc, ncl)
    lat = jnp.clip((ci - nct) // ncl, 0, dm.DB - 1)
    return is_ctx, pos, nseq, lat


def _ssd_dir(dm, ci, rev, x_ref, bc_ref, dt_ref, bias_ref, a_ref, h0_ref, y_ref, s_ref, h_scr):
    Q, N, SH, P = SSD_CHUNK, dm.N, dm.SH, SSD_HEAD_DIM
    hg = SH // SSD_GROUPS
    gw = hg * P
    is_ctx, pos, nseq, _ = _chunk_pos(dm, ci)
    first = pos == (nseq - 1 if rev else 0)
    last = pos == (0 if rev else nseq - 1)

    @pl.when(jnp.logical_and(first, is_ctx))
    def _():
        h_scr[...] = jnp.zeros_like(h_scr)

    @pl.when(jnp.logical_and(first, jnp.logical_not(is_ctx)))
    def _():
        for k in range(dm.Bw // _LANES):
            h_scr[:, k * _LANES:(k + 1) * _LANES] = h0_ref[k * _LANES:(k + 1) * _LANES, :].T

    lane0 = (SH if rev else 0)
    dt = _softplus(dt_ref[...] + bias_ref[...])
    a = dt * a_ref[...]
    ri = lax.broadcasted_iota(jnp.int32, (Q, Q), 0)
    cj = lax.broadcasted_iota(jnp.int32, (Q, Q), 1)
    tri = (ri <= cj) if rev else (ri >= cj)
    cum = jnp.dot(tri.astype(_F32), a, preferred_element_type=_F32, precision=lax.Precision.HIGHEST)
    cum_t, dt_t = cum.T, dt.T
    ecum = jnp.exp(cum)
    end = 0 if rev else Q - 1
    wend = jnp.exp(cum[end:end + 1, :] - cum) * dt
    lane = lax.broadcasted_iota(jnp.int32, (Q, _LANES), 1)
    low = lane < P

    for g in range(SSD_GROUPS):
        b_g = bc_ref[:, g * N:(g + 1) * N]
        c_g = bc_ref[:, dm.GN + g * N:dm.GN + (g + 1) * N]
        cb = lax.dot_general(c_g, b_g, (((1,), (1,)), ((), ())), preferred_element_type=_F32)
        gs = slice(g * gw, (g + 1) * gw)
        y_diag, e_exp, w_exp = [], [], []
        for p in range(hg // 2):
            xs = slice(g * gw + p * _LANES, g * gw + (p + 1) * _LANES)
            xp = x_ref[:, xs].astype(_BF16)
            ys = []
            for s in range(2):
                ln = lane0 + g * hg + 2 * p + s
                seg = cum[:, ln:ln + 1] - cum_t[ln:ln + 1, :]
                m = cb * jnp.exp(jnp.where(tri, seg, _NEG)) * dt_t[ln:ln + 1, :]
                ys.append(jnp.dot(m.astype(_BF16), xp, preferred_element_type=_F32))
            ln = lane0 + g * hg + 2 * p
            y_diag.append(jnp.where(low, ys[0], ys[1]))
            e_exp.append(jnp.where(low, ecum[:, ln:ln + 1], ecum[:, ln + 1:ln + 2]))
            w_exp.append(jnp.where(low, wend[:, ln:ln + 1], wend[:, ln + 1:ln + 2]))
        y_diag = jnp.concatenate(y_diag, axis=1)
        e_exp = jnp.concatenate(e_exp, axis=1)
        w_exp = jnp.concatenate(w_exp, axis=1)
        h_in = h_scr[:, gs]
        y_off = jnp.dot(c_g, h_in.astype(_BF16), preferred_element_type=_F32) * e_exp
        y_ref[:, gs] = y_diag + y_off
        xw = (x_ref[:, gs] * w_exp).astype(_BF16)
        st = lax.dot_general(b_g, xw, (((0,), (0,)), ((), ())), preferred_element_type=_F32)
        h_scr[:, gs] = h_in * e_exp[end:end + 1, :] + st

    @pl.when(jnp.logical_and(last, is_ctx))
    def _():
        for k in range(dm.Bw // _LANES):
            s_ref[k * _LANES:(k + 1) * _LANES, :] = h_scr[:, k * _LANES:(k + 1) * _LANES].T


def _ssd_body(xf_ref, bcf_ref, dtf_ref, xb_ref, bcb_ref, dtb_ref, bias_ref, a_ref, h0f_ref, h0b_ref,
              yf_ref, yb_ref, sf_ref, sb_ref, hf_scr, hb_scr, *, dm):
    i = pl.program_id(0)
    n = pl.num_programs(0)
    _ssd_dir(dm, i, False, xf_ref, bcf_ref, dtf_ref, bias_ref, a_ref, h0f_ref, yf_ref, sf_ref, hf_scr)
    _ssd_dir(dm, n - 1 - i, True, xb_ref, bcb_ref, dtb_ref, bias_ref, a_ref, h0b_ref, yb_ref, sb_ref, hb_scr)


def _ssd(dm, slab, xs, bc, dt_bias, a_neg, state_ssd5, l):
    Q = SSD_CHUNK
    nchunks = dm.T // Q
    ncc = dm.Lc // Q
    HP = dm.SH * SSD_HEAD_DIM
    dcol = dm.s_dt // _LANES
    fwd = lambda i: i
    bwd = lambda i: nchunks - 1 - i

    def seq_c(ci):
        return jnp.minimum(ci // ncc, dm.B - 1)

    def lat_of(ci):
        return _chunk_pos(dm, ci)[3]

    def specs(order):
        return [pl.BlockSpec((Q, dm.Bw), lambda i: (order(i), 0)),
                pl.BlockSpec((Q, 2 * dm.GN), lambda i: (order(i), 0)),
                pl.BlockSpec((Q, _LANES), lambda i: (order(i), dcol))]

    row = pl.BlockSpec((1, _LANES), lambda i: (0, 0))
    h0 = lambda order, d: pl.BlockSpec((None, None, None, HP, dm.N), lambda i: (lat_of(order(i)), l, d, 0, 0))
    st = lambda order: pl.BlockSpec((None, HP, dm.N), lambda i: (seq_c(order(i)), 0, 0))
    ysd = jax.ShapeDtypeStruct((dm.T, dm.Bw), _F32)
    ssd = jax.ShapeDtypeStruct((dm.B, HP, dm.N), _F32)
    return _call(
        functools.partial(_ssd_body, dm=dm), grid=(nchunks,),
        in_specs=specs(fwd) + specs(bwd) + [row, row, h0(fwd, 0), h0(bwd, 1)],
        out_specs=[pl.BlockSpec((Q, dm.Bw), lambda i: (fwd(i), 0)), pl.BlockSpec((Q, dm.Bw), lambda i: (bwd(i), 0)),
                   st(fwd), st(bwd)],
        out_shape=[ysd, ysd, ssd, ssd],
        scratch=[pltpu.VMEM((dm.N, dm.Bw), _F32), pltpu.VMEM((dm.N, dm.Bw), _F32)], name="ssd")(
            xs, bc, slab, xs, bc, slab, dt_bias, a_neg, state_ssd5, state_ssd5)


def _lru_gates(dm, d, x_ref, wr_ref, wi_ref, br_ref, bi_ref, lam_ref, a_scr, u_scr):
    sp = _softplus(-lam_ref[d:d + 1, :])
    for h in range(dm.LH):
        hs = slice(h * _LANES, (h + 1) * _LANES)
        x = x_ref[:, hs]
        xb = x.astype(_BF16)
        r = jax.nn.sigmoid(jnp.dot(xb, wr_ref[d, h], preferred_element_type=_F32) + br_ref[d:d + 1, hs])
        ig = jax.nn.sigmoid(jnp.dot(xb, wi_ref[d, h], preferred_element_type=_F32) + bi_ref[d:d + 1, hs])
        log_a = -LRU_C * r * sp[:, hs]
        a = jnp.exp(log_a)
        a_scr[:, hs] = a
        u_scr[:, hs] = jnp.sqrt(-jnp.tanh(log_a) * (a * a + 1.0)) * (ig * x)


def _lru_body(xf_ref, xb_ref, wr_ref, wi_ref, br_ref, bi_ref, lam_ref, h0f_ref, h0b_ref,
              hf_ref, hb_ref, finf_ref, finb_ref, af, uf, ab, ub, carry, *, dm):
    R = _ROW_BLOCK
    i = pl.program_id(0)
    n = pl.num_programs(0)
    _lru_gates(dm, 0, xf_ref, wr_ref, wi_ref, br_ref, bi_ref, lam_ref, af, uf)
    _lru_gates(dm, 1, xb_ref, wr_ref, wi_ref, br_ref, bi_ref, lam_ref, ab, ub)

    posf, Lf = dm.seq_pos(i * R)
    posb, Lb = dm.seq_pos((n - 1 - i) * R)
    ctx_f = i * R < dm.Tc
    ctx_b = (n - 1 - i) * R < dm.Tc
    for d, start, is_ctx, h0_ref in ((0, posf == 0, ctx_f, h0f_ref), (1, posb + R == Lb, ctx_b, h0b_ref)):
        @pl.when(jnp.logical_and(start, is_ctx))
        def _():
            carry[d:d + 1, :] = jnp.zeros((1, dm.Bw), _F32)

        @pl.when(jnp.logical_and(start, jnp.logical_not(is_ctx)))
        def _():
            carry[d:d + 1, :] = h0_ref[d:d + 1, :]

    def tile(t, hs):
        hf, hb = hs
        base_f = pl.multiple_of(t * _SUBLANES, _SUBLANES)
        base_b = pl.multiple_of(R - _SUBLANES - t * _SUBLANES, _SUBLANES)
        for r in range(_SUBLANES):
            rf = base_f + r
            hf = af[pl.ds(rf, 1), :] * hf + uf[pl.ds(rf, 1), :]
            hf_ref[pl.ds(rf, 1), :] = hf
            rb = base_b + (_SUBLANES - 1 - r)
            hb = ab[pl.ds(rb, 1), :] * hb + ub[pl.ds(rb, 1), :]
            hb_ref[pl.ds(rb, 1), :] = hb
        return hf, hb

    hf, hb = lax.fori_loop(0, R // _SUBLANES, tile, (carry[0:1, :], carry[1:2, :]))
    carry[0:1, :] = hf
    carry[1:2, :] = hb

    @pl.when(jnp.logical_and(posf + R == Lf, ctx_f))
    def _():
        finf_ref[...] = hf

    @pl.when(jnp.logical_and(posb == 0, ctx_b))
    def _():
        finb_ref[...] = hb


def _lru(dm, xf, w_r, w_i, b_r, b_i, lam, state_lru, l):
    R = _ROW_BLOCK
    nblk = dm.T // R
    fwd = lambda i: i
    bwd = lambda i: nblk - 1 - i
    wspec = pl.BlockSpec((2, dm.LH, _LANES, _LANES), lambda i: (0, 0, 0, 0))
    vspec = pl.BlockSpec((2, dm.Bw), lambda i: (0, 0))
    lat_seq = lambda blk: jnp.clip((blk * R - dm.Tc) // dm.Ll, 0, dm.DB - 1)
    ctx_seq = lambda blk: jnp.minimum(blk * R // dm.Lc, dm.B - 1)
    h0 = lambda order: pl.BlockSpec((None, None, 2, dm.Bw), lambda i: (lat_seq(order(i)), l, 0, 0))
    fin = lambda order: pl.BlockSpec((None, 1, dm.Bw), lambda i: (ctx_seq(order(i)), 0, 0))
    hsd = jax.ShapeDtypeStruct((dm.T, dm.Bw), _F32)
    fsd = jax.ShapeDtypeStruct((dm.B, 1, dm.Bw), _F32)
    return _call(
        functools.partial(_lru_body, dm=dm), grid=(nblk,),
        in_specs=[pl.BlockSpec((R, dm.Bw), lambda i: (fwd(i), 0)), pl.BlockSpec((R, dm.Bw), lambda i: (bwd(i), 0)),
                  wspec, wspec, vspec, vspec, vspec, h0(fwd), h0(bwd)],
        out_specs=[pl.BlockSpec((R, dm.Bw), lambda i: (fwd(i), 0)), pl.BlockSpec((R, dm.Bw), lambda i: (bwd(i), 0)),
                   fin(fwd), fin(bwd)],
        out_shape=[hsd, hsd, fsd, fsd],
        scratch=[pltpu.VMEM((R, dm.Bw), _F32)] * 4 + [pltpu.VMEM((2, dm.Bw), _F32)], name="lru")(
            xf, xf, w_r, w_i, b_r, b_i, lam, state_lru, state_lru)


def _finish_body(yf_ref, yb_ref, xs_ref, z_ref, d_ref, ng_ref, hf_ref, hb_ref, gr_ref, ys_ref, yl_ref):
    y = (yf_ref[...] + yb_ref[...] + d_ref[...] * xs_ref[...]) * _silu(z_ref[...])
    gw = y.shape[1] // SSD_GROUPS
    for g in range(SSD_GROUPS):
        gs = slice(g * gw, (g + 1) * gw)
        yg = y[:, gs]
        yg = yg * lax.rsqrt(jnp.mean(yg * yg, axis=-1, keepdims=True) + EPS)
        ys_ref[:, gs] = (yg * ng_ref[:, gs]).astype(ys_ref.dtype)
    yl_ref[...] = ((hf_ref[...] + hb_ref[...]) * jax.nn.gelu(gr_ref[...], approximate=True)).astype(yl_ref.dtype)


def _finish(dm, slab, yf, yb, xs, d_exp, norm_g, hf, hb):
    R = _ROW_BLOCK
    blk = pl.BlockSpec((R, dm.Bw), lambda i: (i, 0))
    row = pl.BlockSpec((1, dm.Bw), lambda i: (0, 0))
    osd = jax.ShapeDtypeStruct((dm.T, dm.Bw), _BF16)
    return _call(
        _finish_body, grid=(dm.T // R,),
        in_specs=[blk, blk, blk, pl.BlockSpec((R, dm.Bw), lambda i: (i, dm.s_z // dm.Bw)), row, row,
                  blk, blk, pl.BlockSpec((R, dm.Bw), lambda i: (i, dm.s_gr // dm.Bw))],
        out_specs=[blk, blk], out_shape=[osd, osd], name="finish")(
            yf, yb, xs, slab, d_exp, norm_g.reshape(1, -1), hf, hb, slab)


def _layer_norm(r, g, b):
    mu = jnp.mean(r, axis=-1, keepdims=True)
    c = r - mu
    var = jnp.mean(c * c, axis=-1, keepdims=True)
    return c * lax.rsqrt(var + EPS) * g + b


def _split_bf16(x):
    hi = x.astype(_BF16)
    return hi, (x - hi.astype(_F32)).astype(_BF16)


def _outproj_body(a0c, a0l, a1, a2, a3, w_ref, x_ref, g1_ref, lg_ref, lb_ref, sc_ref, sh_ref, wr_ref,
                  x1_ref, h_ref, s_ref, acc, *, alpha, ctx_tiles):
    i = pl.program_id(0)
    k = pl.program_id(1)
    is_ctx = i < ctx_tiles
    branches = ((jnp.logical_and(k == 0, is_ctx), a0c, True), (jnp.logical_and(k == 0, jnp.logical_not(is_ctx)), a0l, True),
                (k == 1, a1, False), (k == 2, a2, False), (k == 3, a3, False))
    for cond, a_ref, init in branches:
        @pl.when(cond)
        def _():
            d = jnp.dot(a_ref[...], w_ref[...], preferred_element_type=_F32)
            if init:
                acc[...] = d
            else:
                acc[...] += d

    @pl.when(k == 3)
    def _():
        x1 = _layer_norm(alpha * x_ref[...] + g1_ref[...] * acc[...], lg_ref[...], lb_ref[...])
        x1_ref[...] = x1
        h = x1 * (1.0 + sc_ref[...]) + sh_ref[...]
        h_ref[...] = h
        hh, hl = _split_bf16(h)
        wh, wl = _split_bf16(wr_ref[...])
        logits = (jnp.dot(hh, wh, preferred_element_type=_F32) + jnp.dot(hh, wl, preferred_element_type=_F32)
                  + jnp.dot(hl, wh, preferred_element_type=_F32))
        s_ref[...] = jax.nn.sigmoid(logits)


def _outproj(dm, branches, w_out, x, mod5, ln_g, ln_b, w_router_pad, l, alpha):
    tm = 256
    D, Bw = dm.D, dm.Bw
    a_spec = pl.BlockSpec((tm, Bw), lambda i, k: (i, 0))
    ctx_tiles = dm.Tc // tm
    att_c = pl.BlockSpec((tm, Bw), lambda i, k: (jnp.minimum(i, ctx_tiles - 1), 0))
    att_l = pl.BlockSpec((tm, Bw), lambda i, k: (jnp.maximum(i - ctx_tiles, 0), 0))
    tile = pl.BlockSpec((tm, D), lambda i, k: (i, 0))
    row = pl.BlockSpec((1, D), lambda i, k: (0, 0))
    return _call(
        functools.partial(_outproj_body, alpha=alpha, ctx_tiles=ctx_tiles), grid=(dm.T // tm, 4),
        in_specs=[att_c, att_l] + [a_spec] * 3 + [pl.BlockSpec((Bw, D), lambda i, k: (k, 0)), tile, _mod_spec(dm, l, 2, tm), row, row,
                                 _mod_spec(dm, l, 4, tm), _mod_spec(dm, l, 3, tm),
                                 pl.BlockSpec((D, _LANES), lambda i, k: (0, 0))],
        out_specs=[tile, tile, pl.BlockSpec((tm, _LANES), lambda i, k: (i, 0))],
        out_shape=[jax.ShapeDtypeStruct((dm.T, D), _F32), jax.ShapeDtypeStruct((dm.T, D), _F32),
                   jax.ShapeDtypeStruct((dm.T, _LANES), _F32)],
        scratch=[pltpu.VMEM((tm, D), _F32)], name="outproj")(
            *branches, w_out, x, mod5, ln_g.reshape(1, D), ln_b.reshape(1, D), mod5, mod5, w_router_pad)


def _route(s, router_bias, tm):
    T, E = s.shape
    per = E // N_EXPERT_GROUPS
    sel = s + router_bias.astype(_F32)
    grp_score = lax.top_k(sel.reshape(T, N_EXPERT_GROUPS, per), TOP_K)[0].sum(-1)
    grp = jnp.argmax(grp_score, axis=-1)
    in_grp = (jnp.arange(E) // per)[None, :] == grp[:, None]
    _, idx = lax.top_k(jnp.where(in_grp, sel, -jnp.inf), TOP_K)
    wts = jnp.take_along_axis(s, idx, axis=-1)
    wts = wts / jnp.sum(wts, -1, keepdims=True)
    A = T * TOP_K
    onehot = (idx.reshape(A, 1) == jnp.arange(E, dtype=idx.dtype)[None, :]).astype(jnp.int32)
    csum = jnp.cumsum(onehot, axis=0)
    counts = csum[-1]
    padded = (counts + tm - 1) // tm * tm
    pend = jnp.cumsum(padded)
    pstart = pend - padded
    pos = jnp.sum(onehot * (csum - 1 + pstart[None, :]), axis=1).astype(jnp.int32)
    n_blocks = -(-A // tm) + E
    src_tok = jnp.zeros((n_blocks * tm,), jnp.int32).at[pos].set(jnp.arange(A, dtype=jnp.int32) // TOP_K)
    blk0 = jnp.arange(n_blocks, dtype=jnp.int32) * tm
    block_e = jnp.minimum(jnp.sum((blk0[:, None] >= pend[None, :]).astype(jnp.int32), axis=1), E - 1)
    n_used = (pend[-1] // tm).astype(jnp.int32).reshape(1)
    return src_tok, block_e.astype(jnp.int32), n_used, pos, wts


def _row_copy(src_hbm, row, dst, dst_row, sem):
    return pltpu.make_async_copy(src_hbm.at[pl.ds(row, 1)], dst.at[pl.ds(dst_row, 1)], sem)


def _gather_body(src_ref, h_hbm, o_ref, buf, sem):
    tm = o_ref.shape[0]
    i = pl.program_id(0)
    n = pl.num_programs(0)

    def issue(step, slot):
        def body(r, c):
            _row_copy(h_hbm, src_ref[step * tm + r], buf.at[slot], r, sem.at[slot]).start()
            return c
        lax.fori_loop(0, tm, body, 0)

    @pl.when(i == 0)
    def _():
        issue(0, 0)

    @pl.when(i + 1 < n)
    def _():
        issue(i + 1, (i + 1) % 2)

    slot = i % 2

    def wait(r, c):
        _row_copy(h_hbm, 0, buf.at[slot], r, sem.at[slot]).wait()
        return c
    lax.fori_loop(0, tm, wait, 0)
    o_ref[...] = buf[slot].astype(o_ref.dtype)


def _gather(h, src_tok, tm):
    D = h.shape[1]
    n_blocks = src_tok.shape[0] // tm
    return _call(
        _gather_body, grid=(n_blocks,), nprefetch=1,
        in_specs=[pl.BlockSpec(memory_space=pl.ANY)],
        out_specs=pl.BlockSpec((tm, D), lambda i, src: (i, 0)),
        out_shape=jax.ShapeDtypeStruct((n_blocks * tm, D), _BF16),
        scratch=[pltpu.VMEM((2, tm, D), _F32), pltpu.SemaphoreType.DMA((2,))], name="moe_gather")(src_tok, h)


def _expert_changed(be_ref, m):
    return jnp.logical_or(m == 0, be_ref[m] != be_ref[jnp.maximum(m - 1, 0)])


def _gateup_body(be_ref, nu_ref, x_ref, wg_ref, wu_ref, g_ref, wg_s, wu_s):
    m = pl.program_id(1)

    @pl.when(_expert_changed(be_ref, m))
    def _():
        wg_s[...] = wg_ref[...].astype(_BF16)
        wu_s[...] = wu_ref[...].astype(_BF16)

    @pl.when(m < nu_ref[0])
    def _():
        x = x_ref[...]
        gate = jnp.dot(x, wg_s[...], preferred_element_type=_F32)
        up = jnp.dot(x, wu_s[...], preferred_element_type=_F32)
        g_ref[...] = (_silu(gate) * up).astype(g_ref.dtype)

    @pl.when(m >= nu_ref[0])
    def _():
        g_ref[...] = jnp.zeros_like(g_ref)


def _gateup(xs, block_e, n_used, w_gate, w_up, tm):
    A_pad, D = xs.shape
    F = w_gate.shape[-1]
    tf = _pick_tile(F, (512, 256, 128))
    wspec = pl.BlockSpec((None, D, tf), lambda j, m, be, nu: (be[m], 0, j))
    return _call(
        _gateup_body, grid=(F // tf, A_pad // tm), nprefetch=2,
        in_specs=[pl.BlockSpec((tm, D), lambda j, m, be, nu: (m, 0)), wspec, wspec],
        out_specs=pl.BlockSpec((tm, tf), lambda j, m, be, nu: (m, j)),
        out_shape=jax.ShapeDtypeStruct((A_pad, F), _BF16),
        scratch=[pltpu.VMEM((D, tf), _BF16)] * 2, name="moe_gateup")(block_e, n_used, xs, w_gate, w_up)


def _down_body(be_ref, nu_ref, g_ref, wd_ref, y_ref, wd_s):
    m = pl.program_id(1)

    @pl.when(_expert_changed(be_ref, m))
    def _():
        wd_s[...] = wd_ref[...].astype(_BF16)

    @pl.when(m < nu_ref[0])
    def _():
        y_ref[...] = jnp.dot(g_ref[...], wd_s[...], preferred_element_type=_F32)

    @pl.when(m >= nu_ref[0])
    def _():
        y_ref[...] = jnp.zeros_like(y_ref)


def _down(g, block_e, n_used, w_down, tm):
    A_pad, F = g.shape
    D = w_down.shape[-1]
    tn = _pick_tile(D, (2048, 1024, 512, 256, 128))
    return _call(
        _down_body, grid=(D // tn, A_pad // tm), nprefetch=2,
        in_specs=[pl.BlockSpec((tm, F), lambda j, m, be, nu: (m, 0)),
                  pl.BlockSpec((None, F, tn), lambda j, m, be, nu: (be[m], 0, j))],
        out_specs=pl.BlockSpec((tm, tn), lambda j, m, be, nu: (m, j)),
        out_shape=jax.ShapeDtypeStruct((A_pad, D), _F32),
        scratch=[pltpu.VMEM((F, tn), _BF16)], name="moe_down")(block_e, n_used, g, w_down)


def _combine_body(pos_ref, y_hbm, x1_ref, w_ref, g2_ref, lg_ref, lb_ref, *rest, alpha, ctx_tiles):
    if ctx_tiles is None:
        sc_ref, sh_ref, x2_ref, hn_ref, buf, sem = rest
    else:
        xp_ref, xs_ref, buf, sem = rest
    tm = x1_ref.shape[0]
    i = pl.program_id(0)
    n = pl.num_programs(0)

    def issue(step, slot):
        def body(r, c):
            for k in range(TOP_K):
                _row_copy(y_hbm, pos_ref[(step * tm + r) * TOP_K + k], buf.at[slot, k], r, sem.at[slot]).start()
            return c
        lax.fori_loop(0, tm, body, 0)

    @pl.when(i == 0)
    def _():
        issue(0, 0)

    @pl.when(i + 1 < n)
    def _():
        issue(i + 1, (i + 1) % 2)

    slot = i % 2

    def wait(r, c):
        for k in range(TOP_K):
            _row_copy(y_hbm, 0, buf.at[slot, k], r, sem.at[slot]).wait()
        return c
    lax.fori_loop(0, tm, wait, 0)
    w = w_ref[...]
    ffn = w[:, 0:1] * buf[slot, 0]
    for k in range(1, TOP_K):
        ffn = ffn + w[:, k:k + 1] * buf[slot, k]
    x2 = _layer_norm(alpha * x1_ref[...] + g2_ref[...] * ffn, lg_ref[...], lb_ref[...])
    if ctx_tiles is None:
        x2_ref[...] = x2
        hn_ref[...] = (x2 * (1.0 + sc_ref[...]) + sh_ref[...]).astype(hn_ref.dtype)
    else:
        @pl.when(i < ctx_tiles)
        def _():
            xp_ref[...] = x2

        @pl.when(i >= ctx_tiles)
        def _():
            xs_ref[...] = x2


def _combine(dm, y, pos, wts, x1, mod5, ln_g, ln_b, l, alpha):
    tm = 256
    D = dm.D
    final = l == dm.depth - 1
    ctx_tiles = dm.Tc // tm
    tile = pl.BlockSpec((tm, D), lambda i, p: (i, 0))
    row = pl.BlockSpec((1, D), lambda i, p: (0, 0))
    in_specs = [pl.BlockSpec(memory_space=pl.ANY), tile, pl.BlockSpec((tm, TOP_K), lambda i, p: (i, 0)),
                _mod_spec(dm, l, 5, tm), row, row]
    args = [pos, y, x1, wts, mod5, ln_g.reshape(1, D), ln_b.reshape(1, D)]
    if final:
        out_specs = [pl.BlockSpec((tm, D), lambda i, p: (jnp.minimum(i, ctx_tiles - 1), 0)),
                     pl.BlockSpec((tm, D), lambda i, p: (jnp.maximum(i - ctx_tiles, 0), 0))]
        out_shape = [jax.ShapeDtypeStruct((dm.Tc, D), _F32), jax.ShapeDtypeStruct((dm.Tl, D), _F32)]
    else:
        in_specs += [_mod_spec(dm, l + 1, 1, tm), _mod_spec(dm, l + 1, 0, tm)]
        args += [mod5, mod5]
        out_specs = [tile, tile]
        out_shape = [jax.ShapeDtypeStruct((dm.T, D), _F32), jax.ShapeDtypeStruct((dm.T, D), _BF16)]
    return _call(
        functools.partial(_combine_body, alpha=alpha, ctx_tiles=ctx_tiles if final else None),
        grid=(dm.T // tm,), nprefetch=1, in_specs=in_specs, out_specs=out_specs, out_shape=out_shape,
        scratch=[pltpu.VMEM((2, TOP_K, tm, D), _F32), pltpu.SemaphoreType.DMA((2,))], name="moe_combine")(*args)


def kernel(x_prompt, x_sample, cache_k, cache_v, state_ssd, state_lru, c, c_ctx, w_ada, b_ada, w_in, w_out, attn_sink, ssd_conv_w, ssd_conv_b, ssd_a_log, ssd_dt_bias, ssd_d, ssd_norm_g, lru_conv_w, lru_conv_b, lru_lambda, lru_w_r, lru_b_r, lru_w_i, lru_b_i, pool_w, pool_b, pool_scale, ln1_g, ln1_b, ln2_g, ln2_b, w_router, router_bias, moe_w_gate, moe_w_up, moe_w_down):
    dm = _Dims(x_prompt, x_sample, cache_k, state_ssd, lru_w_r, w_router, moe_w_gate, w_ada)
    depth, D, Bw, KVw, GN, SH = dm.depth, dm.D, dm.Bw, dm.KVw, dm.GN, dm.SH
    alpha = (2 * depth) ** 0.25

    rows = -(-(1 + dm.DB) // _SUBLANES) * _SUBLANES
    cond = jnp.zeros((rows, D), _F32).at[0].set(c_ctx).at[1:1 + dm.DB].set(c)
    mod5 = _ada(cond, w_ada, b_ada).reshape(depth, rows, 6, 1, D)

    x = jnp.concatenate([x_prompt.reshape(dm.Tc, D), x_sample.reshape(dm.Tl, D)], axis=0)
    h = _modulate(dm, x, mod5, 0)

    cos, sin = _rope_tables(dm.Ll)
    cache_k4 = cache_k.reshape(dm.DB, depth, dm.past, KVw)
    cache_v4 = cache_v.reshape(dm.DB, depth, dm.past, KVw)
    state_ssd5 = state_ssd.reshape(dm.DB, depth, 2, SH * SSD_HEAD_DIM, dm.N)
    w_router_pad = jnp.zeros((D, _LANES), _F32).at[:, :dm.E].set(w_router)

    o_k, o_v, o_z = Bw, Bw + KVw, Bw + 2 * KVw
    o_xs = o_z + Bw
    o_B = o_xs + Bw
    o_dt = o_B + 2 * GN
    o_xr = o_dt + 2 * SH
    o_gr, o_xp = o_xr + Bw, o_xr + 2 * Bw

    ks, vs, ssds, lrus = [], [], [], []
    for l in range(depth):
        wi = w_in[l]
        cols = lambda o, w: wi[:, o:o + w]
        w_slab = jnp.concatenate(
            [cols(0, Bw), cols(o_z, Bw), cols(o_xs, Bw), cols(o_xr, Bw), cols(o_gr, Bw), cols(o_xp, Bw),
             cols(o_k, KVw), cols(o_v, KVw), cols(o_B, 2 * GN), cols(o_dt, 2 * SH),
             jnp.zeros((D, dm.slab_w - dm.s_dt - 2 * SH), wi.dtype)], axis=1).astype(_BF16)
        slab = _matmul(h, w_slab)

        att_c = _attn_ctx(dm, slab, attn_sink[l])
        qr, kr = _rope(dm, slab, cos, sin)
        att_l = _attn_lat(dm, slab, qr, kr, cache_k4, cache_v4, attn_sink[l], l)

        cw = ssd_conv_w[l]
        xs = _conv(dm, slab, dm.s_xs, Bw, cw[:, :Bw], ssd_conv_b[l][:Bw], True, _F32)
        bc = _conv(dm, slab, dm.s_B, 2 * GN, cw[:, Bw:], ssd_conv_b[l][Bw:], True, _BF16)
        pad = jnp.zeros((_LANES - 2 * SH,), _F32)
        dt_bias = jnp.concatenate([ssd_dt_bias[l].reshape(-1), pad]).reshape(1, _LANES)
        a_neg = jnp.concatenate([-jnp.exp(ssd_a_log[l].reshape(-1)), pad]).reshape(1, _LANES)
        yf, yb, s_f, s_b = _ssd(dm, slab, xs, bc, dt_bias, a_neg, state_ssd5, l)

        xf = _conv(dm, slab, dm.s_xr, Bw, lru_conv_w[l], lru_conv_b[l], False, _F32)
        hf, hb, fin_f, fin_b = _lru(dm, xf, lru_w_r[l].astype(_BF16), lru_w_i[l].astype(_BF16),
                                    lru_b_r[l], lru_b_i[l], lru_lambda[l], state_lru, l)

        d_exp = jnp.repeat(ssd_d[l], SSD_HEAD_DIM).reshape(1, Bw)
        y_ssd, y_lru = _finish(dm, slab, yf, yb, xs, d_exp, ssd_norm_g[l], hf, hb)
        y_pool = _pool(dm, slab, pool_w[l], pool_b[l], pool_scale[l])

        x1, h2, s_pad = _outproj(dm, (att_c, att_l, y_ssd, y_lru, y_pool), w_out[l].astype(_BF16), x, mod5,
                                 ln1_g[l], ln1_b[l], w_router_pad, l, alpha)

        src_tok, block_e, n_used, pos, wts = _route(s_pad[:, :dm.E], router_bias, _MOE_TM)
        xs_sorted = _gather(h2, src_tok, _MOE_TM)
        g = _gateup(xs_sorted, block_e, n_used, moe_w_gate[l], moe_w_up[l], _MOE_TM)
        y = _down(g, block_e, n_used, moe_w_down[l], _MOE_TM)
        x, h = _combine(dm, y, pos, wts, x1, mod5, ln2_g[l], ln2_b[l], l, alpha)

        ks.append(slab[:dm.Tc, dm.s_k:dm.s_k + KVw].reshape(dm.B, dm.Lc, dm.KV, ATT_HEAD_DIM))
        vs.append(slab[:dm.Tc, dm.s_v:dm.s_v + KVw].reshape(dm.B, dm.Lc, dm.KV, ATT_HEAD_DIM))
        ssds.append(jnp.stack([s_f, s_b], axis=1).reshape(dm.B, 2, SH, SSD_HEAD_DIM, dm.N))
        lrus.append(jnp.concatenate([fin_f, fin_b], axis=1))

    y_prompt = x.reshape(dm.B, dm.Lc, D)
    y_sample = h.reshape(dm.DB, dm.Ll, D)
    return (y_prompt, y_sample, jnp.stack(ks, axis=1), jnp.stack(vs, axis=1),
            jnp.stack(ssds, axis=1), jnp.stack(lrus, axis=1))
```

```python
import functools
import math

import jax
import jax.numpy as jnp
import numpy as np
from jax import lax
from jax.experimental import pallas as pl
from jax.experimental.pallas import tpu as pltpu

_F32 = jnp.float32
_BF16 = jnp.bfloat16

GRID_W = 64
ATT_HEAD_DIM = 128
ATT_BLOCK = 128
ROPE_BASE = 10000.0
SSD_HEAD_DIM = 64
SSD_GROUPS = 2
SSD_CHUNK = 128
LRU_C = 8.0
POOL_WINDOWS = (2, 4, 8, 16)
N_EXPERT_GROUPS = 4
TOP_K = 2
EPS = 1e-5

_LANES = 128
_SUBLANES = 8
_VMEM_BYTES_V7X = 64 * 1024 * 1024
_VMEM_LIMIT = _VMEM_BYTES_V7X * 7 // 8

_ROW_BLOCK = 256
_MOE_TM = 512
_NEG = -1e30
_MIX_SSD_LRU, _MIX_ATT, _MIX_POOL = 0, 2, 3


def _call(body, *, grid, in_specs, out_specs, out_shape, scratch=(), nprefetch=0, aliases=None, name=None):
    grid_spec = pltpu.PrefetchScalarGridSpec(
        num_scalar_prefetch=nprefetch, grid=grid, in_specs=in_specs, out_specs=out_specs,
        scratch_shapes=list(scratch))
    return pl.pallas_call(
        body, grid_spec=grid_spec, out_shape=out_shape,
        compiler_params=pltpu.CompilerParams(
            dimension_semantics=("arbitrary",) * len(grid), vmem_limit_bytes=_VMEM_LIMIT),
        input_output_aliases=aliases or {}, name=name)


def _softplus(x):
    return jnp.maximum(x, 0.0) + jnp.log1p(jnp.exp(-jnp.abs(x)))


def _silu(x):
    return x * jax.nn.sigmoid(x)


class _Dims:
    def __init__(self, x_prompt, x_sample, cache_k, state_ssd, lru_w_r, w_router, moe_w_gate, w_ada):
        self.B, self.Lc, self.D = x_prompt.shape
        self.DB, self.Ll, _ = x_sample.shape
        self.depth = w_ada.shape[0]
        self.Tc, self.Tl = self.B * self.Lc, self.DB * self.Ll
        self.T = self.Tc + self.Tl
        self.Bw = self.D // 4
        self.past, self.KV = cache_k.shape[2], cache_k.shape[3]
        self.H = self.Bw // ATT_HEAD_DIM
        self.G = self.H // self.KV
        self.KVw = self.KV * ATT_HEAD_DIM
        self.SH = self.Bw // SSD_HEAD_DIM
        self.N = state_ssd.shape[-1]
        self.GN = SSD_GROUPS * self.N
        self.LH = lru_w_r.shape[2]
        self.E = w_router.shape[1]
        self.F = moe_w_gate.shape[-1]
        Bw, KVw, GN = self.Bw, self.KVw, self.GN
        self.s_q, self.s_z, self.s_xs, self.s_xr, self.s_gr, self.s_xp = (i * Bw for i in range(6))
        self.s_k = 6 * Bw
        self.s_v = self.s_k + KVw
        self.s_B = self.s_v + KVw
        self.s_C = self.s_B + GN
        self.s_dt = self.s_C + GN
        used = self.s_dt + _LANES
        self.slab_w = -(-used // 512) * 512
        assert 2 * self.SH <= _LANES and self.N == _LANES and ATT_HEAD_DIM == _LANES
        assert self.s_k % KVw == 0 and self.s_B % (2 * GN) == 0 and self.Bw % (2 * _LANES) == 0
        assert self.Lc % _ROW_BLOCK == 0 and self.Ll % _ROW_BLOCK == 0
        assert self.Lc % SSD_CHUNK == 0 and self.Ll % SSD_CHUNK == 0 and self.Ll % ATT_BLOCK == 0
        assert self.Bw // self.LH == _LANES and self.Bw % len(POOL_WINDOWS) == 0

    def seg_of_row(self, r0):
        return jnp.where(r0 < self.Tc, 0, 1 + (r0 - self.Tc) // self.Ll)

    def seq_pos(self, r0):
        is_ctx = r0 < self.Tc
        pos = jnp.where(is_ctx, r0 % self.Lc, (r0 - self.Tc) % self.Ll)
        return pos, jnp.where(is_ctx, self.Lc, self.Ll)


def _ada_body(c_ref, w_ref, b_ref, o_ref):
    s = _silu(c_ref[...]).astype(_BF16)
    o_ref[...] = jnp.dot(s, w_ref[...].astype(_BF16), preferred_element_type=_F32) + b_ref[...]


def _ada(cond, w_ada, b_ada):
    depth, D, n6 = w_ada.shape
    rows = cond.shape[0]
    tn = 512
    return _call(
        _ada_body, grid=(depth, n6 // tn),
        in_specs=[pl.BlockSpec((rows, D), lambda l, j: (0, 0)),
                  pl.BlockSpec((None, D, tn), lambda l, j: (l, 0, j)),
                  pl.BlockSpec((None, 1, tn), lambda l, j: (l, 0, j))],
        out_specs=pl.BlockSpec((None, rows, tn), lambda l, j: (l, 0, j)),
        out_shape=jax.ShapeDtypeStruct((depth, rows, n6), _F32), name="ada")(
            cond, w_ada, b_ada.reshape(depth, 1, n6))


def _mod_spec(dm, l, k, tm):
    return pl.BlockSpec((None, None, None, 1, dm.D), lambda i, *_: (l, dm.seg_of_row(i * tm), k, 0, 0))


def _modulate_body(x_ref, sh_ref, sc_ref, o_ref):
    o_ref[...] = (x_ref[...] * (1.0 + sc_ref[...]) + sh_ref[...]).astype(o_ref.dtype)


def _modulate(dm, x, mod5, l):
    tm = 512
    return _call(
        _modulate_body, grid=(dm.T // tm,),
        in_specs=[pl.BlockSpec((tm, dm.D), lambda i: (i, 0)), _mod_spec(dm, l, 0, tm), _mod_spec(dm, l, 1, tm)],
        out_specs=pl.BlockSpec((tm, dm.D), lambda i: (i, 0)),
        out_shape=jax.ShapeDtypeStruct((dm.T, dm.D), _BF16), name="modulate")(x, mod5, mod5)


def _mm_body(x_ref, w_ref, o_ref):
    o_ref[...] = jnp.dot(x_ref[...], w_ref[...], preferred_element_type=_F32).astype(o_ref.dtype)


def _pick_tile(n, candidates):
    for c in candidates:
        if n % c == 0:
            return c
    raise ValueError(f"no tile for {n}")


def _matmul(x, w, name, out_dtype=_F32):
    M, K = x.shape
    N = w.shape[1]
    tm = _pick_tile(M, (512, 256, 128))
    tn = _pick_tile(N, (1536, 1280, 1024, 768, 512))
    return _call(
        _mm_body, grid=(N // tn, M // tm),
        in_specs=[pl.BlockSpec((tm, K), lambda j, i: (i, 0)), pl.BlockSpec((K, tn), lambda j, i: (0, j))],
        out_specs=pl.BlockSpec((tm, tn), lambda j, i: (i, j)),
        out_shape=jax.ShapeDtypeStruct((M, N), out_dtype), name=name)(x, w)


def _rope_body(q_ref, k_ref, cos_ref, sin_ref, qo_ref, ko_ref):
    cos, sin = cos_ref[...], sin_ref[...]
    quarter = ATT_HEAD_DIM // 4
    lane = lax.broadcasted_iota(jnp.int32, cos.shape, 1)
    first = (lane % (2 * quarter)) < quarter

    def rot(x):
        partner = jnp.where(first, pltpu.roll(x, ATT_HEAD_DIM - quarter, 1), pltpu.roll(x, quarter, 1))
        return x * cos + partner * sin

    for ref, out in ((q_ref, qo_ref), (k_ref, ko_ref)):
        for h in range(ref.shape[1] // ATT_HEAD_DIM):
            sl = slice(h * ATT_HEAD_DIM, (h + 1) * ATT_HEAD_DIM)
            out[:, sl] = rot(ref[:, sl]).astype(out.dtype)


def _rope_tables(L):
    quarter = ATT_HEAD_DIM // 4
    t = jnp.arange(L)
    inv = ROPE_BASE ** (-jnp.arange(quarter, dtype=_F32) / quarter)
    ang_r = (t // GRID_W).astype(_F32)[:, None] * inv[None, :]
    ang_c = (t % GRID_W).astype(_F32)[:, None] * inv[None, :]
    cos = jnp.concatenate([jnp.cos(ang_r)] * 2 + [jnp.cos(ang_c)] * 2, -1)
    sin = jnp.concatenate([-jnp.sin(ang_r), jnp.sin(ang_r), -jnp.sin(ang_c), jnp.sin(ang_c)], -1)
    return cos, sin


def _rope(dm, slab, cos, sin):
    tr = _ROW_BLOCK
    r0, per_seq = dm.Tc // tr, dm.Ll // tr
    return _call(
        _rope_body, grid=(dm.Tl // tr,),
        in_specs=[pl.BlockSpec((tr, dm.Bw), lambda i: (r0 + i, dm.s_q // dm.Bw)),
                  pl.BlockSpec((tr, dm.KVw), lambda i: (r0 + i, dm.s_k // dm.KVw)),
                  pl.BlockSpec((tr, _LANES), lambda i: (i % per_seq, 0)),
                  pl.BlockSpec((tr, _LANES), lambda i: (i % per_seq, 0))],
        out_specs=[pl.BlockSpec((tr, dm.Bw), lambda i: (i, 0)), pl.BlockSpec((tr, dm.KVw), lambda i: (i, 0))],
        out_shape=[jax.ShapeDtypeStruct((dm.Tl, dm.Bw), _BF16), jax.ShapeDtypeStruct((dm.Tl, dm.KVw), _BF16)],
        name="rope")(slab, slab, cos, sin)


def _attend(sink_ref, q, k_all, v_all, bias, o_ref, kv, G):
    Q = q.shape[0]
    q_st = jnp.concatenate([q[:, g * ATT_HEAD_DIM:(g + 1) * ATT_HEAD_DIM] for g in range(G)], axis=0)
    s = lax.dot_general(q_st, k_all, (((1,), (1,)), ((), ())), preferred_element_type=_F32)
    s = s * (ATT_HEAD_DIM ** -0.5)
    if bias is not None:
        s = s + bias
    sk = jnp.concatenate([jnp.full((Q, 1), sink_ref[kv * G + g], _F32) for g in range(G)], axis=0)
    m = jnp.maximum(jnp.max(s, axis=-1, keepdims=True), sk)
    p = jnp.exp(s - m)
    denom = jnp.sum(p, axis=-1, keepdims=True) + jnp.exp(sk - m)
    o = jnp.dot(p.astype(_BF16), v_all, preferred_element_type=_F32) / denom
    for g in range(G):
        h = kv * G + g
        o_ref[:, h * ATT_HEAD_DIM:(h + 1) * ATT_HEAD_DIM] = o[g * Q:(g + 1) * Q].astype(o_ref.dtype)


def _attn_ctx_body(sink_ref, q_ref, k_ref, v_ref, _mix_in, o_ref, *, KV, G):
    for kv in range(KV):
        hs = slice(kv * ATT_HEAD_DIM, (kv + 1) * ATT_HEAD_DIM)
        q = q_ref[:, kv * G * ATT_HEAD_DIM:(kv + 1) * G * ATT_HEAD_DIM].astype(_BF16)
        _attend(sink_ref, q, k_ref[:, hs].astype(_BF16), v_ref[:, hs].astype(_BF16), None, o_ref, kv, G)


def _attn_ctx(dm, slab, sink, mix):
    Lc = dm.Lc
    return _call(
        functools.partial(_attn_ctx_body, KV=dm.KV, G=dm.G), grid=(dm.B,),
        in_specs=[pl.BlockSpec(memory_space=pltpu.SMEM),
                  pl.BlockSpec((Lc, dm.Bw), lambda b: (b, dm.s_q // dm.Bw)),
                  pl.BlockSpec((Lc, dm.KVw), lambda b: (b, dm.s_k // dm.KVw)),
                  pl.BlockSpec((Lc, dm.KVw), lambda b: (b, dm.s_v // dm.KVw)),
                  pl.BlockSpec(memory_space=pl.ANY)],
        out_specs=pl.BlockSpec((Lc, dm.Bw), lambda b: (b, _MIX_ATT)),
        out_shape=jax.ShapeDtypeStruct(mix.shape, mix.dtype), aliases={4: 0}, name="attn_ctx")(
            sink, slab, slab, slab, mix)


def _attn_lat_body(sink_ref, q_ref, kp_ref, kc_ref, kn_ref, vp_ref, vc_ref, vn_ref, ck_ref, cv_ref, _mix_in,
                   o_ref, *, KV, G, nb):
    n = pl.program_id(1)
    Q = ATT_BLOCK
    S = 3 * Q + ck_ref.shape[0]
    rows = G * Q
    c_i = lax.broadcasted_iota(jnp.int32, (rows, S), 1)
    qi = lax.broadcasted_iota(jnp.int32, (rows, S), 0) % Q
    bad_prev = c_i < jnp.maximum(qi, jnp.where(n == 0, Q, 0))
    bad_next = jnp.logical_and(c_i - 2 * Q > qi - jnp.where(n == nb - 1, Q, 0), c_i < 3 * Q)
    bias = jnp.where(jnp.logical_or(bad_prev, bad_next), _NEG, 0.0)
    for kv in range(KV):
        hs = slice(kv * ATT_HEAD_DIM, (kv + 1) * ATT_HEAD_DIM)
        k_all = jnp.concatenate([kp_ref[:, hs], kc_ref[:, hs], kn_ref[:, hs], ck_ref[:, hs].astype(_BF16)], axis=0)
        v_all = jnp.concatenate([vp_ref[:, hs].astype(_BF16), vc_ref[:, hs].astype(_BF16),
                                 vn_ref[:, hs].astype(_BF16), cv_ref[:, hs].astype(_BF16)], axis=0)
        q = q_ref[:, kv * G * ATT_HEAD_DIM:(kv + 1) * G * ATT_HEAD_DIM]
        _attend(sink_ref, q, k_all, v_all, bias, o_ref, kv, G)


def _attn_lat(dm, slab, qr, kr, cache_k4, cache_v4, sink, l, mix):
    Q = ATT_BLOCK
    nb = dm.Ll // Q
    r0 = dm.Tc // Q
    vcol = dm.s_v // dm.KVw
    prev = lambda n: jnp.maximum(n - 1, 0)
    nxt = lambda n: jnp.minimum(n + 1, nb - 1)
    cache_spec = pl.BlockSpec((None, None, dm.past, dm.KVw), lambda b, n: (b, l, 0, 0))
    return _call(
        functools.partial(_attn_lat_body, KV=dm.KV, G=dm.G, nb=nb), grid=(dm.DB, nb),
        in_specs=[pl.BlockSpec(memory_space=pltpu.SMEM),
                  pl.BlockSpec((Q, dm.Bw), lambda b, n: (b * nb + n, 0)),
                  pl.BlockSpec((Q, dm.KVw), lambda b, n: (b * nb + prev(n), 0)),
                  pl.BlockSpec((Q, dm.KVw), lambda b, n: (b * nb + n, 0)),
                  pl.BlockSpec((Q, dm.KVw), lambda b, n: (b * nb + nxt(n), 0)),
                  pl.BlockSpec((Q, dm.KVw), lambda b, n: (r0 + b * nb + prev(n), vcol)),
                  pl.BlockSpec((Q, dm.KVw), lambda b, n: (r0 + b * nb + n, vcol)),
                  pl.BlockSpec((Q, dm.KVw), lambda b, n: (r0 + b * nb + nxt(n), vcol)),
                  cache_spec, cache_spec, pl.BlockSpec(memory_space=pl.ANY)],
        out_specs=pl.BlockSpec((Q, dm.Bw), lambda b, n: (r0 + b * nb + n, _MIX_ATT)),
        out_shape=jax.ShapeDtypeStruct(mix.shape, mix.dtype), aliases={10: 0}, name="attn_lat")(
            sink, qr, kr, kr, kr, slab, slab, slab, cache_k4, cache_v4, mix)


def _halo_specs(dm, cw, colblk):
    R = _ROW_BLOCK
    per = R // _SUBLANES
    last = dm.T // _SUBLANES - 1
    return [pl.BlockSpec((R, cw), lambda i, *_: (i, colblk)),
            pl.BlockSpec((_SUBLANES, cw), lambda i, *_: (jnp.maximum(i * per - 1, 0), colblk)),
            pl.BlockSpec((_SUBLANES, cw), lambda i, *_: (jnp.minimum((i + 1) * per, last), colblk))]


def _fill_padded(dm, pad_ref, x_ref, prev_ref, next_ref):
    R = _ROW_BLOCK
    r0 = pl.program_id(0) * R
    pos, L = dm.seq_pos(r0)
    pad_ref[pl.ds(_SUBLANES, R), :] = x_ref[...].astype(_F32)
    pad_ref[pl.ds(0, _SUBLANES), :] = jnp.where(pos == 0, 0.0, prev_ref[...].astype(_F32))
    pad_ref[pl.ds(_SUBLANES + R, _SUBLANES), :] = jnp.where(pos + R == L, 0.0, next_ref[...].astype(_F32))
    return pos, L


def _conv_body(x_ref, prev_ref, next_ref, w_ref, b_ref, o_ref, pad_ref, *, dm, act):
    _fill_padded(dm, pad_ref, x_ref, prev_ref, next_ref)
    R = _ROW_BLOCK
    width = w_ref.shape[0]
    left = (width - 1) // 2
    out = pad_ref[pl.ds(_SUBLANES - left, R), :] * w_ref[0:1, :]
    for tap in range(1, width):
        out = out + pad_ref[pl.ds(_SUBLANES - left + tap, R), :] * w_ref[tap:tap + 1, :]
    out = out + b_ref[...]
    if act:
        out = _silu(out)
    o_ref[...] = out.astype(o_ref.dtype)


def _conv(dm, slab, col, cw, w, b, act, out_dtype):
    R = _ROW_BLOCK
    width = w.shape[0]
    return _call(
        functools.partial(_conv_body, dm=dm, act=act), grid=(dm.T // R,),
        in_specs=_halo_specs(dm, cw, col // cw) + [pl.BlockSpec((width, cw), lambda i: (0, 0)),
                                                   pl.BlockSpec((1, cw), lambda i: (0, 0))],
        out_specs=pl.BlockSpec((R, cw), lambda i: (i, 0)),
        out_shape=jax.ShapeDtypeStruct((dm.T, cw), out_dtype),
        scratch=[pltpu.VMEM((R + 2 * _SUBLANES, cw), _F32)], name="conv")(slab, slab, slab, w, b.reshape(1, cw))


def _pool_body(x_ref, prev_ref, next_ref, w_ref, b_ref, sc_ref, _mix_in, o_ref, pad_ref, *, dm):
    pos, L = _fill_padded(dm, pad_ref, x_ref, prev_ref, next_ref)
    R = _ROW_BLOCK
    gd = w_ref.shape[1]
    t = pos + lax.broadcasted_iota(jnp.int32, (R, 1), 0)
    for gi, win in enumerate(POOL_WINDOWS):
        cs = slice(gi * gd, (gi + 1) * gd)
        half = win // 2
        total = pad_ref[pl.ds(_SUBLANES - half, R), cs]
        for k in range(1, win):
            total = total + pad_ref[pl.ds(_SUBLANES - half + k, R), cs]
        cnt = (jnp.minimum(t + (win - half), L) - jnp.maximum(t - half, 0)).astype(_F32)
        pooled = total / cnt - pad_ref[pl.ds(_SUBLANES, R), cs]
        y = jnp.dot(pooled.astype(_BF16), w_ref[gi], preferred_element_type=_F32) + b_ref[:, cs]
        o_ref[:, cs] = (y * sc_ref[:, cs]).astype(o_ref.dtype)


def _pool(dm, slab, w, b, scale, mix):
    R = _ROW_BLOCK
    ng, gd, _ = w.shape
    assert max(POOL_WINDOWS) // 2 <= _SUBLANES
    row = pl.BlockSpec((1, dm.Bw), lambda i: (0, 0))
    return _call(
        functools.partial(_pool_body, dm=dm), grid=(dm.T // R,),
        in_specs=_halo_specs(dm, dm.Bw, dm.s_xp // dm.Bw) + [pl.BlockSpec((ng, gd, gd), lambda i: (0, 0, 0)), row, row,
                                                              pl.BlockSpec(memory_space=pl.ANY)],
        out_specs=pl.BlockSpec((R, dm.Bw), lambda i: (i, _MIX_POOL)),
        out_shape=jax.ShapeDtypeStruct(mix.shape, mix.dtype), aliases={6: 0},
        scratch=[pltpu.VMEM((R + 2 * _SUBLANES, dm.Bw), _F32)], name="pool")(
            slab, slab, slab, w.astype(_BF16), b.reshape(1, -1), scale.reshape(1, -1), mix)


def _chunk_pos(dm, ci):
    ncc, ncl = dm.Lc // SSD_CHUNK, dm.Ll // SSD_CHUNK
    nct = dm.Tc // SSD_CHUNK
    is_ctx = ci < nct
    pos = jnp.where(is_ctx, ci % ncc, (ci - nct) % ncl)
    nseq = jnp.where(is_ctx, ncc, ncl)
    lat = jnp.clip((ci - nct) // ncl, 0, dm.DB - 1)
    return is_ctx, pos, nseq, lat


def _ssd_dir(dm, ci, rev, x_ref, bc_ref, dt_ref, bias_ref, a_ref, h0_ref, y_ref, s_ref, h_scr):
    Q, N, SH, P = SSD_CHUNK, dm.N, dm.SH, SSD_HEAD_DIM
    hg = SH // SSD_GROUPS
    gw = hg * P
    is_ctx, pos, nseq, _ = _chunk_pos(dm, ci)
    first = pos == (nseq - 1 if rev else 0)
    last = pos == (0 if rev else nseq - 1)

    @pl.when(jnp.logical_and(first, is_ctx))
    def _():
        h_scr[...] = jnp.zeros_like(h_scr)

    @pl.when(jnp.logical_and(first, jnp.logical_not(is_ctx)))
    def _():
        for k in range(dm.Bw // _LANES):
            h_scr[:, k * _LANES:(k + 1) * _LANES] = h0_ref[k * _LANES:(k + 1) * _LANES, :].T

    lane0 = (SH if rev else 0)
    dt = _softplus(dt_ref[...] + bias_ref[...])
    a = dt * a_ref[...]
    ri = lax.broadcasted_iota(jnp.int32, (Q, Q), 0)
    cj = lax.broadcasted_iota(jnp.int32, (Q, Q), 1)
    tri = (ri <= cj) if rev else (ri >= cj)
    cum = jnp.dot(tri.astype(_F32), a, preferred_element_type=_F32, precision=lax.Precision.HIGHEST)
    cum_t, dt_t = cum.T, dt.T
    ecum = jnp.exp(cum)
    end = 0 if rev else Q - 1
    wend = jnp.exp(cum[end:end + 1, :] - cum) * dt
    lane = lax.broadcasted_iota(jnp.int32, (Q, _LANES), 1)
    low = lane < P

    for g in range(SSD_GROUPS):
        b_g = bc_ref[:, g * N:(g + 1) * N]
        c_g = bc_ref[:, dm.GN + g * N:dm.GN + (g + 1) * N]
        cb = lax.dot_general(c_g, b_g, (((1,), (1,)), ((), ())), preferred_element_type=_F32)
        gs = slice(g * gw, (g + 1) * gw)
        y_diag, e_exp, w_exp = [], [], []
        for p in range(hg // 2):
            xs = slice(g * gw + p * _LANES, g * gw + (p + 1) * _LANES)
            xp = x_ref[:, xs].astype(_BF16)
            ys = []
            for s in range(2):
                ln = lane0 + g * hg + 2 * p + s
                seg = cum[:, ln:ln + 1] - cum_t[ln:ln + 1, :]
                m = cb * jnp.exp(jnp.where(tri, seg, _NEG)) * dt_t[ln:ln + 1, :]
                ys.append(jnp.dot(m.astype(_BF16), xp, preferred_element_type=_F32))
            ln = lane0 + g * hg + 2 * p
            y_diag.append(jnp.where(low, ys[0], ys[1]))
            e_exp.append(jnp.where(low, ecum[:, ln:ln + 1], ecum[:, ln + 1:ln + 2]))
            w_exp.append(jnp.where(low, wend[:, ln:ln + 1], wend[:, ln + 1:ln + 2]))
        y_diag = jnp.concatenate(y_diag, axis=1)
        e_exp = jnp.concatenate(e_exp, axis=1)
        w_exp = jnp.concatenate(w_exp, axis=1)
        h_in = h_scr[:, gs]
        y_off = jnp.dot(c_g, h_in.astype(_BF16), preferred_element_type=_F32) * e_exp
        y_ref[:, gs] = y_diag + y_off
        xw = (x_ref[:, gs] * w_exp).astype(_BF16)
        st = lax.dot_general(b_g, xw, (((0,), (0,)), ((), ())), preferred_element_type=_F32)
        h_scr[:, gs] = h_in * e_exp[end:end + 1, :] + st

    @pl.when(jnp.logical_and(last, is_ctx))
    def _():
        for k in range(dm.Bw // _LANES):
            s_ref[k * _LANES:(k + 1) * _LANES, :] = h_scr[:, k * _LANES:(k + 1) * _LANES].T


def _ssd_body(xf_ref, bcf_ref, dtf_ref, xb_ref, bcb_ref, dtb_ref, bias_ref, a_ref, h0f_ref, h0b_ref,
              yf_ref, yb_ref, sf_ref, sb_ref, hf_scr, hb_scr, *, dm):
    i = pl.program_id(0)
    n = pl.num_programs(0)
    _ssd_dir(dm, i, False, xf_ref, bcf_ref, dtf_ref, bias_ref, a_ref, h0f_ref, yf_ref, sf_ref, hf_scr)
    _ssd_dir(dm, n - 1 - i, True, xb_ref, bcb_ref, dtb_ref, bias_ref, a_ref, h0b_ref, yb_ref, sb_ref, hb_scr)


def _ssd(dm, slab, xs, bc, dt_bias, a_neg, state_ssd5, l):
    Q = SSD_CHUNK
    nchunks = dm.T // Q
    ncc = dm.Lc // Q
    HP = dm.SH * SSD_HEAD_DIM
    dcol = dm.s_dt // _LANES
    fwd = lambda i: i
    bwd = lambda i: nchunks - 1 - i

    def seq_c(ci):
        return jnp.minimum(ci // ncc, dm.B - 1)

    def lat_of(ci):
        return _chunk_pos(dm, ci)[3]

    def specs(order):
        return [pl.BlockSpec((Q, dm.Bw), lambda i: (order(i), 0)),
                pl.BlockSpec((Q, 2 * dm.GN), lambda i: (order(i), 0)),
                pl.BlockSpec((Q, _LANES), lambda i: (order(i), dcol))]

    row = pl.BlockSpec((1, _LANES), lambda i: (0, 0))
    h0 = lambda order, d: pl.BlockSpec((None, None, None, HP, dm.N), lambda i: (lat_of(order(i)), l, d, 0, 0))
    st = lambda order: pl.BlockSpec((None, HP, dm.N), lambda i: (seq_c(order(i)), 0, 0))
    ysd = jax.ShapeDtypeStruct((dm.T, dm.Bw), _F32)
    ssd = jax.ShapeDtypeStruct((dm.B, HP, dm.N), _F32)
    return _call(
        functools.partial(_ssd_body, dm=dm), grid=(nchunks,),
        in_specs=specs(fwd) + specs(bwd) + [row, row, h0(fwd, 0), h0(bwd, 1)],
        out_specs=[pl.BlockSpec((Q, dm.Bw), lambda i: (fwd(i), 0)), pl.BlockSpec((Q, dm.Bw), lambda i: (bwd(i), 0)),
                   st(fwd), st(bwd)],
        out_shape=[ysd, ysd, ssd, ssd],
        scratch=[pltpu.VMEM((dm.N, dm.Bw), _F32), pltpu.VMEM((dm.N, dm.Bw), _F32)], name="ssd")(
            xs, bc, slab, xs, bc, slab, dt_bias, a_neg, state_ssd5, state_ssd5)


def _lru_gates(dm, d, x_ref, wr_ref, wi_ref, br_ref, bi_ref, lam_ref, a_scr, u_scr):
    sp = _softplus(-lam_ref[d:d + 1, :])
    for h in range(dm.LH):
        hs = slice(h * _LANES, (h + 1) * _LANES)
        x = x_ref[:, hs]
        xb = x.astype(_BF16)
        r = jax.nn.sigmoid(jnp.dot(xb, wr_ref[d, h], preferred_element_type=_F32) + br_ref[d:d + 1, hs])
        ig = jax.nn.sigmoid(jnp.dot(xb, wi_ref[d, h], preferred_element_type=_F32) + bi_ref[d:d + 1, hs])
        log_a = -LRU_C * r * sp[:, hs]
        a = jnp.exp(log_a)
        a_scr[:, hs] = a
        u_scr[:, hs] = jnp.sqrt(-jnp.tanh(log_a) * (a * a + 1.0)) * (ig * x)


def _lru_body(xf_ref, xb_ref, wr_ref, wi_ref, br_ref, bi_ref, lam_ref, h0f_ref, h0b_ref,
              hf_ref, hb_ref, finf_ref, finb_ref, af, uf, ab, ub, carry, *, dm):
    R = _ROW_BLOCK
    i = pl.program_id(0)
    n = pl.num_programs(0)
    _lru_gates(dm, 0, xf_ref, wr_ref, wi_ref, br_ref, bi_ref, lam_ref, af, uf)
    _lru_gates(dm, 1, xb_ref, wr_ref, wi_ref, br_ref, bi_ref, lam_ref, ab, ub)

    posf, Lf = dm.seq_pos(i * R)
    posb, Lb = dm.seq_pos((n - 1 - i) * R)
    ctx_f = i * R < dm.Tc
    ctx_b = (n - 1 - i) * R < dm.Tc
    for d, start, is_ctx, h0_ref in ((0, posf == 0, ctx_f, h0f_ref), (1, posb + R == Lb, ctx_b, h0b_ref)):
        @pl.when(jnp.logical_and(start, is_ctx))
        def _():
            carry[d:d + 1, :] = jnp.zeros((1, dm.Bw), _F32)

        @pl.when(jnp.logical_and(start, jnp.logical_not(is_ctx)))
        def _():
            carry[d:d + 1, :] = h0_ref[d:d + 1, :]

    def tile(t, hs):
        hf, hb = hs
        base_f = pl.multiple_of(t * _SUBLANES, _SUBLANES)
        base_b = pl.multiple_of(R - _SUBLANES - t * _SUBLANES, _SUBLANES)
        for r in range(_SUBLANES):
            rf = base_f + r
            hf = af[pl.ds(rf, 1), :] * hf + uf[pl.ds(rf, 1), :]
            hf_ref[pl.ds(rf, 1), :] = hf
            rb = base_b + (_SUBLANES - 1 - r)
            hb = ab[pl.ds(rb, 1), :] * hb + ub[pl.ds(rb, 1), :]
            hb_ref[pl.ds(rb, 1), :] = hb
        return hf, hb

    hf, hb = lax.fori_loop(0, R // _SUBLANES, tile, (carry[0:1, :], carry[1:2, :]))
    carry[0:1, :] = hf
    carry[1:2, :] = hb

    @pl.when(jnp.logical_and(posf + R == Lf, ctx_f))
    def _():
        finf_ref[...] = hf

    @pl.when(jnp.logical_and(posb == 0, ctx_b))
    def _():
        finb_ref[...] = hb


def _lru(dm, xf, w_r, w_i, b_r, b_i, lam, state_lru, l):
    R = _ROW_BLOCK
    nblk = dm.T // R
    fwd = lambda i: i
    bwd = lambda i: nblk - 1 - i
    wspec = pl.BlockSpec((2, dm.LH, _LANES, _LANES), lambda i: (0, 0, 0, 0))
    vspec = pl.BlockSpec((2, dm.Bw), lambda i: (0, 0))
    lat_seq = lambda blk: jnp.clip((blk * R - dm.Tc) // dm.Ll, 0, dm.DB - 1)
    ctx_seq = lambda blk: jnp.minimum(blk * R // dm.Lc, dm.B - 1)
    h0 = lambda order: pl.BlockSpec((None, None, 2, dm.Bw), lambda i: (lat_seq(order(i)), l, 0, 0))
    fin = lambda order: pl.BlockSpec((None, 1, dm.Bw), lambda i: (ctx_seq(order(i)), 0, 0))
    hsd = jax.ShapeDtypeStruct((dm.T, dm.Bw), _F32)
    fsd = jax.ShapeDtypeStruct((dm.B, 1, dm.Bw), _F32)
    return _call(
        functools.partial(_lru_body, dm=dm), grid=(nblk,),
        in_specs=[pl.BlockSpec((R, dm.Bw), lambda i: (fwd(i), 0)), pl.BlockSpec((R, dm.Bw), lambda i: (bwd(i), 0)),
                  wspec, wspec, vspec, vspec, vspec, h0(fwd), h0(bwd)],
        out_specs=[pl.BlockSpec((R, dm.Bw), lambda i: (fwd(i), 0)), pl.BlockSpec((R, dm.Bw), lambda i: (bwd(i), 0)),
                   fin(fwd), fin(bwd)],
        out_shape=[hsd, hsd, fsd, fsd],
        scratch=[pltpu.VMEM((R, dm.Bw), _F32)] * 4 + [pltpu.VMEM((2, dm.Bw), _F32)], name="lru")(
            xf, xf, w_r, w_i, b_r, b_i, lam, state_lru, state_lru)


def _finish_body(yf_ref, yb_ref, xs_ref, z_ref, d_ref, ng_ref, hf_ref, hb_ref, gr_ref, _mix_in, o_ref):
    y = (yf_ref[...] + yb_ref[...] + d_ref[...] * xs_ref[...]) * _silu(z_ref[...])
    gw = y.shape[1] // SSD_GROUPS
    for g in range(SSD_GROUPS):
        gs = slice(g * gw, (g + 1) * gw)
        yg = y[:, gs]
        yg = yg * lax.rsqrt(jnp.mean(yg * yg, axis=-1, keepdims=True) + EPS)
        o_ref[:, gs] = (yg * ng_ref[:, gs]).astype(o_ref.dtype)
    bw = y.shape[1]
    o_ref[:, bw:] = ((hf_ref[...] + hb_ref[...]) * jax.nn.gelu(gr_ref[...], approximate=True)).astype(o_ref.dtype)


def _finish(dm, slab, yf, yb, xs, d_exp, norm_g, hf, hb, mix):
    R = _ROW_BLOCK
    blk = pl.BlockSpec((R, dm.Bw), lambda i: (i, 0))
    row = pl.BlockSpec((1, dm.Bw), lambda i: (0, 0))
    return _call(
        _finish_body, grid=(dm.T // R,),
        in_specs=[blk, blk, blk, pl.BlockSpec((R, dm.Bw), lambda i: (i, dm.s_z // dm.Bw)), row, row,
                  blk, blk, pl.BlockSpec((R, dm.Bw), lambda i: (i, dm.s_gr // dm.Bw)),
                  pl.BlockSpec(memory_space=pl.ANY)],
        out_specs=pl.BlockSpec((R, 2 * dm.Bw), lambda i: (i, _MIX_SSD_LRU)),
        out_shape=jax.ShapeDtypeStruct(mix.shape, mix.dtype), aliases={9: 0}, name="finish")(
            yf, yb, xs, slab, d_exp, norm_g.reshape(1, -1), hf, hb, slab, mix)


def _layer_norm(r, g, b):
    mu = jnp.mean(r, axis=-1, keepdims=True)
    c = r - mu
    var = jnp.mean(c * c, axis=-1, keepdims=True)
    return c * lax.rsqrt(var + EPS) * g + b


def _split_bf16(x):
    hi = x.astype(_BF16)
    return hi, (x - hi.astype(_F32)).astype(_BF16)


def _norm_router_body(mix_ref, x_ref, g1_ref, lg_ref, lb_ref, sc_ref, sh_ref, wr_ref, x1_ref, h_ref, s_ref, *, alpha):
    x1 = _layer_norm(alpha * x_ref[...] + g1_ref[...] * mix_ref[...], lg_ref[...], lb_ref[...])
    x1_ref[...] = x1
    h = x1 * (1.0 + sc_ref[...]) + sh_ref[...]
    h_ref[...] = h
    hh, hl = _split_bf16(h)
    wh, wl = _split_bf16(wr_ref[...])
    logits = (jnp.dot(hh, wh, preferred_element_type=_F32) + jnp.dot(hh, wl, preferred_element_type=_F32)
              + jnp.dot(hl, wh, preferred_element_type=_F32))
    s_ref[...] = jax.nn.sigmoid(logits)


def _norm_router(dm, mix, x, mod5, ln_g, ln_b, w_router_pad, l, alpha):
    tm = 256
    D = dm.D
    tile = pl.BlockSpec((tm, D), lambda i: (i, 0))
    row = pl.BlockSpec((1, D), lambda i: (0, 0))
    return _call(
        functools.partial(_norm_router_body, alpha=alpha), grid=(dm.T // tm,),
        in_specs=[tile, tile, _mod_spec(dm, l, 2, tm), row, row, _mod_spec(dm, l, 4, tm), _mod_spec(dm, l, 3, tm),
                  pl.BlockSpec((D, _LANES), lambda i: (0, 0))],
        out_specs=[tile, tile, pl.BlockSpec((tm, _LANES), lambda i: (i, 0))],
        out_shape=[jax.ShapeDtypeStruct((dm.T, D), _F32), jax.ShapeDtypeStruct((dm.T, D), _F32),
                   jax.ShapeDtypeStruct((dm.T, _LANES), _F32)], name="norm_router")(
            mix, x, mod5, ln_g.reshape(1, D), ln_b.reshape(1, D), mod5, mod5, w_router_pad)


def _top2(v):
    n = v.shape[-1]
    i1 = jnp.argmax(v, axis=-1)
    m1 = jnp.max(v, axis=-1)
    rest = jnp.where(jnp.arange(n) == i1[..., None], -jnp.inf, v)
    return (m1, jnp.max(rest, axis=-1)), (i1, jnp.argmax(rest, axis=-1))


def _route(s, router_bias, tm):
    T, E = s.shape
    per = E // N_EXPERT_GROUPS
    assert TOP_K == 2
    sel = s + router_bias.astype(_F32)
    grp_score = sum(_top2(sel.reshape(T, N_EXPERT_GROUPS, per))[0])
    grp = jnp.argmax(grp_score, axis=-1)
    in_grp = (jnp.arange(E) // per)[None, :] == grp[:, None]
    idx = jnp.stack(_top2(jnp.where(in_grp, sel, -jnp.inf))[1], axis=-1)
    wts = jnp.take_along_axis(s, idx, axis=-1)
    wts = wts / jnp.sum(wts, -1, keepdims=True)
    A = T * TOP_K
    onehot = (idx.reshape(A, 1) == jnp.arange(E, dtype=idx.dtype)[None, :]).astype(jnp.int32)
    csum = jnp.cumsum(onehot, axis=0)
    counts = csum[-1]
    padded = (counts + tm - 1) // tm * tm
    pend = jnp.cumsum(padded)
    pstart = pend - padded
    pos = jnp.sum(onehot * (csum - 1 + pstart[None, :]), axis=1).astype(jnp.int32)
    n_blocks = -(-A // tm) + E
    src_tok = jnp.zeros((n_blocks * tm,), jnp.int32).at[pos].set(jnp.arange(A, dtype=jnp.int32) // TOP_K)
    blk0 = jnp.arange(n_blocks, dtype=jnp.int32) * tm
    block_e = jnp.minimum(jnp.sum((blk0[:, None] >= pend[None, :]).astype(jnp.int32), axis=1), E - 1)
    n_used = (pend[-1] // tm).astype(jnp.int32).reshape(1)
    n_valid = jnp.clip(counts[block_e] - (blk0 - pstart[block_e]), 0, tm)
    n_valid = jnp.where(blk0 < pend[-1], n_valid, 0).astype(jnp.int32)
    return src_tok, block_e.astype(jnp.int32), n_used, n_valid, pos, wts


def _row_copy(src_hbm, row, dst, dst_row, sem):
    return pltpu.make_async_copy(src_hbm.at[pl.ds(row, 1)], dst.at[pl.ds(dst_row, 1)], sem)


def _gather_body(src_ref, nv_ref, h_hbm, o_ref, buf, sem):
    tm = o_ref.shape[0]
    i = pl.program_id(0)
    n = pl.num_programs(0)

    def issue(step, slot):
        def body(r, c):
            _row_copy(h_hbm, src_ref[step * tm + r], buf.at[slot], r, sem.at[slot]).start()
            return c
        lax.fori_loop(0, nv_ref[step], body, 0)

    @pl.when(i == 0)
    def _():
        buf[...] = jnp.zeros_like(buf)
        issue(0, 0)

    @pl.when(i + 1 < n)
    def _():
        issue(i + 1, (i + 1) % 2)

    slot = i % 2

    def wait(r, c):
        _row_copy(h_hbm, 0, buf.at[slot], r, sem.at[slot]).wait()
        return c
    lax.fori_loop(0, nv_ref[i], wait, 0)
    o_ref[...] = buf[slot].astype(o_ref.dtype)


def _gather(h, src_tok, n_valid, tm):
    D = h.shape[1]
    n_blocks = src_tok.shape[0] // tm
    return _call(
        _gather_body, grid=(n_blocks,), nprefetch=2,
        in_specs=[pl.BlockSpec(memory_space=pl.ANY)],
        out_specs=pl.BlockSpec((tm, D), lambda i, src, nv: (i, 0)),
        out_shape=jax.ShapeDtypeStruct((n_blocks * tm, D), _BF16),
        scratch=[pltpu.VMEM((2, tm, D), _F32), pltpu.SemaphoreType.DMA((2,))], name="moe_gather")(
            src_tok, n_valid, h)


def _expert_changed(be_ref, m):
    return jnp.logical_or(m == 0, be_ref[m] != be_ref[jnp.maximum(m - 1, 0)])


def _gateup_body(be_ref, nu_ref, x_ref, wg_ref, wu_ref, g_ref, wg_s, wu_s):
    m = pl.program_id(1)

    @pl.when(_expert_changed(be_ref, m))
    def _():
        wg_s[...] = wg_ref[...].astype(_BF16)
        wu_s[...] = wu_ref[...].astype(_BF16)

    @pl.when(m < nu_ref[0])
    def _():
        x = x_ref[...]
        gate = jnp.dot(x, wg_s[...], preferred_element_type=_F32)
        up = jnp.dot(x, wu_s[...], preferred_element_type=_F32)
        g_ref[...] = (_silu(gate) * up).astype(g_ref.dtype)

    @pl.when(m >= nu_ref[0])
    def _():
        g_ref[...] = jnp.zeros_like(g_ref)


def _gateup(xs, block_e, n_used, w_gate, w_up, l, tm):
    A_pad, D = xs.shape
    F = w_gate.shape[-1]
    tf = _pick_tile(F, (512, 256, 128))
    wspec = pl.BlockSpec((None, None, D, tf), lambda j, m, be, nu: (l, be[m], 0, j))
    return _call(
        _gateup_body, grid=(F // tf, A_pad // tm), nprefetch=2,
        in_specs=[pl.BlockSpec((tm, D), lambda j, m, be, nu: (m, 0)), wspec, wspec],
        out_specs=pl.BlockSpec((tm, tf), lambda j, m, be, nu: (m, j)),
        out_shape=jax.ShapeDtypeStruct((A_pad, F), _BF16),
        scratch=[pltpu.VMEM((D, tf), _BF16)] * 2, name="moe_gateup")(block_e, n_used, xs, w_gate, w_up)


def _down_body(be_ref, nu_ref, g_ref, wd_ref, y_ref, wd_s):
    m = pl.program_id(1)

    @pl.when(_expert_changed(be_ref, m))
    def _():
        wd_s[...] = wd_ref[...].astype(_BF16)

    @pl.when(m < nu_ref[0])
    def _():
        y_ref[...] = jnp.dot(g_ref[...], wd_s[...], preferred_element_type=_F32)

    @pl.when(m >= nu_ref[0])
    def _():
        y_ref[...] = jnp.zeros_like(y_ref)


def _down(g, block_e, n_used, w_down, l, tm):
    A_pad, F = g.shape
    D = w_down.shape[-1]
    tn = _pick_tile(D, (2048, 1024, 512, 256, 128))
    return _call(
        _down_body, grid=(D // tn, A_pad // tm), nprefetch=2,
        in_specs=[pl.BlockSpec((tm, F), lambda j, m, be, nu: (m, 0)),
                  pl.BlockSpec((None, None, F, tn), lambda j, m, be, nu: (l, be[m], 0, j))],
        out_specs=pl.BlockSpec((tm, tn), lambda j, m, be, nu: (m, j)),
        out_shape=jax.ShapeDtypeStruct((A_pad, D), _F32),
        scratch=[pltpu.VMEM((F, tn), _BF16)], name="moe_down")(block_e, n_used, g, w_down)


def _combine_body(pos_ref, y_hbm, x1_ref, w_ref, g2_ref, lg_ref, lb_ref, *rest, alpha, ctx_tiles):
    if ctx_tiles is None:
        sc_ref, sh_ref, x2_ref, hn_ref, buf, sem = rest
    else:
        xp_ref, xs_ref, buf, sem = rest
    tm = x1_ref.shape[0]
    i = pl.program_id(0)
    n = pl.num_programs(0)

    def issue(step, slot):
        def body(r, c):
            for k in range(TOP_K):
                _row_copy(y_hbm, pos_ref[(step * tm + r) * TOP_K + k], buf.at[slot, k], r, sem.at[slot]).start()
            return c
        lax.fori_loop(0, tm, body, 0)

    @pl.when(i == 0)
    def _():
        issue(0, 0)

    @pl.when(i + 1 < n)
    def _():
        issue(i + 1, (i + 1) % 2)

    slot = i % 2

    def wait(r, c):
        for k in range(TOP_K):
            _row_copy(y_hbm, 0, buf.at[slot, k], r, sem.at[slot]).wait()
        return c
    lax.fori_loop(0, tm, wait, 0)
    w = w_ref[...]
    ffn = w[:, 0:1] * buf[slot, 0]
    for k in range(1, TOP_K):
        ffn = ffn + w[:, k:k + 1] * buf[slot, k]
    x2 = _layer_norm(alpha * x1_ref[...] + g2_ref[...] * ffn, lg_ref[...], lb_ref[...])
    if ctx_tiles is None:
        x2_ref[...] = x2
        hn_ref[...] = (x2 * (1.0 + sc_ref[...]) + sh_ref[...]).astype(hn_ref.dtype)
    else:
        @pl.when(i < ctx_tiles)
        def _():
            xp_ref[...] = x2

        @pl.when(i >= ctx_tiles)
        def _():
            xs_ref[...] = x2


def _combine(dm, y, pos, wts, x1, mod5, ln_g, ln_b, l, alpha):
    tm = 256
    D = dm.D
    final = l == dm.depth - 1
    ctx_tiles = dm.Tc // tm
    tile = pl.BlockSpec((tm, D), lambda i, p: (i, 0))
    row = pl.BlockSpec((1, D), lambda i, p: (0, 0))
    in_specs = [pl.BlockSpec(memory_space=pl.ANY), tile, pl.BlockSpec((tm, TOP_K), lambda i, p: (i, 0)),
                _mod_spec(dm, l, 5, tm), row, row]
    args = [pos, y, x1, wts, mod5, ln_g.reshape(1, D), ln_b.reshape(1, D)]
    if final:
        out_specs = [pl.BlockSpec((tm, D), lambda i, p: (jnp.minimum(i, ctx_tiles - 1), 0)),
                     pl.BlockSpec((tm, D), lambda i, p: (jnp.maximum(i - ctx_tiles, 0), 0))]
        out_shape = [jax.ShapeDtypeStruct((dm.Tc, D), _F32), jax.ShapeDtypeStruct((dm.Tl, D), _F32)]
    else:
        in_specs += [_mod_spec(dm, l + 1, 1, tm), _mod_spec(dm, l + 1, 0, tm)]
        args += [mod5, mod5]
        out_specs = [tile, tile]
        out_shape = [jax.ShapeDtypeStruct((dm.T, D), _F32), jax.ShapeDtypeStruct((dm.T, D), _BF16)]
    return _call(
        functools.partial(_combine_body, alpha=alpha, ctx_tiles=ctx_tiles if final else None),
        grid=(dm.T // tm,), nprefetch=1, in_specs=in_specs, out_specs=out_specs, out_shape=out_shape,
        scratch=[pltpu.VMEM((2, TOP_K, tm, D), _F32), pltpu.SemaphoreType.DMA((2,))], name="moe_combine")(*args)


def kernel(x_prompt, x_sample, cache_k, cache_v, state_ssd, state_lru, c, c_ctx, w_ada, b_ada, w_in, w_out, attn_sink, ssd_conv_w, ssd_conv_b, ssd_a_log, ssd_dt_bias, ssd_d, ssd_norm_g, lru_conv_w, lru_conv_b, lru_lambda, lru_w_r, lru_b_r, lru_w_i, lru_b_i, pool_w, pool_b, pool_scale, ln1_g, ln1_b, ln2_g, ln2_b, w_router, router_bias, moe_w_gate, moe_w_up, moe_w_down):
    dm = _Dims(x_prompt, x_sample, cache_k, state_ssd, lru_w_r, w_router, moe_w_gate, w_ada)
    depth, D, Bw, KVw, GN, SH = dm.depth, dm.D, dm.Bw, dm.KVw, dm.GN, dm.SH
    alpha = (2 * depth) ** 0.25

    rows = -(-(1 + dm.DB) // _SUBLANES) * _SUBLANES
    cond = jnp.zeros((rows, D), _F32).at[0].set(c_ctx).at[1:1 + dm.DB].set(c)
    mod5 = _ada(cond, w_ada, b_ada).reshape(depth, rows, 6, 1, D)

    x = jnp.concatenate([x_prompt.reshape(dm.Tc, D), x_sample.reshape(dm.Tl, D)], axis=0)
    h = _modulate(dm, x, mod5, 0)

    cos, sin = _rope_tables(dm.Ll)
    cache_k4 = cache_k.reshape(dm.DB, depth, dm.past, KVw)
    cache_v4 = cache_v.reshape(dm.DB, depth, dm.past, KVw)
    state_ssd5 = state_ssd.reshape(dm.DB, depth, 2, SH * SSD_HEAD_DIM, dm.N)
    w_router_pad = jnp.zeros((D, _LANES), _F32).at[:, :dm.E].set(w_router)

    o_k, o_v, o_z = Bw, Bw + KVw, Bw + 2 * KVw
    o_xs = o_z + Bw
    o_B = o_xs + Bw
    o_dt = o_B + 2 * GN
    o_xr = o_dt + 2 * SH
    o_gr, o_xp = o_xr + Bw, o_xr + 2 * Bw

    ks, vs, ssds, lrus = [], [], [], []
    for l in range(depth):
        wi = w_in[l]
        cols = lambda o, w: wi[:, o:o + w]
        w_slab = jnp.concatenate(
            [cols(0, Bw), cols(o_z, Bw), cols(o_xs, Bw), cols(o_xr, Bw), cols(o_gr, Bw), cols(o_xp, Bw),
             cols(o_k, KVw), cols(o_v, KVw), cols(o_B, 2 * GN), cols(o_dt, 2 * SH),
             jnp.zeros((D, dm.slab_w - dm.s_dt - 2 * SH), wi.dtype)], axis=1).astype(_BF16)
        slab = _matmul(h, w_slab, "in_proj")
        mix_in = jnp.zeros((dm.T, D), _BF16)

        mix_in = _attn_ctx(dm, slab, attn_sink[l], mix_in)
        qr, kr = _rope(dm, slab, cos, sin)
        mix_in = _attn_lat(dm, slab, qr, kr, cache_k4, cache_v4, attn_sink[l], l, mix_in)

        cw = ssd_conv_w[l]
        xs = _conv(dm, slab, dm.s_xs, Bw, cw[:, :Bw], ssd_conv_b[l][:Bw], True, _F32)
        bc = _conv(dm, slab, dm.s_B, 2 * GN, cw[:, Bw:], ssd_conv_b[l][Bw:], True, _BF16)
        pad = jnp.zeros((_LANES - 2 * SH,), _F32)
        dt_bias = jnp.concatenate([ssd_dt_bias[l].reshape(-1), pad]).reshape(1, _LANES)
        a_neg = jnp.concatenate([-jnp.exp(ssd_a_log[l].reshape(-1)), pad]).reshape(1, _LANES)
        yf, yb, s_f, s_b = _ssd(dm, slab, xs, bc, dt_bias, a_neg, state_ssd5, l)

        xf = _conv(dm, slab, dm.s_xr, Bw, lru_conv_w[l], lru_conv_b[l], False, _F32)
        hf, hb, fin_f, fin_b = _lru(dm, xf, lru_w_r[l].astype(_BF16), lru_w_i[l].astype(_BF16),
                                    lru_b_r[l], lru_b_i[l], lru_lambda[l], state_lru, l)

        d_exp = jnp.repeat(ssd_d[l], SSD_HEAD_DIM).reshape(1, Bw)
        mix_in = _finish(dm, slab, yf, yb, xs, d_exp, ssd_norm_g[l], hf, hb, mix_in)
        mix_in = _pool(dm, slab, pool_w[l], pool_b[l], pool_scale[l], mix_in)

        wo = w_out[l]
        w_mix = jnp.concatenate([wo[Bw:3 * Bw], wo[:Bw], wo[3 * Bw:]], axis=0).astype(_BF16)
        mix = _matmul(mix_in, w_mix, "out_proj")
        x1, h2, s_pad = _norm_router(dm, mix, x, mod5, ln1_g[l], ln1_b[l], w_router_pad, l, alpha)

        src_tok, block_e, n_used, n_valid, pos, wts = _route(s_pad[:, :dm.E], router_bias, _MOE_TM)
        xs_sorted = _gather(h2, src_tok, n_valid, _MOE_TM)
        g = _gateup(xs_sorted, block_e, n_used, moe_w_gate, moe_w_up, l, _MOE_TM)
        y = _down(g, block_e, n_used, moe_w_down, l, _MOE_TM)
        x, h = _combine(dm, y, pos, wts, x1, mod5, ln2_g[l], ln2_b[l], l, alpha)

        ks.append(slab[:dm.Tc, dm.s_k:dm.s_k + KVw].reshape(dm.B, dm.Lc, dm.KV, ATT_HEAD_DIM))
        vs.append(slab[:dm.Tc, dm.s_v:dm.s_v + KVw].reshape(dm.B, dm.Lc, dm.KV, ATT_HEAD_DIM))
        ssds.append(jnp.stack([s_f, s_b], axis=1).reshape(dm.B, 2, SH, SSD_HEAD_DIM, dm.N))
        lrus.append(jnp.concatenate([fin_f, fin_b], axis=1))

    y_prompt = x.reshape(dm.B, dm.Lc, D)
    y_sample = h.reshape(dm.DB, dm.Ll, D)
    return (y_prompt, y_sample, jnp.stack(ks, axis=1), jnp.stack(vs, axis=1),
            jnp.stack(ssds, axis=1), jnp.stack(lrus, axis=1))
```

```python
import functools
import math

import jax
import jax.numpy as jnp
import numpy as np
from jax import lax
from jax.experimental import pallas as pl
from jax.experimental.pallas import tpu as pltpu

_F32 = jnp.float32
_BF16 = jnp.bfloat16

GRID_W = 64
ATT_HEAD_DIM = 128
ATT_BLOCK = 128
ROPE_BASE = 10000.0
SSD_HEAD_DIM = 64
SSD_GROUPS = 2
SSD_CHUNK = 128
LRU_C = 8.0
POOL_WINDOWS = (2, 4, 8, 16)
N_EXPERT_GROUPS = 4
TOP_K = 2
EPS = 1e-5

_LANES = 128
_SUBLANES = 8
_VMEM_BYTES_V7X = 64 * 1024 * 1024
_VMEM_LIMIT = _VMEM_BYTES_V7X * 7 // 8

_ROW_BLOCK = 256
_MOE_TM = 512
_NEG = -1e30
_MIX_SSD_LRU, _MIX_ATT, _MIX_POOL = 0, 2, 3


def _call(body, *, grid, in_specs, out_specs, out_shape, scratch=(), nprefetch=0, aliases=None, name=None):
    grid_spec = pltpu.PrefetchScalarGridSpec(
        num_scalar_prefetch=nprefetch, grid=grid, in_specs=in_specs, out_specs=out_specs,
        scratch_shapes=list(scratch))
    return pl.pallas_call(
        body, grid_spec=grid_spec, out_shape=out_shape,
        compiler_params=pltpu.CompilerParams(
            dimension_semantics=("arbitrary",) * len(grid), vmem_limit_bytes=_VMEM_LIMIT),
        input_output_aliases=aliases or {}, name=name)


def _softplus(x):
    return jnp.maximum(x, 0.0) + jnp.log1p(jnp.exp(-jnp.abs(x)))


def _silu(x):
    return x * jax.nn.sigmoid(x)


class _Dims:
    def __init__(self, x_prompt, x_sample, cache_k, state_ssd, lru_w_r, w_router, moe_w_gate, w_ada):
        self.B, self.Lc, self.D = x_prompt.shape
        self.DB, self.Ll, _ = x_sample.shape
        self.depth = w_ada.shape[0]
        self.Tc, self.Tl = self.B * self.Lc, self.DB * self.Ll
        self.T = self.Tc + self.Tl
        self.Bw = self.D // 4
        self.past, self.KV = cache_k.shape[2], cache_k.shape[3]
        self.H = self.Bw // ATT_HEAD_DIM
        self.G = self.H // self.KV
        self.KVw = self.KV * ATT_HEAD_DIM
        self.SH = self.Bw // SSD_HEAD_DIM
        self.N = state_ssd.shape[-1]
        self.GN = SSD_GROUPS * self.N
        self.LH = lru_w_r.shape[2]
        self.E = w_router.shape[1]
        self.F = moe_w_gate.shape[-1]
        Bw, KVw, GN = self.Bw, self.KVw, self.GN
        self.s_q, self.s_z, self.s_xs, self.s_xr, self.s_gr, self.s_xp = (i * Bw for i in range(6))
        self.s_k = 6 * Bw
        self.s_v = self.s_k + KVw
        self.s_B = self.s_v + KVw
        self.s_C = self.s_B + GN
        self.s_dt = self.s_C + GN
        used = self.s_dt + _LANES
        self.slab_w = -(-used // 512) * 512
        assert 2 * self.SH <= _LANES and self.N == _LANES and ATT_HEAD_DIM == _LANES
        assert self.s_k % KVw == 0 and self.s_B % (2 * GN) == 0 and self.Bw % (2 * _LANES) == 0
        assert self.Lc % _ROW_BLOCK == 0 and self.Ll % _ROW_BLOCK == 0
        assert self.Lc % SSD_CHUNK == 0 and self.Ll % SSD_CHUNK == 0 and self.Ll % ATT_BLOCK == 0
        assert self.Bw // self.LH == _LANES and self.Bw % len(POOL_WINDOWS) == 0

    def seg_of_row(self, r0):
        return jnp.where(r0 < self.Tc, 0, 1 + (r0 - self.Tc) // self.Ll)

    def seq_pos(self, r0):
        is_ctx = r0 < self.Tc
        pos = jnp.where(is_ctx, r0 % self.Lc, (r0 - self.Tc) % self.Ll)
        return pos, jnp.where(is_ctx, self.Lc, self.Ll)


def _ada_body(c_ref, w_ref, b_ref, o_ref):
    s = _silu(c_ref[...]).astype(_BF16)
    o_ref[...] = jnp.dot(s, w_ref[...].astype(_BF16), preferred_element_type=_F32) + b_ref[...]


def _ada(cond, w_ada, b_ada):
    depth, D, n6 = w_ada.shape
    rows = cond.shape[0]
    tn = 512
    return _call(
        _ada_body, grid=(depth, n6 // tn),
        in_specs=[pl.BlockSpec((rows, D), lambda l, j: (0, 0)),
                  pl.BlockSpec((None, D, tn), lambda l, j: (l, 0, j)),
                  pl.BlockSpec((None, 1, tn), lambda l, j: (l, 0, j))],
        out_specs=pl.BlockSpec((None, rows, tn), lambda l, j: (l, 0, j)),
        out_shape=jax.ShapeDtypeStruct((depth, rows, n6), _F32), name="ada")(
            cond, w_ada, b_ada.reshape(depth, 1, n6))


def _mod_spec(dm, l, k, tm):
    return pl.BlockSpec((None, None, None, 1, dm.D), lambda i, *_: (l, dm.seg_of_row(i * tm), k, 0, 0))


def _modulate_body(xc_ref, xl_ref, sh_ref, sc_ref, o_ref, *, ctx_tiles):
    x = jnp.where(pl.program_id(0) < ctx_tiles, xc_ref[...], xl_ref[...])
    o_ref[...] = (x * (1.0 + sc_ref[...]) + sh_ref[...]).astype(o_ref.dtype)


def _modulate(dm, xc, xl, mod5, l):
    tm = 512
    return _call(
        functools.partial(_modulate_body, ctx_tiles=dm.Tc // tm), grid=(dm.T // tm,),
        in_specs=_stream_specs(dm, tm, dm.D) + [_mod_spec(dm, l, 0, tm), _mod_spec(dm, l, 1, tm)],
        out_specs=pl.BlockSpec((tm, dm.D), lambda i: (i, 0)),
        out_shape=jax.ShapeDtypeStruct((dm.T, dm.D), _BF16), name="modulate")(xc, xl, mod5, mod5)


def _mm_body(x_ref, w_ref, o_ref):
    o_ref[...] = jnp.dot(x_ref[...], w_ref[...], preferred_element_type=_F32).astype(o_ref.dtype)


def _pick_tile(n, candidates):
    for c in candidates:
        if n % c == 0:
            return c
    raise ValueError(f"no tile for {n}")


def _matmul(x, w, name, out_dtype=_F32):
    M, K = x.shape
    N = w.shape[1]
    tm = _pick_tile(M, (512, 256, 128))
    tn = _pick_tile(N, (1536, 1280, 1024, 768, 512))
    return _call(
        _mm_body, grid=(N // tn, M // tm),
        in_specs=[pl.BlockSpec((tm, K), lambda j, i: (i, 0)), pl.BlockSpec((K, tn), lambda j, i: (0, j))],
        out_specs=pl.BlockSpec((tm, tn), lambda j, i: (i, j)),
        out_shape=jax.ShapeDtypeStruct((M, N), out_dtype), name=name)(x, w)


def _rope_body(q_ref, k_ref, cos_ref, sin_ref, qo_ref, ko_ref):
    cos, sin = cos_ref[...], sin_ref[...]
    quarter = ATT_HEAD_DIM // 4
    lane = lax.broadcasted_iota(jnp.int32, cos.shape, 1)
    first = (lane % (2 * quarter)) < quarter

    def rot(x):
        partner = jnp.where(first, pltpu.roll(x, ATT_HEAD_DIM - quarter, 1), pltpu.roll(x, quarter, 1))
        return x * cos + partner * sin

    for ref, out in ((q_ref, qo_ref), (k_ref, ko_ref)):
        for h in range(ref.shape[1] // ATT_HEAD_DIM):
            sl = slice(h * ATT_HEAD_DIM, (h + 1) * ATT_HEAD_DIM)
            out[:, sl] = rot(ref[:, sl]).astype(out.dtype)


def _rope_tables(L):
    quarter = ATT_HEAD_DIM // 4
    t = jnp.arange(L)
    inv = ROPE_BASE ** (-jnp.arange(quarter, dtype=_F32) / quarter)
    ang_r = (t // GRID_W).astype(_F32)[:, None] * inv[None, :]
    ang_c = (t % GRID_W).astype(_F32)[:, None] * inv[None, :]
    cos = jnp.concatenate([jnp.cos(ang_r)] * 2 + [jnp.cos(ang_c)] * 2, -1)
    sin = jnp.concatenate([-jnp.sin(ang_r), jnp.sin(ang_r), -jnp.sin(ang_c), jnp.sin(ang_c)], -1)
    return cos, sin


def _rope(dm, slab, cos, sin):
    tr = _ROW_BLOCK
    r0, per_seq = dm.Tc // tr, dm.Ll // tr
    return _call(
        _rope_body, grid=(dm.Tl // tr,),
        in_specs=[pl.BlockSpec((tr, dm.Bw), lambda i: (r0 + i, dm.s_q // dm.Bw)),
                  pl.BlockSpec((tr, dm.KVw), lambda i: (r0 + i, dm.s_k // dm.KVw)),
                  pl.BlockSpec((tr, _LANES), lambda i: (i % per_seq, 0)),
                  pl.BlockSpec((tr, _LANES), lambda i: (i % per_seq, 0))],
        out_specs=[pl.BlockSpec((tr, dm.Bw), lambda i: (i, 0)), pl.BlockSpec((tr, dm.KVw), lambda i: (i, 0))],
        out_shape=[jax.ShapeDtypeStruct((dm.Tl, dm.Bw), _BF16), jax.ShapeDtypeStruct((dm.Tl, dm.KVw), _BF16)],
        name="rope")(slab, slab, cos, sin)


def _attend(sink_ref, q, k_all, v_all, bias, o_ref, kv, G):
    Q = q.shape[0]
    q_st = jnp.concatenate([q[:, g * ATT_HEAD_DIM:(g + 1) * ATT_HEAD_DIM] for g in range(G)], axis=0)
    s = lax.dot_general(q_st, k_all, (((1,), (1,)), ((), ())), preferred_element_type=_F32)
    s = s * (ATT_HEAD_DIM ** -0.5)
    if bias is not None:
        s = s + bias
    sk = jnp.concatenate([jnp.full((Q, 1), sink_ref[kv * G + g], _F32) for g in range(G)], axis=0)
    m = jnp.maximum(jnp.max(s, axis=-1, keepdims=True), sk)
    p = jnp.exp(s - m)
    denom = jnp.sum(p, axis=-1, keepdims=True) + jnp.exp(sk - m)
    o = jnp.dot(p.astype(_BF16), v_all, preferred_element_type=_F32) / denom
    for g in range(G):
        h = kv * G + g
        o_ref[:, h * ATT_HEAD_DIM:(h + 1) * ATT_HEAD_DIM] = o[g * Q:(g + 1) * Q].astype(o_ref.dtype)


def _attn_ctx_body(sink_ref, q_ref, k_ref, v_ref, _mix_in, o_ref, *, KV, G):
    for kv in range(KV):
        hs = slice(kv * ATT_HEAD_DIM, (kv + 1) * ATT_HEAD_DIM)
        q = q_ref[:, kv * G * ATT_HEAD_DIM:(kv + 1) * G * ATT_HEAD_DIM].astype(_BF16)
        _attend(sink_ref, q, k_ref[:, hs].astype(_BF16), v_ref[:, hs].astype(_BF16), None, o_ref, kv, G)


def _attn_ctx(dm, slab, sink, mix):
    Lc = dm.Lc
    return _call(
        functools.partial(_attn_ctx_body, KV=dm.KV, G=dm.G), grid=(dm.B,),
        in_specs=[pl.BlockSpec(memory_space=pltpu.SMEM),
                  pl.BlockSpec((Lc, dm.Bw), lambda b: (b, dm.s_q // dm.Bw)),
                  pl.BlockSpec((Lc, dm.KVw), lambda b: (b, dm.s_k // dm.KVw)),
                  pl.BlockSpec((Lc, dm.KVw), lambda b: (b, dm.s_v // dm.KVw)),
                  pl.BlockSpec(memory_space=pl.ANY)],
        out_specs=pl.BlockSpec((Lc, dm.Bw), lambda b: (b, _MIX_ATT)),
        out_shape=jax.ShapeDtypeStruct(mix.shape, mix.dtype), aliases={4: 0}, name="attn_ctx")(
            sink, slab, slab, slab, mix)


def _attn_lat_body(sink_ref, q_ref, kp_ref, kc_ref, kn_ref, vp_ref, vc_ref, vn_ref, ck_ref, cv_ref, _mix_in,
                   o_ref, *, KV, G, nb):
    n = pl.program_id(1)
    Q = ATT_BLOCK
    S = 3 * Q + ck_ref.shape[0]
    rows = G * Q
    c_i = lax.broadcasted_iota(jnp.int32, (rows, S), 1)
    qi = lax.broadcasted_iota(jnp.int32, (rows, S), 0) % Q
    bad_prev = c_i < jnp.maximum(qi, jnp.where(n == 0, Q, 0))
    bad_next = jnp.logical_and(c_i - 2 * Q > qi - jnp.where(n == nb - 1, Q, 0), c_i < 3 * Q)
    bias = jnp.where(jnp.logical_or(bad_prev, bad_next), _NEG, 0.0)
    for kv in range(KV):
        hs = slice(kv * ATT_HEAD_DIM, (kv + 1) * ATT_HEAD_DIM)
        k_all = jnp.concatenate([kp_ref[:, hs], kc_ref[:, hs], kn_ref[:, hs], ck_ref[:, hs].astype(_BF16)], axis=0)
        v_all = jnp.concatenate([vp_ref[:, hs].astype(_BF16), vc_ref[:, hs].astype(_BF16),
                                 vn_ref[:, hs].astype(_BF16), cv_ref[:, hs].astype(_BF16)], axis=0)
        q = q_ref[:, kv * G * ATT_HEAD_DIM:(kv + 1) * G * ATT_HEAD_DIM]
        _attend(sink_ref, q, k_all, v_all, bias, o_ref, kv, G)


def _attn_lat(dm, slab, qr, kr, cache_k4, cache_v4, sink, l, mix):
    Q = ATT_BLOCK
    nb = dm.Ll // Q
    r0 = dm.Tc // Q
    vcol = dm.s_v // dm.KVw
    prev = lambda n: jnp.maximum(n - 1, 0)
    nxt = lambda n: jnp.minimum(n + 1, nb - 1)
    cache_spec = pl.BlockSpec((None, None, dm.past, dm.KVw), lambda b, n: (b, l, 0, 0))
    return _call(
        functools.partial(_attn_lat_body, KV=dm.KV, G=dm.G, nb=nb), grid=(dm.DB, nb),
        in_specs=[pl.BlockSpec(memory_space=pltpu.SMEM),
                  pl.BlockSpec((Q, dm.Bw), lambda b, n: (b * nb + n, 0)),
                  pl.BlockSpec((Q, dm.KVw), lambda b, n: (b * nb + prev(n), 0)),
                  pl.BlockSpec((Q, dm.KVw), lambda b, n: (b * nb + n, 0)),
                  pl.BlockSpec((Q, dm.KVw), lambda b, n: (b * nb + nxt(n), 0)),
                  pl.BlockSpec((Q, dm.KVw), lambda b, n: (r0 + b * nb + prev(n), vcol)),
                  pl.BlockSpec((Q, dm.KVw), lambda b, n: (r0 + b * nb + n, vcol)),
                  pl.BlockSpec((Q, dm.KVw), lambda b, n: (r0 + b * nb + nxt(n), vcol)),
                  cache_spec, cache_spec, pl.BlockSpec(memory_space=pl.ANY)],
        out_specs=pl.BlockSpec((Q, dm.Bw), lambda b, n: (r0 + b * nb + n, _MIX_ATT)),
        out_shape=jax.ShapeDtypeStruct(mix.shape, mix.dtype), aliases={10: 0}, name="attn_lat")(
            sink, qr, kr, kr, kr, slab, slab, slab, cache_k4, cache_v4, mix)


def _halo_specs(dm, cw, colblk):
    R = _ROW_BLOCK
    per = R // _SUBLANES
    last = dm.T // _SUBLANES - 1
    return [pl.BlockSpec((R, cw), lambda i, *_: (i, colblk)),
            pl.BlockSpec((_SUBLANES, cw), lambda i, *_: (jnp.maximum(i * per - 1, 0), colblk)),
            pl.BlockSpec((_SUBLANES, cw), lambda i, *_: (jnp.minimum((i + 1) * per, last), colblk))]


def _fill_padded(dm, pad_ref, x_ref, prev_ref, next_ref):
    R = _ROW_BLOCK
    r0 = pl.program_id(0) * R
    pos, L = dm.seq_pos(r0)
    pad_ref[pl.ds(_SUBLANES, R), :] = x_ref[...].astype(_F32)
    pad_ref[pl.ds(0, _SUBLANES), :] = jnp.where(pos == 0, 0.0, prev_ref[...].astype(_F32))
    pad_ref[pl.ds(_SUBLANES + R, _SUBLANES), :] = jnp.where(pos + R == L, 0.0, next_ref[...].astype(_F32))
    return pos, L


def _conv_body(x_ref, prev_ref, next_ref, w_ref, b_ref, o_ref, pad_ref, *, dm, act):
    _fill_padded(dm, pad_ref, x_ref, prev_ref, next_ref)
    R = _ROW_BLOCK
    width = w_ref.shape[0]
    left = (width - 1) // 2
    out = pad_ref[pl.ds(_SUBLANES - left, R), :] * w_ref[0:1, :]
    for tap in range(1, width):
        out = out + pad_ref[pl.ds(_SUBLANES - left + tap, R), :] * w_ref[tap:tap + 1, :]
    out = out + b_ref[...]
    if act:
        out = _silu(out)
    o_ref[...] = out.astype(o_ref.dtype)


def _conv(dm, slab, col, cw, w, b, act, out_dtype):
    R = _ROW_BLOCK
    width = w.shape[0]
    return _call(
        functools.partial(_conv_body, dm=dm, act=act), grid=(dm.T // R,),
        in_specs=_halo_specs(dm, cw, col // cw) + [pl.BlockSpec((width, cw), lambda i: (0, 0)),
                                                   pl.BlockSpec((1, cw), lambda i: (0, 0))],
        out_specs=pl.BlockSpec((R, cw), lambda i: (i, 0)),
        out_shape=jax.ShapeDtypeStruct((dm.T, cw), out_dtype),
        scratch=[pltpu.VMEM((R + 2 * _SUBLANES, cw), _F32)], name="conv")(slab, slab, slab, w, b.reshape(1, cw))


def _pool_body(x_ref, prev_ref, next_ref, w_ref, b_ref, sc_ref, _mix_in, o_ref, pad_ref, *, dm):
    pos, L = _fill_padded(dm, pad_ref, x_ref, prev_ref, next_ref)
    R = _ROW_BLOCK
    gd = w_ref.shape[1]
    t = pos + lax.broadcasted_iota(jnp.int32, (R, 1), 0)
    for gi, win in enumerate(POOL_WINDOWS):
        cs = slice(gi * gd, (gi + 1) * gd)
        half = win // 2
        total = pad_ref[pl.ds(_SUBLANES - half, R), cs]
        for k in range(1, win):
            total = total + pad_ref[pl.ds(_SUBLANES - half + k, R), cs]
        cnt = (jnp.minimum(t + (win - half), L) - jnp.maximum(t - half, 0)).astype(_F32)
        pooled = total / cnt - pad_ref[pl.ds(_SUBLANES, R), cs]
        y = jnp.dot(pooled.astype(_BF16), w_ref[gi], preferred_element_type=_F32) + b_ref[:, cs]
        o_ref[:, cs] = (y * sc_ref[:, cs]).astype(o_ref.dtype)


def _pool(dm, slab, w, b, scale, mix):
    R = _ROW_BLOCK
    ng, gd, _ = w.shape
    assert max(POOL_WINDOWS) // 2 <= _SUBLANES
    row = pl.BlockSpec((1, dm.Bw), lambda i: (0, 0))
    return _call(
        functools.partial(_pool_body, dm=dm), grid=(dm.T // R,),
        in_specs=_halo_specs(dm, dm.Bw, dm.s_xp // dm.Bw) + [pl.BlockSpec((ng, gd, gd), lambda i: (0, 0, 0)), row, row,
                                                              pl.BlockSpec(memory_space=pl.ANY)],
        out_specs=pl.BlockSpec((R, dm.Bw), lambda i: (i, _MIX_POOL)),
        out_shape=jax.ShapeDtypeStruct(mix.shape, mix.dtype), aliases={6: 0},
        scratch=[pltpu.VMEM((R + 2 * _SUBLANES, dm.Bw), _F32)], name="pool")(
            slab, slab, slab, w.astype(_BF16), b.reshape(1, -1), scale.reshape(1, -1), mix)


def _chunk_pos(dm, ci):
    ncc, ncl = dm.Lc // SSD_CHUNK, dm.Ll // SSD_CHUNK
    nct = dm.Tc // SSD_CHUNK
    is_ctx = ci < nct
    pos = jnp.where(is_ctx, ci % ncc, (ci - nct) % ncl)
    nseq = jnp.where(is_ctx, ncc, ncl)
    lat = jnp.clip((ci - nct) // ncl, 0, dm.DB - 1)
    return is_ctx, pos, nseq, lat


def _ssd_dir(dm, ci, rev, x_ref, bc_ref, dt_ref, bias_ref, a_ref, h0_ref, y_ref, s_ref, h_scr):
    Q, N, SH, P = SSD_CHUNK, dm.N, dm.SH, SSD_HEAD_DIM
    hg = SH // SSD_GROUPS
    gw = hg * P
    is_ctx, pos, nseq, _ = _chunk_pos(dm, ci)
    first = pos == (nseq - 1 if rev else 0)
    last = pos == (0 if rev else nseq - 1)

    @pl.when(jnp.logical_and(first, is_ctx))
    def _():
        h_scr[...] = jnp.zeros_like(h_scr)

    @pl.when(jnp.logical_and(first, jnp.logical_not(is_ctx)))
    def _():
        for k in range(dm.Bw // _LANES):
            h_scr[:, k * _LANES:(k + 1) * _LANES] = h0_ref[k * _LANES:(k + 1) * _LANES, :].T

    lane0 = (SH if rev else 0)
    dt = _softplus(dt_ref[...] + bias_ref[...])
    a = dt * a_ref[...]
    ri = lax.broadcasted_iota(jnp.int32, (Q, Q), 0)
    cj = lax.broadcasted_iota(jnp.int32, (Q, Q), 1)
    tri = (ri <= cj) if rev else (ri >= cj)
    cum = jnp.dot(tri.astype(_F32), a, preferred_element_type=_F32, precision=lax.Precision.HIGHEST)
    cum_t, dt_t = cum.T, dt.T
    ecum = jnp.exp(cum)
    end = 0 if rev else Q - 1
    wend = jnp.exp(cum[end:end + 1, :] - cum) * dt
    lane = lax.broadcasted_iota(jnp.int32, (Q, _LANES), 1)
    low = lane < P

    for g in range(SSD_GROUPS):
        b_g = bc_ref[:, g * N:(g + 1) * N]
        c_g = bc_ref[:, dm.GN + g * N:dm.GN + (g + 1) * N]
        cb = lax.dot_general(c_g, b_g, (((1,), (1,)), ((), ())), preferred_element_type=_F32)
        gs = slice(g * gw, (g + 1) * gw)
        y_diag, e_exp, w_exp = [], [], []
        for p in range(hg // 2):
            xs = slice(g * gw + p * _LANES, g * gw + (p + 1) * _LANES)
            xp = x_ref[:, xs].astype(_BF16)
            ys = []
            for s in range(2):
                ln = lane0 + g * hg + 2 * p + s
                seg = cum[:, ln:ln + 1] - cum_t[ln:ln + 1, :]
                m = cb * jnp.exp(jnp.where(tri, seg, _NEG)) * dt_t[ln:ln + 1, :]
                ys.append(jnp.dot(m.astype(_BF16), xp, preferred_element_type=_F32))
            ln = lane0 + g * hg + 2 * p
            y_diag.append(jnp.where(low, ys[0], ys[1]))
            e_exp.append(jnp.where(low, ecum[:, ln:ln + 1], ecum[:, ln + 1:ln + 2]))
            w_exp.append(jnp.where(low, wend[:, ln:ln + 1], wend[:, ln + 1:ln + 2]))
        y_diag = jnp.concatenate(y_diag, axis=1)
        e_exp = jnp.concatenate(e_exp, axis=1)
        w_exp = jnp.concatenate(w_exp, axis=1)
        h_in = h_scr[:, gs]
        y_off = jnp.dot(c_g, h_in.astype(_BF16), preferred_element_type=_F32) * e_exp
        y_ref[:, gs] = y_diag + y_off
        xw = (x_ref[:, gs] * w_exp).astype(_BF16)
        st = lax.dot_general(b_g, xw, (((0,), (0,)), ((), ())), preferred_element_type=_F32)
        h_scr[:, gs] = h_in * e_exp[end:end + 1, :] + st

    @pl.when(jnp.logical_and(last, is_ctx))
    def _():
        for k in range(dm.Bw // _LANES):
            s_ref[k * _LANES:(k + 1) * _LANES, :] = h_scr[:, k * _LANES:(k + 1) * _LANES].T


def _ssd_body(xf_ref, bcf_ref, dtf_ref, xb_ref, bcb_ref, dtb_ref, bias_ref, a_ref, h0f_ref, h0b_ref,
              yf_ref, yb_ref, sf_ref, sb_ref, hf_scr, hb_scr, *, dm):
    i = pl.program_id(0)
    n = pl.num_programs(0)
    _ssd_dir(dm, i, False, xf_ref, bcf_ref, dtf_ref, bias_ref, a_ref, h0f_ref, yf_ref, sf_ref, hf_scr)
    _ssd_dir(dm, n - 1 - i, True, xb_ref, bcb_ref, dtb_ref, bias_ref, a_ref, h0b_ref, yb_ref, sb_ref, hb_scr)


def _ssd(dm, slab, xs, bc, dt_bias, a_neg, state_ssd5, l):
    Q = SSD_CHUNK
    nchunks = dm.T // Q
    ncc = dm.Lc // Q
    HP = dm.SH * SSD_HEAD_DIM
    dcol = dm.s_dt // _LANES
    fwd = lambda i: i
    bwd = lambda i: nchunks - 1 - i

    def seq_c(ci):
        return jnp.minimum(ci // ncc, dm.B - 1)

    def lat_of(ci):
        return _chunk_pos(dm, ci)[3]

    def specs(order):
        return [pl.BlockSpec((Q, dm.Bw), lambda i: (order(i), 0)),
                pl.BlockSpec((Q, 2 * dm.GN), lambda i: (order(i), 0)),
                pl.BlockSpec((Q, _LANES), lambda i: (order(i), dcol))]

    row = pl.BlockSpec((1, _LANES), lambda i: (0, 0))
    h0 = lambda order, d: pl.BlockSpec((None, None, None, HP, dm.N), lambda i: (lat_of(order(i)), l, d, 0, 0))
    st = lambda order: pl.BlockSpec((None, HP, dm.N), lambda i: (seq_c(order(i)), 0, 0))
    ysd = jax.ShapeDtypeStruct((dm.T, dm.Bw), _F32)
    ssd = jax.ShapeDtypeStruct((dm.B, HP, dm.N), _F32)
    return _call(
        functools.partial(_ssd_body, dm=dm), grid=(nchunks,),
        in_specs=specs(fwd) + specs(bwd) + [row, row, h0(fwd, 0), h0(bwd, 1)],
        out_specs=[pl.BlockSpec((Q, dm.Bw), lambda i: (fwd(i), 0)), pl.BlockSpec((Q, dm.Bw), lambda i: (bwd(i), 0)),
                   st(fwd), st(bwd)],
        out_shape=[ysd, ysd, ssd, ssd],
        scratch=[pltpu.VMEM((dm.N, dm.Bw), _F32), pltpu.VMEM((dm.N, dm.Bw), _F32)], name="ssd")(
            xs, bc, slab, xs, bc, slab, dt_bias, a_neg, state_ssd5, state_ssd5)


def _lru_gates(dm, d, x_ref, wr_ref, wi_ref, br_ref, bi_ref, lam_ref, a_scr, u_scr):
    sp = _softplus(-lam_ref[d:d + 1, :])
    for h in range(dm.LH):
        hs = slice(h * _LANES, (h + 1) * _LANES)
        x = x_ref[:, hs]
        xb = x.astype(_BF16)
        r = jax.nn.sigmoid(jnp.dot(xb, wr_ref[d, h], preferred_element_type=_F32) + br_ref[d:d + 1, hs])
        ig = jax.nn.sigmoid(jnp.dot(xb, wi_ref[d, h], preferred_element_type=_F32) + bi_ref[d:d + 1, hs])
        log_a = -LRU_C * r * sp[:, hs]
        a = jnp.exp(log_a)
        a_scr[:, hs] = a
        u_scr[:, hs] = jnp.sqrt(-jnp.tanh(log_a) * (a * a + 1.0)) * (ig * x)


def _lru_body(xf_ref, xb_ref, wr_ref, wi_ref, br_ref, bi_ref, lam_ref, h0f_ref, h0b_ref,
              hf_ref, hb_ref, finf_ref, finb_ref, af, uf, ab, ub, carry, *, dm):
    R = _ROW_BLOCK
    i = pl.program_id(0)
    n = pl.num_programs(0)
    _lru_gates(dm, 0, xf_ref, wr_ref, wi_ref, br_ref, bi_ref, lam_ref, af, uf)
    _lru_gates(dm, 1, xb_ref, wr_ref, wi_ref, br_ref, bi_ref, lam_ref, ab, ub)

    posf, Lf = dm.seq_pos(i * R)
    posb, Lb = dm.seq_pos((n - 1 - i) * R)
    ctx_f = i * R < dm.Tc
    ctx_b = (n - 1 - i) * R < dm.Tc
    for d, start, is_ctx, h0_ref in ((0, posf == 0, ctx_f, h0f_ref), (1, posb + R == Lb, ctx_b, h0b_ref)):
        @pl.when(jnp.logical_and(start, is_ctx))
        def _():
            carry[d:d + 1, :] = jnp.zeros((1, dm.Bw), _F32)

        @pl.when(jnp.logical_and(start, jnp.logical_not(is_ctx)))
        def _():
            carry[d:d + 1, :] = h0_ref[d:d + 1, :]

    def tile(t, hs):
        hf, hb = hs
        base_f = pl.multiple_of(t * _SUBLANES, _SUBLANES)
        base_b = pl.multiple_of(R - _SUBLANES - t * _SUBLANES, _SUBLANES)
        for r in range(_SUBLANES):
            rf = base_f + r
            hf = af[pl.ds(rf, 1), :] * hf + uf[pl.ds(rf, 1), :]
            hf_ref[pl.ds(rf, 1), :] = hf
            rb = base_b + (_SUBLANES - 1 - r)
            hb = ab[pl.ds(rb, 1), :] * hb + ub[pl.ds(rb, 1), :]
            hb_ref[pl.ds(rb, 1), :] = hb
        return hf, hb

    hf, hb = lax.fori_loop(0, R // _SUBLANES, tile, (carry[0:1, :], carry[1:2, :]))
    carry[0:1, :] = hf
    carry[1:2, :] = hb

    @pl.when(jnp.logical_and(posf + R == Lf, ctx_f))
    def _():
        finf_ref[...] = hf

    @pl.when(jnp.logical_and(posb == 0, ctx_b))
    def _():
        finb_ref[...] = hb


def _lru(dm, xf, w_r, w_i, b_r, b_i, lam, state_lru, l):
    R = _ROW_BLOCK
    nblk = dm.T // R
    fwd = lambda i: i
    bwd = lambda i: nblk - 1 - i
    wspec = pl.BlockSpec((2, dm.LH, _LANES, _LANES), lambda i: (0, 0, 0, 0))
    vspec = pl.BlockSpec((2, dm.Bw), lambda i: (0, 0))
    lat_seq = lambda blk: jnp.clip((blk * R - dm.Tc) // dm.Ll, 0, dm.DB - 1)
    ctx_seq = lambda blk: jnp.minimum(blk * R // dm.Lc, dm.B - 1)
    h0 = lambda order: pl.BlockSpec((None, None, 2, dm.Bw), lambda i: (lat_seq(order(i)), l, 0, 0))
    fin = lambda order: pl.BlockSpec((None, 1, dm.Bw), lambda i: (ctx_seq(order(i)), 0, 0))
    hsd = jax.ShapeDtypeStruct((dm.T, dm.Bw), _F32)
    fsd = jax.ShapeDtypeStruct((dm.B, 1, dm.Bw), _F32)
    return _call(
        functools.partial(_lru_body, dm=dm), grid=(nblk,),
        in_specs=[pl.BlockSpec((R, dm.Bw), lambda i: (fwd(i), 0)), pl.BlockSpec((R, dm.Bw), lambda i: (bwd(i), 0)),
                  wspec, wspec, vspec, vspec, vspec, h0(fwd), h0(bwd)],
        out_specs=[pl.BlockSpec((R, dm.Bw), lambda i: (fwd(i), 0)), pl.BlockSpec((R, dm.Bw), lambda i: (bwd(i), 0)),
                   fin(fwd), fin(bwd)],
        out_shape=[hsd, hsd, fsd, fsd],
        scratch=[pltpu.VMEM((R, dm.Bw), _F32)] * 4 + [pltpu.VMEM((2, dm.Bw), _F32)], name="lru")(
            xf, xf, w_r, w_i, b_r, b_i, lam, state_lru, state_lru)


def _finish_body(yf_ref, yb_ref, xs_ref, z_ref, d_ref, ng_ref, hf_ref, hb_ref, gr_ref, _mix_in, o_ref):
    y = (yf_ref[...] + yb_ref[...] + d_ref[...] * xs_ref[...]) * _silu(z_ref[...])
    gw = y.shape[1] // SSD_GROUPS
    for g in range(SSD_GROUPS):
        gs = slice(g * gw, (g + 1) * gw)
        yg = y[:, gs]
        yg = yg * lax.rsqrt(jnp.mean(yg * yg, axis=-1, keepdims=True) + EPS)
        o_ref[:, gs] = (yg * ng_ref[:, gs]).astype(o_ref.dtype)
    bw = y.shape[1]
    o_ref[:, bw:] = ((hf_ref[...] + hb_ref[...]) * jax.nn.gelu(gr_ref[...], approximate=True)).astype(o_ref.dtype)


def _finish(dm, slab, yf, yb, xs, d_exp, norm_g, hf, hb, mix):
    R = _ROW_BLOCK
    blk = pl.BlockSpec((R, dm.Bw), lambda i: (i, 0))
    row = pl.BlockSpec((1, dm.Bw), lambda i: (0, 0))
    return _call(
        _finish_body, grid=(dm.T // R,),
        in_specs=[blk, blk, blk, pl.BlockSpec((R, dm.Bw), lambda i: (i, dm.s_z // dm.Bw)), row, row,
                  blk, blk, pl.BlockSpec((R, dm.Bw), lambda i: (i, dm.s_gr // dm.Bw)),
                  pl.BlockSpec(memory_space=pl.ANY)],
        out_specs=pl.BlockSpec((R, 2 * dm.Bw), lambda i: (i, _MIX_SSD_LRU)),
        out_shape=jax.ShapeDtypeStruct(mix.shape, mix.dtype), aliases={9: 0}, name="finish")(
            yf, yb, xs, slab, d_exp, norm_g.reshape(1, -1), hf, hb, slab, mix)


def _layer_norm(r, g, b):
    mu = jnp.mean(r, axis=-1, keepdims=True)
    c = r - mu
    var = jnp.mean(c * c, axis=-1, keepdims=True)
    return c * lax.rsqrt(var + EPS) * g + b


def _split_bf16(x):
    hi = x.astype(_BF16)
    return hi, (x - hi.astype(_F32)).astype(_BF16)


_HI16 = 0xFFFF0000


def _pack_bf16_pair(a, b):
    ua = pltpu.bitcast(a.astype(_BF16).astype(_F32), jnp.uint32)
    ub = pltpu.bitcast(b.astype(_BF16).astype(_F32), jnp.uint32)
    return (ua >> 16) | (ub & jnp.uint32(_HI16))


def _unpack_bf16_pair(p):
    return pltpu.bitcast(p << 16, _F32), pltpu.bitcast(p & jnp.uint32(_HI16), _F32)


def _norm_router_body(mix_ref, xc_ref, xl_ref, g1_ref, lg_ref, lb_ref, sc_ref, sh_ref, wr_ref, x1_ref, h_ref, s_ref,
                      *, alpha, ctx_tiles):
    x = jnp.where(pl.program_id(0) < ctx_tiles, xc_ref[...], xl_ref[...])
    x1 = _layer_norm(alpha * x + g1_ref[...] * mix_ref[...], lg_ref[...], lb_ref[...])
    x1_ref[...] = x1
    h = x1 * (1.0 + sc_ref[...]) + sh_ref[...]
    half = h.shape[1] // 2
    h_ref[...] = _pack_bf16_pair(h[:, :half], h[:, half:])
    hh, hl = _split_bf16(h)
    wh, wl = _split_bf16(wr_ref[...])
    logits = (jnp.dot(hh, wh, preferred_element_type=_F32) + jnp.dot(hh, wl, preferred_element_type=_F32)
              + jnp.dot(hl, wh, preferred_element_type=_F32))
    s_ref[...] = jax.nn.sigmoid(logits)


def _stream_specs(dm, tm, width):
    ctx_tiles = dm.Tc // tm
    return [pl.BlockSpec((tm, width), lambda i, *_: (jnp.minimum(i, ctx_tiles - 1), 0)),
            pl.BlockSpec((tm, width), lambda i, *_: (jnp.maximum(i - ctx_tiles, 0), 0))]


def _norm_router(dm, mix, xc, xl, mod5, ln_g, ln_b, w_router_pad, l, alpha):
    tm = 256
    D = dm.D
    tile = pl.BlockSpec((tm, D), lambda i: (i, 0))
    row = pl.BlockSpec((1, D), lambda i: (0, 0))
    return _call(
        functools.partial(_norm_router_body, alpha=alpha, ctx_tiles=dm.Tc // tm), grid=(dm.T // tm,),
        in_specs=[tile] + _stream_specs(dm, tm, D) + [
            _mod_spec(dm, l, 2, tm), row, row, _mod_spec(dm, l, 4, tm), _mod_spec(dm, l, 3, tm),
            pl.BlockSpec((D, _LANES), lambda i: (0, 0))],
        out_specs=[tile, pl.BlockSpec((tm, D // 2), lambda i: (i, 0)), pl.BlockSpec((tm, _LANES), lambda i: (i, 0))],
        out_shape=[jax.ShapeDtypeStruct((dm.T, D), _F32), jax.ShapeDtypeStruct((dm.T, D // 2), jnp.uint32),
                   jax.ShapeDtypeStruct((dm.T, _LANES), _F32)], name="norm_router")(
            mix, xc, xl, mod5, ln_g.reshape(1, D), ln_b.reshape(1, D), mod5, mod5, w_router_pad)


def _top2(v):
    n = v.shape[-1]
    i1 = jnp.argmax(v, axis=-1)
    m1 = jnp.max(v, axis=-1)
    rest = jnp.where(jnp.arange(n) == i1[..., None], -jnp.inf, v)
    return (m1, jnp.max(rest, axis=-1)), (i1, jnp.argmax(rest, axis=-1))


def _route(s, router_bias, tm):
    T, E = s.shape
    per = E // N_EXPERT_GROUPS
    assert TOP_K == 2
    sel = s + router_bias.astype(_F32)
    grp_score = sum(_top2(sel.reshape(T, N_EXPERT_GROUPS, per))[0])
    grp = jnp.argmax(grp_score, axis=-1)
    in_grp = (jnp.arange(E) // per)[None, :] == grp[:, None]
    idx = jnp.stack(_top2(jnp.where(in_grp, sel, -jnp.inf))[1], axis=-1)
    wts = jnp.take_along_axis(s, idx, axis=-1)
    wts = wts / jnp.sum(wts, -1, keepdims=True)
    A = T * TOP_K
    onehot = (idx.reshape(A, 1) == jnp.arange(E, dtype=idx.dtype)[None, :]).astype(jnp.int32)
    csum = jnp.cumsum(onehot, axis=0)
    counts = csum[-1]
    padded = (counts + tm - 1) // tm * tm
    pend = jnp.cumsum(padded)
    pstart = pend - padded
    pos = jnp.sum(onehot * (csum - 1 + pstart[None, :]), axis=1).astype(jnp.int32)
    n_blocks = -(-A // tm) + E
    src_tok = jnp.zeros((n_blocks * tm,), jnp.int32).at[pos].set(jnp.arange(A, dtype=jnp.int32) // TOP_K)
    blk0 = jnp.arange(n_blocks, dtype=jnp.int32) * tm
    block_e = jnp.minimum(jnp.sum((blk0[:, None] >= pend[None, :]).astype(jnp.int32), axis=1), E - 1)
    n_used = (pend[-1] // tm).astype(jnp.int32).reshape(1)
    n_valid = jnp.clip(counts[block_e] - (blk0 - pstart[block_e]), 0, tm)
    n_valid = jnp.where(blk0 < pend[-1], n_valid, 0).astype(jnp.int32)
    return src_tok, block_e.astype(jnp.int32), n_used, n_valid, pos, wts


def _row_copy(src_hbm, row, dst, dst_row, sem):
    return pltpu.make_async_copy(src_hbm.at[pl.ds(row, 1)], dst.at[pl.ds(dst_row, 1)], sem)


def _gather_body(src_ref, nv_ref, h_hbm, o_ref, buf, sem):
    tm = o_ref.shape[0]
    i = pl.program_id(0)
    n = pl.num_programs(0)

    def issue(step, slot):
        def body(r, c):
            _row_copy(h_hbm, src_ref[step * tm + r], buf.at[slot], r, sem.at[slot]).start()
            return c
        lax.fori_loop(0, nv_ref[step], body, 0)

    @pl.when(i == 0)
    def _():
        buf[...] = jnp.zeros_like(buf)
        issue(0, 0)

    @pl.when(i + 1 < n)
    def _():
        issue(i + 1, (i + 1) % 2)

    slot = i % 2

    def wait(r, c):
        _row_copy(h_hbm, 0, buf.at[slot], r, sem.at[slot]).wait()
        return c
    lax.fori_loop(0, nv_ref[i], wait, 0)
    half = buf.shape[2]
    lo, hi = _unpack_bf16_pair(buf[slot])
    o_ref[:, :half] = lo.astype(o_ref.dtype)
    o_ref[:, half:] = hi.astype(o_ref.dtype)


def _gather(h_packed, src_tok, n_valid, tm):
    half = h_packed.shape[1]
    n_blocks = src_tok.shape[0] // tm
    return _call(
        _gather_body, grid=(n_blocks,), nprefetch=2,
        in_specs=[pl.BlockSpec(memory_space=pl.ANY)],
        out_specs=pl.BlockSpec((tm, 2 * half), lambda i, src, nv: (i, 0)),
        out_shape=jax.ShapeDtypeStruct((n_blocks * tm, 2 * half), _BF16),
        scratch=[pltpu.VMEM((2, tm, half), jnp.uint32), pltpu.SemaphoreType.DMA((2,))], name="moe_gather")(
            src_tok, n_valid, h_packed)


def _expert_changed(be_ref, m):
    return jnp.logical_or(m == 0, be_ref[m] != be_ref[jnp.maximum(m - 1, 0)])


def _gateup_body(be_ref, nu_ref, x_ref, wg_ref, wu_ref, g_ref, wg_s, wu_s):
    m = pl.program_id(1)

    @pl.when(_expert_changed(be_ref, m))
    def _():
        wg_s[...] = wg_ref[...].astype(_BF16)
        wu_s[...] = wu_ref[...].astype(_BF16)

    @pl.when(m < nu_ref[0])
    def _():
        x = x_ref[...]
        gate = jnp.dot(x, wg_s[...], preferred_element_type=_F32)
        up = jnp.dot(x, wu_s[...], preferred_element_type=_F32)
        g_ref[...] = (_silu(gate) * up).astype(g_ref.dtype)

    @pl.when(m >= nu_ref[0])
    def _():
        g_ref[...] = jnp.zeros_like(g_ref)


def _gateup(xs, block_e, n_used, w_gate, w_up, l, tm):
    A_pad, D = xs.shape
    F = w_gate.shape[-1]
    tf = _pick_tile(F, (512, 256, 128))
    wspec = pl.BlockSpec((None, None, D, tf), lambda j, m, be, nu: (l, be[m], 0, j))
    return _call(
        _gateup_body, grid=(F // tf, A_pad // tm), nprefetch=2,
        in_specs=[pl.BlockSpec((tm, D), lambda j, m, be, nu: (m, 0)), wspec, wspec],
        out_specs=pl.BlockSpec((tm, tf), lambda j, m, be, nu: (m, j)),
        out_shape=jax.ShapeDtypeStruct((A_pad, F), _BF16),
        scratch=[pltpu.VMEM((D, tf), _BF16)] * 2, name="moe_gateup")(block_e, n_used, xs, w_gate, w_up)


def _down_body(be_ref, nu_ref, g_ref, wd_ref, y_ref, wd_s):
    m = pl.program_id(1)

    @pl.when(_expert_changed(be_ref, m))
    def _():
        wd_s[...] = wd_ref[...].astype(_BF16)

    @pl.when(m < nu_ref[0])
    def _():
        y = jnp.dot(g_ref[...], wd_s[...], preferred_element_type=_F32)
        half = y.shape[1] // 2
        y_ref[...] = _pack_bf16_pair(y[:, :half], y[:, half:])

    @pl.when(m >= nu_ref[0])
    def _():
        y_ref[...] = jnp.zeros_like(y_ref)


def _down_tile(D):
    return _pick_tile(D, (2048, 1024, 512, 256))


def _down(g, block_e, n_used, w_down, l, tm):
    A_pad, F = g.shape
    D = w_down.shape[-1]
    tn = _down_tile(D)
    return _call(
        _down_body, grid=(D // tn, A_pad // tm), nprefetch=2,
        in_specs=[pl.BlockSpec((tm, F), lambda j, m, be, nu: (m, 0)),
                  pl.BlockSpec((None, None, F, tn), lambda j, m, be, nu: (l, be[m], 0, j))],
        out_specs=pl.BlockSpec((tm, tn // 2), lambda j, m, be, nu: (m, j)),
        out_shape=jax.ShapeDtypeStruct((A_pad, D // 2), jnp.uint32),
        scratch=[pltpu.VMEM((F, tn), _BF16)], name="moe_down")(block_e, n_used, g, w_down)


def _combine_body(pos_ref, y_hbm, x1_ref, w_ref, g2_ref, lg_ref, lb_ref, *rest, alpha, ctx_tiles, final, tn):
    if final:
        xp_ref, xs_ref, buf, sem = rest
    else:
        sc_ref, sh_ref, xp_ref, xs_ref, hn_ref, buf, sem = rest
    tm = x1_ref.shape[0]
    i = pl.program_id(0)
    n = pl.num_programs(0)

    def issue(step, slot):
        def body(r, c):
            for k in range(TOP_K):
                _row_copy(y_hbm, pos_ref[(step * tm + r) * TOP_K + k], buf.at[slot, k], r, sem.at[slot]).start()
            return c
        lax.fori_loop(0, tm, body, 0)

    @pl.when(i == 0)
    def _():
        issue(0, 0)

    @pl.when(i + 1 < n)
    def _():
        issue(i + 1, (i + 1) % 2)

    slot = i % 2

    def wait(r, c):
        for k in range(TOP_K):
            _row_copy(y_hbm, 0, buf.at[slot, k], r, sem.at[slot]).wait()
        return c
    lax.fori_loop(0, tm, wait, 0)
    w = w_ref[...]
    ffn = None
    for k in range(TOP_K):
        parts = []
        for j in range(2 * buf.shape[3] // tn):
            parts += _unpack_bf16_pair(buf[slot, k, :, j * (tn // 2):(j + 1) * (tn // 2)])
        term = w[:, k:k + 1] * jnp.concatenate(parts, axis=1)
        ffn = term if ffn is None else ffn + term
    x2 = _layer_norm(alpha * x1_ref[...] + g2_ref[...] * ffn, lg_ref[...], lb_ref[...])
    if not final:
        hn_ref[...] = (x2 * (1.0 + sc_ref[...]) + sh_ref[...]).astype(hn_ref.dtype)

    @pl.when(i < ctx_tiles)
    def _():
        xp_ref[...] = x2

    @pl.when(i >= ctx_tiles)
    def _():
        xs_ref[...] = x2


def _combine(dm, y_packed, pos, wts, x1, mod5, ln_g, ln_b, l, alpha):
    tm = 256
    D = dm.D
    final = l == dm.depth - 1
    tile = pl.BlockSpec((tm, D), lambda i, p: (i, 0))
    row = pl.BlockSpec((1, D), lambda i, p: (0, 0))
    in_specs = [pl.BlockSpec(memory_space=pl.ANY), tile, pl.BlockSpec((tm, TOP_K), lambda i, p: (i, 0)),
                _mod_spec(dm, l, 5, tm), row, row]
    args = [pos, y_packed, x1, wts, mod5, ln_g.reshape(1, D), ln_b.reshape(1, D)]
    out_specs = _stream_specs(dm, tm, D)
    out_shape = [jax.ShapeDtypeStruct((dm.Tc, D), _F32), jax.ShapeDtypeStruct((dm.Tl, D), _F32)]
    if not final:
        in_specs += [_mod_spec(dm, l + 1, 1, tm), _mod_spec(dm, l + 1, 0, tm)]
        args += [mod5, mod5]
        out_specs = out_specs + [tile]
        out_shape = out_shape + [jax.ShapeDtypeStruct((dm.T, D), _BF16)]
    return _call(
        functools.partial(_combine_body, alpha=alpha, ctx_tiles=dm.Tc // tm, final=final, tn=_down_tile(D)),
        grid=(dm.T // tm,), nprefetch=1, in_specs=in_specs, out_specs=out_specs, out_shape=out_shape,
        scratch=[pltpu.VMEM((2, TOP_K, tm, D // 2), jnp.uint32), pltpu.SemaphoreType.DMA((2,))],
        name="moe_combine")(*args)


def kernel(x_prompt, x_sample, cache_k, cache_v, state_ssd, state_lru, c, c_ctx, w_ada, b_ada, w_in, w_out, attn_sink, ssd_conv_w, ssd_conv_b, ssd_a_log, ssd_dt_bias, ssd_d, ssd_norm_g, lru_conv_w, lru_conv_b, lru_lambda, lru_w_r, lru_b_r, lru_w_i, lru_b_i, pool_w, pool_b, pool_scale, ln1_g, ln1_b, ln2_g, ln2_b, w_router, router_bias, moe_w_gate, moe_w_up, moe_w_down):
    dm = _Dims(x_prompt, x_sample, cache_k, state_ssd, lru_w_r, w_router, moe_w_gate, w_ada)
    depth, D, Bw, KVw, GN, SH = dm.depth, dm.D, dm.Bw, dm.KVw, dm.GN, dm.SH
    alpha = (2 * depth) ** 0.25

    rows = -(-(1 + dm.DB) // _SUBLANES) * _SUBLANES
    cond = jnp.zeros((rows, D), _F32).at[0].set(c_ctx).at[1:1 + dm.DB].set(c)
    mod5 = _ada(cond, w_ada, b_ada).reshape(depth, rows, 6, 1, D)

    xc, xl = x_prompt.reshape(dm.Tc, D), x_sample.reshape(dm.Tl, D)
    h = _modulate(dm, xc, xl, mod5, 0)

    cos, sin = _rope_tables(dm.Ll)
    cache_k4 = cache_k.reshape(dm.DB, depth, dm.past, KVw)
    cache_v4 = cache_v.reshape(dm.DB, depth, dm.past, KVw)
    state_ssd5 = state_ssd.reshape(dm.DB, depth, 2, SH * SSD_HEAD_DIM, dm.N)
    w_router_pad = jnp.zeros((D, _LANES), _F32).at[:, :dm.E].set(w_router)

    o_k, o_v, o_z = Bw, Bw + KVw, Bw + 2 * KVw
    o_xs = o_z + Bw
    o_B = o_xs + Bw
    o_dt = o_B + 2 * GN
    o_xr = o_dt + 2 * SH
    o_gr, o_xp = o_xr + Bw, o_xr + 2 * Bw

    ks, vs, ssds, lrus = [], [], [], []
    for l in range(depth):
        wi = w_in[l]
        cols = lambda o, w: wi[:, o:o + w]
        w_slab = jnp.concatenate(
            [cols(0, Bw), cols(o_z, Bw), cols(o_xs, Bw), cols(o_xr, Bw), cols(o_gr, Bw), cols(o_xp, Bw),
             cols(o_k, KVw), cols(o_v, KVw), cols(o_B, 2 * GN), cols(o_dt, 2 * SH),
             jnp.zeros((D, dm.slab_w - dm.s_dt - 2 * SH), wi.dtype)], axis=1).astype(_BF16)
        slab = _matmul(h, w_slab, "in_proj")
        mix_in = jnp.zeros((dm.T, D), _BF16)

        mix_in = _attn_ctx(dm, slab, attn_sink[l], mix_in)
        qr, kr = _rope(dm, slab, cos, sin)
        mix_in = _attn_lat(dm, slab, qr, kr, cache_k4, cache_v4, attn_sink[l], l, mix_in)

        cw = ssd_conv_w[l]
        xs = _conv(dm, slab, dm.s_xs, Bw, cw[:, :Bw], ssd_conv_b[l][:Bw], True, _F32)
        bc = _conv(dm, slab, dm.s_B, 2 * GN, cw[:, Bw:], ssd_conv_b[l][Bw:], True, _BF16)
        pad = jnp.zeros((_LANES - 2 * SH,), _F32)
        dt_bias = jnp.concatenate([ssd_dt_bias[l].reshape(-1), pad]).reshape(1, _LANES)
        a_neg = jnp.concatenate([-jnp.exp(ssd_a_log[l].reshape(-1)), pad]).reshape(1, _LANES)
        yf, yb, s_f, s_b = _ssd(dm, slab, xs, bc, dt_bias, a_neg, state_ssd5, l)

        xf = _conv(dm, slab, dm.s_xr, Bw, lru_conv_w[l], lru_conv_b[l], False, _F32)
        hf, hb, fin_f, fin_b = _lru(dm, xf, lru_w_r[l].astype(_BF16), lru_w_i[l].astype(_BF16),
                                    lru_b_r[l], lru_b_i[l], lru_lambda[l], state_lru, l)

        d_exp = jnp.repeat(ssd_d[l], SSD_HEAD_DIM).reshape(1, Bw)
        mix_in = _finish(dm, slab, yf, yb, xs, d_exp, ssd_norm_g[l], hf, hb, mix_in)
        mix_in = _pool(dm, slab, pool_w[l], pool_b[l], pool_scale[l], mix_in)

        wo = w_out[l]
        w_mix = jnp.concatenate([wo[Bw:3 * Bw], wo[:Bw], wo[3 * Bw:]], axis=0).astype(_BF16)
        mix = _matmul(mix_in, w_mix, "out_proj")
        x1, h2, s_pad = _norm_router(dm, mix, xc, xl, mod5, ln1_g[l], ln1_b[l], w_router_pad, l, alpha)

        src_tok, block_e, n_used, n_valid, pos, wts = _route(s_pad[:, :dm.E], router_bias, _MOE_TM)
        xs_sorted = _gather(h2, src_tok, n_valid, _MOE_TM)
        g = _gateup(xs_sorted, block_e, n_used, moe_w_gate, moe_w_up, l, _MOE_TM)
        y = _down(g, block_e, n_used, moe_w_down, l, _MOE_TM)
        outs = _combine(dm, y, pos, wts, x1, mod5, ln2_g[l], ln2_b[l], l, alpha)
        xc, xl = outs[0], outs[1]
        if l + 1 < depth:
            h = outs[2]

        ks.append(slab[:dm.Tc, dm.s_k:dm.s_k + KVw].reshape(dm.B, dm.Lc, dm.KV, ATT_HEAD_DIM))
        vs.append(slab[:dm.Tc, dm.s_v:dm.s_v + KVw].reshape(dm.B, dm.Lc, dm.KV, ATT_HEAD_DIM))
        ssds.append(jnp.stack([s_f, s_b], axis=1).reshape(dm.B, 2, SH, SSD_HEAD_DIM, dm.N))
        lrus.append(jnp.concatenate([fin_f, fin_b], axis=1))

    y_prompt = xc.reshape(dm.B, dm.Lc, D)
    y_sample = xl.reshape(dm.DB, dm.Ll, D)
    return (y_prompt, y_sample, jnp.stack(ks, axis=1), jnp.stack(vs, axis=1),
            jnp.stack(ssds, axis=1), jnp.stack(lrus, axis=1))
```

```python
import functools
import math

import jax
import jax.numpy as jnp
import numpy as np
from jax import lax
from jax.experimental import pallas as pl
from jax.experimental.pallas import tpu as pltpu

_F32 = jnp.float32
_BF16 = jnp.bfloat16

GRID_W = 64
ATT_HEAD_DIM = 128
ATT_BLOCK = 128
ROPE_BASE = 10000.0
SSD_HEAD_DIM = 64
SSD_GROUPS = 2
SSD_CHUNK = 128
LRU_C = 8.0
POOL_WINDOWS = (2, 4, 8, 16)
N_EXPERT_GROUPS = 4
TOP_K = 2
EPS = 1e-5

_LANES = 128
_SUBLANES = 8
_VMEM_BYTES_V7X = 64 * 1024 * 1024
_VMEM_LIMIT = _VMEM_BYTES_V7X * 7 // 8

_ROW_BLOCK = 256
_MOE_TM = 512
_NEG = -1e30
_MIX_SSD_LRU, _MIX_ATT, _MIX_POOL = 0, 2, 3


def _call(body, *, grid, in_specs, out_specs, out_shape, scratch=(), nprefetch=0, aliases=None, name=None):
    grid_spec = pltpu.PrefetchScalarGridSpec(
        num_scalar_prefetch=nprefetch, grid=grid, in_specs=in_specs, out_specs=out_specs,
        scratch_shapes=list(scratch))
    return pl.pallas_call(
        body, grid_spec=grid_spec, out_shape=out_shape,
        compiler_params=pltpu.CompilerParams(
            dimension_semantics=("arbitrary",) * len(grid), vmem_limit_bytes=_VMEM_LIMIT),
        input_output_aliases=aliases or {}, name=name)


def _softplus(x):
    return jnp.maximum(x, 0.0) + jnp.log1p(jnp.exp(-jnp.abs(x)))


def _silu(x):
    return x * jax.nn.sigmoid(x)


class _Dims:
    def __init__(self, x_prompt, x_sample, cache_k, state_ssd, lru_w_r, w_router, moe_w_gate, w_ada):
        self.B, self.Lc, self.D = x_prompt.shape
        self.DB, self.Ll, _ = x_sample.shape
        self.depth = w_ada.shape[0]
        self.Tc, self.Tl = self.B * self.Lc, self.DB * self.Ll
        self.T = self.Tc + self.Tl
        self.Bw = self.D // 4
        self.past, self.KV = cache_k.shape[2], cache_k.shape[3]
        self.H = self.Bw // ATT_HEAD_DIM
        self.G = self.H // self.KV
        self.KVw = self.KV * ATT_HEAD_DIM
        self.SH = self.Bw // SSD_HEAD_DIM
        self.N = state_ssd.shape[-1]
        self.GN = SSD_GROUPS * self.N
        self.LH = lru_w_r.shape[2]
        self.E = w_router.shape[1]
        self.F = moe_w_gate.shape[-1]
        Bw, KVw, GN = self.Bw, self.KVw, self.GN
        self.s_q, self.s_z, self.s_xs, self.s_xr, self.s_gr, self.s_xp = (i * Bw for i in range(6))
        self.s_k = 6 * Bw
        self.s_v = self.s_k + KVw
        self.s_B = self.s_v + KVw
        self.s_C = self.s_B + GN
        self.s_dt = self.s_C + GN
        used = self.s_dt + _LANES
        self.slab_w = -(-used // 512) * 512
        assert 2 * self.SH <= _LANES and self.N == _LANES and ATT_HEAD_DIM == _LANES
        assert self.s_k % KVw == 0 and self.s_B % (2 * GN) == 0 and self.Bw % (2 * _LANES) == 0
        assert self.Lc % _ROW_BLOCK == 0 and self.Ll % _ROW_BLOCK == 0
        assert self.Lc % SSD_CHUNK == 0 and self.Ll % SSD_CHUNK == 0 and self.Ll % ATT_BLOCK == 0
        assert self.Bw // self.LH == _LANES and self.Bw % len(POOL_WINDOWS) == 0

    def seg_of_row(self, r0):
        return jnp.where(r0 < self.Tc, 0, 1 + (r0 - self.Tc) // self.Ll)

    def seq_pos(self, r0):
        is_ctx = r0 < self.Tc
        pos = jnp.where(is_ctx, r0 % self.Lc, (r0 - self.Tc) % self.Ll)
        return pos, jnp.where(is_ctx, self.Lc, self.Ll)


def _ada_body(c_ref, w_ref, b_ref, o_ref):
    s = _silu(c_ref[...]).astype(_BF16)
    o_ref[...] = jnp.dot(s, w_ref[...].astype(_BF16), preferred_element_type=_F32) + b_ref[...]


def _ada(cond, w_ada, b_ada):
    depth, D, n6 = w_ada.shape
    rows = cond.shape[0]
    tn = 512
    return _call(
        _ada_body, grid=(depth, n6 // tn),
        in_specs=[pl.BlockSpec((rows, D), lambda l, j: (0, 0)),
                  pl.BlockSpec((None, D, tn), lambda l, j: (l, 0, j)),
                  pl.BlockSpec((None, 1, tn), lambda l, j: (l, 0, j))],
        out_specs=pl.BlockSpec((None, rows, tn), lambda l, j: (l, 0, j)),
        out_shape=jax.ShapeDtypeStruct((depth, rows, n6), _F32), name="ada")(
            cond, w_ada, b_ada.reshape(depth, 1, n6))


def _mod_spec(dm, l, k, tm):
    return pl.BlockSpec((None, None, None, 1, dm.D), lambda i, *_: (l, dm.seg_of_row(i * tm), k, 0, 0))


def _modulate_body(xc_ref, xl_ref, sh_ref, sc_ref, o_ref, *, ctx_tiles):
    x = jnp.where(pl.program_id(0) < ctx_tiles, xc_ref[...], xl_ref[...])
    o_ref[...] = (x * (1.0 + sc_ref[...]) + sh_ref[...]).astype(o_ref.dtype)


def _modulate(dm, xc, xl, mod5, l):
    tm = 512
    return _call(
        functools.partial(_modulate_body, ctx_tiles=dm.Tc // tm), grid=(dm.T // tm,),
        in_specs=_stream_specs(dm, tm, dm.D) + [_mod_spec(dm, l, 0, tm), _mod_spec(dm, l, 1, tm)],
        out_specs=pl.BlockSpec((tm, dm.D), lambda i: (i, 0)),
        out_shape=jax.ShapeDtypeStruct((dm.T, dm.D), _BF16), name="modulate")(xc, xl, mod5, mod5)


def _mm_body(x_ref, w_ref, o_ref):
    o_ref[...] = jnp.dot(x_ref[...], w_ref[...], preferred_element_type=_F32).astype(o_ref.dtype)


def _pick_tile(n, candidates):
    for c in candidates:
        if n % c == 0:
            return c
    raise ValueError(f"no tile for {n}")


def _matmul(x, w, name, out_dtype=_F32):
    M, K = x.shape
    N = w.shape[1]
    tm = _pick_tile(M, (512, 256, 128))
    tn = _pick_tile(N, (1536, 1280, 1024, 768, 512))
    return _call(
        _mm_body, grid=(N // tn, M // tm),
        in_specs=[pl.BlockSpec((tm, K), lambda j, i: (i, 0)), pl.BlockSpec((K, tn), lambda j, i: (0, j))],
        out_specs=pl.BlockSpec((tm, tn), lambda j, i: (i, j)),
        out_shape=jax.ShapeDtypeStruct((M, N), out_dtype), name=name)(x, w)


def _rope_body(q_ref, k_ref, cos_ref, sin_ref, qo_ref, ko_ref):
    cos, sin = cos_ref[...], sin_ref[...]
    quarter = ATT_HEAD_DIM // 4
    lane = lax.broadcasted_iota(jnp.int32, cos.shape, 1)
    first = (lane % (2 * quarter)) < quarter

    def rot(x):
        partner = jnp.where(first, pltpu.roll(x, ATT_HEAD_DIM - quarter, 1), pltpu.roll(x, quarter, 1))
        return x * cos + partner * sin

    for ref, out in ((q_ref, qo_ref), (k_ref, ko_ref)):
        for h in range(ref.shape[1] // ATT_HEAD_DIM):
            sl = slice(h * ATT_HEAD_DIM, (h + 1) * ATT_HEAD_DIM)
            out[:, sl] = rot(ref[:, sl]).astype(out.dtype)


def _rope_tables(L):
    quarter = ATT_HEAD_DIM // 4
    t = jnp.arange(L)
    inv = ROPE_BASE ** (-jnp.arange(quarter, dtype=_F32) / quarter)
    ang_r = (t // GRID_W).astype(_F32)[:, None] * inv[None, :]
    ang_c = (t % GRID_W).astype(_F32)[:, None] * inv[None, :]
    cos = jnp.concatenate([jnp.cos(ang_r)] * 2 + [jnp.cos(ang_c)] * 2, -1)
    sin = jnp.concatenate([-jnp.sin(ang_r), jnp.sin(ang_r), -jnp.sin(ang_c), jnp.sin(ang_c)], -1)
    return cos, sin


def _rope(dm, slab, cos, sin):
    tr = _ROW_BLOCK
    r0, per_seq = dm.Tc // tr, dm.Ll // tr
    return _call(
        _rope_body, grid=(dm.Tl // tr,),
        in_specs=[pl.BlockSpec((tr, dm.Bw), lambda i: (r0 + i, dm.s_q // dm.Bw)),
                  pl.BlockSpec((tr, dm.KVw), lambda i: (r0 + i, dm.s_k // dm.KVw)),
                  pl.BlockSpec((tr, _LANES), lambda i: (i % per_seq, 0)),
                  pl.BlockSpec((tr, _LANES), lambda i: (i % per_seq, 0))],
        out_specs=[pl.BlockSpec((tr, dm.Bw), lambda i: (i, 0)), pl.BlockSpec((tr, dm.KVw), lambda i: (i, 0))],
        out_shape=[jax.ShapeDtypeStruct((dm.Tl, dm.Bw), _BF16), jax.ShapeDtypeStruct((dm.Tl, dm.KVw), _BF16)],
        name="rope")(slab, slab, cos, sin)


def _attend(sink_ref, q, k_all, v_all, bias, o_ref, kv, G):
    Q = q.shape[0]
    q_st = jnp.concatenate([q[:, g * ATT_HEAD_DIM:(g + 1) * ATT_HEAD_DIM] for g in range(G)], axis=0)
    s = lax.dot_general(q_st, k_all, (((1,), (1,)), ((), ())), preferred_element_type=_F32)
    s = s * (ATT_HEAD_DIM ** -0.5)
    if bias is not None:
        s = s + bias
    sk = jnp.concatenate([jnp.full((Q, 1), sink_ref[kv * G + g], _F32) for g in range(G)], axis=0)
    m = jnp.maximum(jnp.max(s, axis=-1, keepdims=True), sk)
    p = jnp.exp(s - m)
    denom = jnp.sum(p, axis=-1, keepdims=True) + jnp.exp(sk - m)
    o = jnp.dot(p.astype(_BF16), v_all, preferred_element_type=_F32) / denom
    for g in range(G):
        h = kv * G + g
        o_ref[:, h * ATT_HEAD_DIM:(h + 1) * ATT_HEAD_DIM] = o[g * Q:(g + 1) * Q].astype(o_ref.dtype)


def _attn_ctx_body(sink_ref, q_ref, k_ref, v_ref, _mix_in, o_ref, *, KV, G):
    for kv in range(KV):
        hs = slice(kv * ATT_HEAD_DIM, (kv + 1) * ATT_HEAD_DIM)
        q = q_ref[:, kv * G * ATT_HEAD_DIM:(kv + 1) * G * ATT_HEAD_DIM].astype(_BF16)
        _attend(sink_ref, q, k_ref[:, hs].astype(_BF16), v_ref[:, hs].astype(_BF16), None, o_ref, kv, G)


def _attn_ctx(dm, slab, sink, mix):
    Lc = dm.Lc
    return _call(
        functools.partial(_attn_ctx_body, KV=dm.KV, G=dm.G), grid=(dm.B,),
        in_specs=[pl.BlockSpec(memory_space=pltpu.SMEM),
                  pl.BlockSpec((Lc, dm.Bw), lambda b: (b, dm.s_q // dm.Bw)),
                  pl.BlockSpec((Lc, dm.KVw), lambda b: (b, dm.s_k // dm.KVw)),
                  pl.BlockSpec((Lc, dm.KVw), lambda b: (b, dm.s_v // dm.KVw)),
                  pl.BlockSpec(memory_space=pl.ANY)],
        out_specs=pl.BlockSpec((Lc, dm.Bw), lambda b: (b, _MIX_ATT)),
        out_shape=jax.ShapeDtypeStruct(mix.shape, mix.dtype), aliases={4: 0}, name="attn_ctx")(
            sink, slab, slab, slab, mix)


def _attn_lat_body(sink_ref, q_ref, kp_ref, kc_ref, kn_ref, vp_ref, vc_ref, vn_ref, ck_ref, cv_ref, _mix_in,
                   o_ref, *, KV, G, nb):
    n = pl.program_id(1)
    Q = ATT_BLOCK
    S = 3 * Q + ck_ref.shape[0]
    rows = G * Q
    c_i = lax.broadcasted_iota(jnp.int32, (rows, S), 1)
    qi = lax.broadcasted_iota(jnp.int32, (rows, S), 0) % Q
    bad_prev = c_i < jnp.maximum(qi, jnp.where(n == 0, Q, 0))
    bad_next = jnp.logical_and(c_i - 2 * Q > qi - jnp.where(n == nb - 1, Q, 0), c_i < 3 * Q)
    bias = jnp.where(jnp.logical_or(bad_prev, bad_next), _NEG, 0.0)
    for kv in range(KV):
        hs = slice(kv * ATT_HEAD_DIM, (kv + 1) * ATT_HEAD_DIM)
        k_all = jnp.concatenate([kp_ref[:, hs], kc_ref[:, hs], kn_ref[:, hs], ck_ref[:, hs].astype(_BF16)], axis=0)
        v_all = jnp.concatenate([vp_ref[:, hs].astype(_BF16), vc_ref[:, hs].astype(_BF16),
                                 vn_ref[:, hs].astype(_BF16), cv_ref[:, hs].astype(_BF16)], axis=0)
        q = q_ref[:, kv * G * ATT_HEAD_DIM:(kv + 1) * G * ATT_HEAD_DIM]
        _attend(sink_ref, q, k_all, v_all, bias, o_ref, kv, G)


def _attn_lat(dm, slab, qr, kr, cache_k4, cache_v4, sink, l, mix):
    Q = ATT_BLOCK
    nb = dm.Ll // Q
    r0 = dm.Tc // Q
    vcol = dm.s_v // dm.KVw
    prev = lambda n: jnp.maximum(n - 1, 0)
    nxt = lambda n: jnp.minimum(n + 1, nb - 1)
    cache_spec = pl.BlockSpec((None, None, dm.past, dm.KVw), lambda b, n: (b, l, 0, 0))
    return _call(
        functools.partial(_attn_lat_body, KV=dm.KV, G=dm.G, nb=nb), grid=(dm.DB, nb),
        in_specs=[pl.BlockSpec(memory_space=pltpu.SMEM),
                  pl.BlockSpec((Q, dm.Bw), lambda b, n: (b * nb + n, 0)),
                  pl.BlockSpec((Q, dm.KVw), lambda b, n: (b * nb + prev(n), 0)),
                  pl.BlockSpec((Q, dm.KVw), lambda b, n: (b * nb + n, 0)),
                  pl.BlockSpec((Q, dm.KVw), lambda b, n: (b * nb + nxt(n), 0)),
                  pl.BlockSpec((Q, dm.KVw), lambda b, n: (r0 + b * nb + prev(n), vcol)),
                  pl.BlockSpec((Q, dm.KVw), lambda b, n: (r0 + b * nb + n, vcol)),
                  pl.BlockSpec((Q, dm.KVw), lambda b, n: (r0 + b * nb + nxt(n), vcol)),
                  cache_spec, cache_spec, pl.BlockSpec(memory_space=pl.ANY)],
        out_specs=pl.BlockSpec((Q, dm.Bw), lambda b, n: (r0 + b * nb + n, _MIX_ATT)),
        out_shape=jax.ShapeDtypeStruct(mix.shape, mix.dtype), aliases={10: 0}, name="attn_lat")(
            sink, qr, kr, kr, kr, slab, slab, slab, cache_k4, cache_v4, mix)


def _halo_specs(dm, cw, colblk):
    R = _ROW_BLOCK
    per = R // _SUBLANES
    last = dm.T // _SUBLANES - 1
    return [pl.BlockSpec((R, cw), lambda i, *_: (i, colblk)),
            pl.BlockSpec((_SUBLANES, cw), lambda i, *_: (jnp.maximum(i * per - 1, 0), colblk)),
            pl.BlockSpec((_SUBLANES, cw), lambda i, *_: (jnp.minimum((i + 1) * per, last), colblk))]


def _fill_padded(dm, pad_ref, x_ref, prev_ref, next_ref):
    R = _ROW_BLOCK
    r0 = pl.program_id(0) * R
    pos, L = dm.seq_pos(r0)
    pad_ref[pl.ds(_SUBLANES, R), :] = x_ref[...].astype(_F32)
    pad_ref[pl.ds(0, _SUBLANES), :] = jnp.where(pos == 0, 0.0, prev_ref[...].astype(_F32))
    pad_ref[pl.ds(_SUBLANES + R, _SUBLANES), :] = jnp.where(pos + R == L, 0.0, next_ref[...].astype(_F32))
    return pos, L


def _conv_body(x_ref, prev_ref, next_ref, w_ref, b_ref, o_ref, pad_ref, *, dm, act):
    _fill_padded(dm, pad_ref, x_ref, prev_ref, next_ref)
    R = _ROW_BLOCK
    width = w_ref.shape[0]
    left = (width - 1) // 2
    out = pad_ref[pl.ds(_SUBLANES - left, R), :] * w_ref[0:1, :]
    for tap in range(1, width):
        out = out + pad_ref[pl.ds(_SUBLANES - left + tap, R), :] * w_ref[tap:tap + 1, :]
    out = out + b_ref[...]
    if act:
        out = _silu(out)
    o_ref[...] = out.astype(o_ref.dtype)


def _conv(dm, slab, col, cw, w, b, act, out_dtype):
    R = _ROW_BLOCK
    width = w.shape[0]
    return _call(
        functools.partial(_conv_body, dm=dm, act=act), grid=(dm.T // R,),
        in_specs=_halo_specs(dm, cw, col // cw) + [pl.BlockSpec((width, cw), lambda i: (0, 0)),
                                                   pl.BlockSpec((1, cw), lambda i: (0, 0))],
        out_specs=pl.BlockSpec((R, cw), lambda i: (i, 0)),
        out_shape=jax.ShapeDtypeStruct((dm.T, cw), out_dtype),
        scratch=[pltpu.VMEM((R + 2 * _SUBLANES, cw), _F32)], name="conv")(slab, slab, slab, w, b.reshape(1, cw))


def _pool_body(x_ref, prev_ref, next_ref, w_ref, b_ref, sc_ref, _mix_in, o_ref, pad_ref, *, dm):
    pos, L = _fill_padded(dm, pad_ref, x_ref, prev_ref, next_ref)
    R = _ROW_BLOCK
    gd = w_ref.shape[1]
    t = pos + lax.broadcasted_iota(jnp.int32, (R, 1), 0)
    for gi, win in enumerate(POOL_WINDOWS):
        cs = slice(gi * gd, (gi + 1) * gd)
        half = win // 2
        total = pad_ref[pl.ds(_SUBLANES - half, R), cs]
        for k in range(1, win):
            total = total + pad_ref[pl.ds(_SUBLANES - half + k, R), cs]
        cnt = (jnp.minimum(t + (win - half), L) - jnp.maximum(t - half, 0)).astype(_F32)
        pooled = total / cnt - pad_ref[pl.ds(_SUBLANES, R), cs]
        y = jnp.dot(pooled.astype(_BF16), w_ref[gi], preferred_element_type=_F32) + b_ref[:, cs]
        o_ref[:, cs] = (y * sc_ref[:, cs]).astype(o_ref.dtype)


def _pool(dm, slab, w, b, scale, mix):
    R = _ROW_BLOCK
    ng, gd, _ = w.shape
    assert max(POOL_WINDOWS) // 2 <= _SUBLANES
    row = pl.BlockSpec((1, dm.Bw), lambda i: (0, 0))
    return _call(
        functools.partial(_pool_body, dm=dm), grid=(dm.T // R,),
        in_specs=_halo_specs(dm, dm.Bw, dm.s_xp // dm.Bw) + [pl.BlockSpec((ng, gd, gd), lambda i: (0, 0, 0)), row, row,
                                                              pl.BlockSpec(memory_space=pl.ANY)],
        out_specs=pl.BlockSpec((R, dm.Bw), lambda i: (i, _MIX_POOL)),
        out_shape=jax.ShapeDtypeStruct(mix.shape, mix.dtype), aliases={6: 0},
        scratch=[pltpu.VMEM((R + 2 * _SUBLANES, dm.Bw), _F32)], name="pool")(
            slab, slab, slab, w.astype(_BF16), b.reshape(1, -1), scale.reshape(1, -1), mix)


def _chunk_pos(dm, ci):
    ncc, ncl = dm.Lc // SSD_CHUNK, dm.Ll // SSD_CHUNK
    nct = dm.Tc // SSD_CHUNK
    is_ctx = ci < nct
    pos = jnp.where(is_ctx, ci % ncc, (ci - nct) % ncl)
    nseq = jnp.where(is_ctx, ncc, ncl)
    lat = jnp.clip((ci - nct) // ncl, 0, dm.DB - 1)
    return is_ctx, pos, nseq, lat


def _ssd_dir(dm, ci, rev, x_ref, bc_ref, dt_ref, bias_ref, a_ref, h0_ref, y_ref, s_ref, h_scr):
    Q, N, SH, P = SSD_CHUNK, dm.N, dm.SH, SSD_HEAD_DIM
    hg = SH // SSD_GROUPS
    gw = hg * P
    is_ctx, pos, nseq, _ = _chunk_pos(dm, ci)
    first = pos == (nseq - 1 if rev else 0)
    last = pos == (0 if rev else nseq - 1)

    @pl.when(jnp.logical_and(first, is_ctx))
    def _():
        h_scr[...] = jnp.zeros_like(h_scr)

    @pl.when(jnp.logical_and(first, jnp.logical_not(is_ctx)))
    def _():
        for k in range(dm.Bw // _LANES):
            h_scr[:, k * _LANES:(k + 1) * _LANES] = h0_ref[k * _LANES:(k + 1) * _LANES, :].T

    lane0 = (SH if rev else 0)
    dt = _softplus(dt_ref[...] + bias_ref[...])
    a = dt * a_ref[...]
    ri = lax.broadcasted_iota(jnp.int32, (Q, Q), 0)
    cj = lax.broadcasted_iota(jnp.int32, (Q, Q), 1)
    tri = (ri <= cj) if rev else (ri >= cj)
    cum = jnp.dot(tri.astype(_F32), a, preferred_element_type=_F32, precision=lax.Precision.HIGHEST)
    cum_t, dt_t = cum.T, dt.T
    ecum = jnp.exp(cum)
    end = 0 if rev else Q - 1
    wend = jnp.exp(cum[end:end + 1, :] - cum) * dt
    lane = lax.broadcasted_iota(jnp.int32, (Q, _LANES), 1)
    low = lane < P

    for g in range(SSD_GROUPS):
        b_g = bc_ref[:, g * N:(g + 1) * N]
        c_g = bc_ref[:, dm.GN + g * N:dm.GN + (g + 1) * N]
        cb = lax.dot_general(c_g, b_g, (((1,), (1,)), ((), ())), preferred_element_type=_F32)
        gs = slice(g * gw, (g + 1) * gw)
        y_diag, e_exp, w_exp = [], [], []
        for p in range(hg // 2):
            xs = slice(g * gw + p * _LANES, g * gw + (p + 1) * _LANES)
            xp = x_ref[:, xs].astype(_BF16)
            ys = []
            for s in range(2):
                ln = lane0 + g * hg + 2 * p + s
                seg = cum[:, ln:ln + 1] - cum_t[ln:ln + 1, :]
                m = cb * jnp.exp(jnp.where(tri, seg, _NEG)) * dt_t[ln:ln + 1, :]
                ys.append(jnp.dot(m.astype(_BF16), xp, preferred_element_type=_F32))
            ln = lane0 + g * hg + 2 * p
            y_diag.append(jnp.where(low, ys[0], ys[1]))
            e_exp.append(jnp.where(low, ecum[:, ln:ln + 1], ecum[:, ln + 1:ln + 2]))
            w_exp.append(jnp.where(low, wend[:, ln:ln + 1], wend[:, ln + 1:ln + 2]))
        y_diag = jnp.concatenate(y_diag, axis=1)
        e_exp = jnp.concatenate(e_exp, axis=1)
        w_exp = jnp.concatenate(w_exp, axis=1)
        h_in = h_scr[:, gs]
        y_off = jnp.dot(c_g, h_in.astype(_BF16), preferred_element_type=_F32) * e_exp
        y_ref[:, gs] = y_diag + y_off
        xw = (x_ref[:, gs] * w_exp).astype(_BF16)
        st = lax.dot_general(b_g, xw, (((0,), (0,)), ((), ())), preferred_element_type=_F32)
        h_scr[:, gs] = h_in * e_exp[end:end + 1, :] + st

    @pl.when(jnp.logical_and(last, is_ctx))
    def _():
        for k in range(dm.Bw // _LANES):
            s_ref[k * _LANES:(k + 1) * _LANES, :] = h_scr[:, k * _LANES:(k + 1) * _LANES].T


def _ssd_body(xf_ref, bcf_ref, dtf_ref, xb_ref, bcb_ref, dtb_ref, bias_ref, a_ref, h0f_ref, h0b_ref,
              yf_ref, yb_ref, sf_ref, sb_ref, hf_scr, hb_scr, *, dm):
    i = pl.program_id(0)
    n = pl.num_programs(0)
    _ssd_dir(dm, i, False, xf_ref, bcf_ref, dtf_ref, bias_ref, a_ref, h0f_ref, yf_ref, sf_ref, hf_scr)
    _ssd_dir(dm, n - 1 - i, True, xb_ref, bcb_ref, dtb_ref, bias_ref, a_ref, h0b_ref, yb_ref, sb_ref, hb_scr)


def _ssd(dm, slab, xs, bc, dt_bias, a_neg, state_ssd5, l):
    Q = SSD_CHUNK
    nchunks = dm.T // Q
    ncc = dm.Lc // Q
    HP = dm.SH * SSD_HEAD_DIM
    dcol = dm.s_dt // _LANES
    fwd = lambda i: i
    bwd = lambda i: nchunks - 1 - i

    def seq_c(ci):
        return jnp.minimum(ci // ncc, dm.B - 1)

    def lat_of(ci):
        return _chunk_pos(dm, ci)[3]

    def specs(order):
        return [pl.BlockSpec((Q, dm.Bw), lambda i: (order(i), 0)),
                pl.BlockSpec((Q, 2 * dm.GN), lambda i: (order(i), 0)),
                pl.BlockSpec((Q, _LANES), lambda i: (order(i), dcol))]

    row = pl.BlockSpec((1, _LANES), lambda i: (0, 0))
    h0 = lambda order, d: pl.BlockSpec((None, None, None, HP, dm.N), lambda i: (lat_of(order(i)), l, d, 0, 0))
    st = lambda order: pl.BlockSpec((None, HP, dm.N), lambda i: (seq_c(order(i)), 0, 0))
    ysd = jax.ShapeDtypeStruct((dm.T, dm.Bw), _F32)
    ssd = jax.ShapeDtypeStruct((dm.B, HP, dm.N), _F32)
    return _call(
        functools.partial(_ssd_body, dm=dm), grid=(nchunks,),
        in_specs=specs(fwd) + specs(bwd) + [row, row, h0(fwd, 0), h0(bwd, 1)],
        out_specs=[pl.BlockSpec((Q, dm.Bw), lambda i: (fwd(i), 0)), pl.BlockSpec((Q, dm.Bw), lambda i: (bwd(i), 0)),
                   st(fwd), st(bwd)],
        out_shape=[ysd, ysd, ssd, ssd],
        scratch=[pltpu.VMEM((dm.N, dm.Bw), _F32), pltpu.VMEM((dm.N, dm.Bw), _F32)], name="ssd")(
            xs, bc, slab, xs, bc, slab, dt_bias, a_neg, state_ssd5, state_ssd5)


def _lru_gates(dm, d, x_ref, wr_ref, wi_ref, br_ref, bi_ref, lam_ref, a_scr, u_scr):
    sp = _softplus(-lam_ref[d:d + 1, :])
    for h in range(dm.LH):
        hs = slice(h * _LANES, (h + 1) * _LANES)
        x = x_ref[:, hs]
        xb = x.astype(_BF16)
        r = jax.nn.sigmoid(jnp.dot(xb, wr_ref[d, h], preferred_element_type=_F32) + br_ref[d:d + 1, hs])
        ig = jax.nn.sigmoid(jnp.dot(xb, wi_ref[d, h], preferred_element_type=_F32) + bi_ref[d:d + 1, hs])
        log_a = -LRU_C * r * sp[:, hs]
        a = jnp.exp(log_a)
        a_scr[:, hs] = a
        u_scr[:, hs] = jnp.sqrt(-jnp.tanh(log_a) * (a * a + 1.0)) * (ig * x)


def _lru_body(xf_ref, xb_ref, wr_ref, wi_ref, br_ref, bi_ref, lam_ref, h0f_ref, h0b_ref,
              hf_ref, hb_ref, finf_ref, finb_ref, af, uf, ab, ub, carry, *, dm):
    R = _ROW_BLOCK
    i = pl.program_id(0)
    n = pl.num_programs(0)
    _lru_gates(dm, 0, xf_ref, wr_ref, wi_ref, br_ref, bi_ref, lam_ref, af, uf)
    _lru_gates(dm, 1, xb_ref, wr_ref, wi_ref, br_ref, bi_ref, lam_ref, ab, ub)

    posf, Lf = dm.seq_pos(i * R)
    posb, Lb = dm.seq_pos((n - 1 - i) * R)
    ctx_f = i * R < dm.Tc
    ctx_b = (n - 1 - i) * R < dm.Tc
    for d, start, is_ctx, h0_ref in ((0, posf == 0, ctx_f, h0f_ref), (1, posb + R == Lb, ctx_b, h0b_ref)):
        @pl.when(jnp.logical_and(start, is_ctx))
        def _():
            carry[d:d + 1, :] = jnp.zeros((1, dm.Bw), _F32)

        @pl.when(jnp.logical_and(start, jnp.logical_not(is_ctx)))
        def _():
            carry[d:d + 1, :] = h0_ref[d:d + 1, :]

    def tile(t, hs):
        hf, hb = hs
        base_f = pl.multiple_of(t * _SUBLANES, _SUBLANES)
        base_b = pl.multiple_of(R - _SUBLANES - t * _SUBLANES, _SUBLANES)
        for r in range(_SUBLANES):
            rf = base_f + r
            hf = af[pl.ds(rf, 1), :] * hf + uf[pl.ds(rf, 1), :]
            hf_ref[pl.ds(rf, 1), :] = hf
            rb = base_b + (_SUBLANES - 1 - r)
            hb = ab[pl.ds(rb, 1), :] * hb + ub[pl.ds(rb, 1), :]
            hb_ref[pl.ds(rb, 1), :] = hb
        return hf, hb

    hf, hb = lax.fori_loop(0, R // _SUBLANES, tile, (carry[0:1, :], carry[1:2, :]))
    carry[0:1, :] = hf
    carry[1:2, :] = hb

    @pl.when(jnp.logical_and(posf + R == Lf, ctx_f))
    def _():
        finf_ref[...] = hf

    @pl.when(jnp.logical_and(posb == 0, ctx_b))
    def _():
        finb_ref[...] = hb


def _lru(dm, xf, w_r, w_i, b_r, b_i, lam, state_lru, l):
    R = _ROW_BLOCK
    nblk = dm.T // R
    fwd = lambda i: i
    bwd = lambda i: nblk - 1 - i
    wspec = pl.BlockSpec((2, dm.LH, _LANES, _LANES), lambda i: (0, 0, 0, 0))
    vspec = pl.BlockSpec((2, dm.Bw), lambda i: (0, 0))
    lat_seq = lambda blk: jnp.clip((blk * R - dm.Tc) // dm.Ll, 0, dm.DB - 1)
    ctx_seq = lambda blk: jnp.minimum(blk * R // dm.Lc, dm.B - 1)
    h0 = lambda order: pl.BlockSpec((None, None, 2, dm.Bw), lambda i: (lat_seq(order(i)), l, 0, 0))
    fin = lambda order: pl.BlockSpec((None, 1, dm.Bw), lambda i: (ctx_seq(order(i)), 0, 0))
    hsd = jax.ShapeDtypeStruct((dm.T, dm.Bw), _F32)
    fsd = jax.ShapeDtypeStruct((dm.B, 1, dm.Bw), _F32)
    return _call(
        functools.partial(_lru_body, dm=dm), grid=(nblk,),
        in_specs=[pl.BlockSpec((R, dm.Bw), lambda i: (fwd(i), 0)), pl.BlockSpec((R, dm.Bw), lambda i: (bwd(i), 0)),
                  wspec, wspec, vspec, vspec, vspec, h0(fwd), h0(bwd)],
        out_specs=[pl.BlockSpec((R, dm.Bw), lambda i: (fwd(i), 0)), pl.BlockSpec((R, dm.Bw), lambda i: (bwd(i), 0)),
                   fin(fwd), fin(bwd)],
        out_shape=[hsd, hsd, fsd, fsd],
        scratch=[pltpu.VMEM((R, dm.Bw), _F32)] * 4 + [pltpu.VMEM((2, dm.Bw), _F32)], name="lru")(
            xf, xf, w_r, w_i, b_r, b_i, lam, state_lru, state_lru)


def _finish_body(yf_ref, yb_ref, xs_ref, z_ref, d_ref, ng_ref, hf_ref, hb_ref, gr_ref, _mix_in, o_ref):
    y = (yf_ref[...] + yb_ref[...] + d_ref[...] * xs_ref[...]) * _silu(z_ref[...])
    gw = y.shape[1] // SSD_GROUPS
    for g in range(SSD_GROUPS):
        gs = slice(g * gw, (g + 1) * gw)
        yg = y[:, gs]
        yg = yg * lax.rsqrt(jnp.mean(yg * yg, axis=-1, keepdims=True) + EPS)
        o_ref[:, gs] = (yg * ng_ref[:, gs]).astype(o_ref.dtype)
    bw = y.shape[1]
    o_ref[:, bw:] = ((hf_ref[...] + hb_ref[...]) * jax.nn.gelu(gr_ref[...], approximate=True)).astype(o_ref.dtype)


def _finish(dm, slab, yf, yb, xs, d_exp, norm_g, hf, hb, mix):
    R = _ROW_BLOCK
    blk = pl.BlockSpec((R, dm.Bw), lambda i: (i, 0))
    row = pl.BlockSpec((1, dm.Bw), lambda i: (0, 0))
    return _call(
        _finish_body, grid=(dm.T // R,),
        in_specs=[blk, blk, blk, pl.BlockSpec((R, dm.Bw), lambda i: (i, dm.s_z // dm.Bw)), row, row,
                  blk, blk, pl.BlockSpec((R, dm.Bw), lambda i: (i, dm.s_gr // dm.Bw)),
                  pl.BlockSpec(memory_space=pl.ANY)],
        out_specs=pl.BlockSpec((R, 2 * dm.Bw), lambda i: (i, _MIX_SSD_LRU)),
        out_shape=jax.ShapeDtypeStruct(mix.shape, mix.dtype), aliases={9: 0}, name="finish")(
            yf, yb, xs, slab, d_exp, norm_g.reshape(1, -1), hf, hb, slab, mix)


def _layer_norm(r, g, b):
    mu = jnp.mean(r, axis=-1, keepdims=True)
    c = r - mu
    var = jnp.mean(c * c, axis=-1, keepdims=True)
    return c * lax.rsqrt(var + EPS) * g + b


def _split_bf16(x):
    hi = x.astype(_BF16)
    return hi, (x - hi.astype(_F32)).astype(_BF16)


_HI16 = 0xFFFF0000


def _pack_bf16_pair(a, b):
    ua = pltpu.bitcast(a.astype(_BF16).astype(_F32), jnp.uint32)
    ub = pltpu.bitcast(b.astype(_BF16).astype(_F32), jnp.uint32)
    return (ua >> 16) | (ub & jnp.uint32(_HI16))


def _unpack_bf16_pair(p):
    return pltpu.bitcast(p << 16, _F32), pltpu.bitcast(p & jnp.uint32(_HI16), _F32)


def _norm_router_body(mix_ref, xc_ref, xl_ref, g1_ref, lg_ref, lb_ref, sc_ref, sh_ref, wr_ref, x1_ref, h_ref, s_ref,
                      *, alpha, ctx_tiles):
    x = jnp.where(pl.program_id(0) < ctx_tiles, xc_ref[...], xl_ref[...])
    x1 = _layer_norm(alpha * x + g1_ref[...] * mix_ref[...], lg_ref[...], lb_ref[...])
    x1_ref[...] = x1
    h = x1 * (1.0 + sc_ref[...]) + sh_ref[...]
    half = h.shape[1] // 2
    h_ref[...] = _pack_bf16_pair(h[:, :half], h[:, half:])
    hh, hl = _split_bf16(h)
    wh, wl = _split_bf16(wr_ref[...])
    logits = (jnp.dot(hh, wh, preferred_element_type=_F32) + jnp.dot(hh, wl, preferred_element_type=_F32)
              + jnp.dot(hl, wh, preferred_element_type=_F32))
    s_ref[...] = jax.nn.sigmoid(logits)


def _stream_specs(dm, tm, width):
    ctx_tiles = dm.Tc // tm
    return [pl.BlockSpec((tm, width), lambda i, *_: (jnp.minimum(i, ctx_tiles - 1), 0)),
            pl.BlockSpec((tm, width), lambda i, *_: (jnp.maximum(i - ctx_tiles, 0), 0))]


def _norm_router(dm, mix, xc, xl, mod5, ln_g, ln_b, w_router_pad, l, alpha):
    tm = 256
    D = dm.D
    tile = pl.BlockSpec((tm, D), lambda i: (i, 0))
    row = pl.BlockSpec((1, D), lambda i: (0, 0))
    return _call(
        functools.partial(_norm_router_body, alpha=alpha, ctx_tiles=dm.Tc // tm), grid=(dm.T // tm,),
        in_specs=[tile] + _stream_specs(dm, tm, D) + [
            _mod_spec(dm, l, 2, tm), row, row, _mod_spec(dm, l, 4, tm), _mod_spec(dm, l, 3, tm),
            pl.BlockSpec((D, _LANES), lambda i: (0, 0))],
        out_specs=[tile, pl.BlockSpec((tm, D // 2), lambda i: (i, 0)), pl.BlockSpec((tm, _LANES), lambda i: (i, 0))],
        out_shape=[jax.ShapeDtypeStruct((dm.T, D), _F32), jax.ShapeDtypeStruct((dm.T, D // 2), jnp.uint32),
                   jax.ShapeDtypeStruct((dm.T, _LANES), _F32)], name="norm_router")(
            mix, xc, xl, mod5, ln_g.reshape(1, D), ln_b.reshape(1, D), mod5, mod5, w_router_pad)


def _top2(v):
    n = v.shape[-1]
    i1 = jnp.argmax(v, axis=-1)
    m1 = jnp.max(v, axis=-1)
    rest = jnp.where(jnp.arange(n) == i1[..., None], -jnp.inf, v)
    return (m1, jnp.max(rest, axis=-1)), (i1, jnp.argmax(rest, axis=-1))


def _route(s, router_bias, tm):
    T, E = s.shape
    per = E // N_EXPERT_GROUPS
    assert TOP_K == 2
    sel = s + router_bias.astype(_F32)
    grp_score = sum(_top2(sel.reshape(T, N_EXPERT_GROUPS, per))[0])
    grp = jnp.argmax(grp_score, axis=-1)
    in_grp = (jnp.arange(E) // per)[None, :] == grp[:, None]
    idx = jnp.stack(_top2(jnp.where(in_grp, sel, -jnp.inf))[1], axis=-1)
    wts = jnp.take_along_axis(s, idx, axis=-1)
    wts = wts / jnp.sum(wts, -1, keepdims=True)
    A = T * TOP_K
    onehot = (idx.reshape(A, 1) == jnp.arange(E, dtype=idx.dtype)[None, :]).astype(jnp.int32)
    csum = jnp.cumsum(onehot, axis=0)
    counts = csum[-1]
    padded = (counts + tm - 1) // tm * tm
    pend = jnp.cumsum(padded)
    pstart = pend - padded
    pos = jnp.sum(onehot * (csum - 1 + pstart[None, :]), axis=1).astype(jnp.int32)
    n_blocks = -(-A // tm) + E
    src_tok = jnp.zeros((n_blocks * tm,), jnp.int32).at[pos].set(jnp.arange(A, dtype=jnp.int32) // TOP_K)
    blk0 = jnp.arange(n_blocks, dtype=jnp.int32) * tm
    block_e = jnp.minimum(jnp.sum((blk0[:, None] >= pend[None, :]).astype(jnp.int32), axis=1), E - 1)
    n_used = (pend[-1] // tm).astype(jnp.int32).reshape(1)
    n_valid = jnp.clip(counts[block_e] - (blk0 - pstart[block_e]), 0, tm)
    n_valid = -(-n_valid // _DMA_UNROLL) * _DMA_UNROLL
    n_valid = jnp.where(blk0 < pend[-1], n_valid, 0).astype(jnp.int32)
    return src_tok, block_e.astype(jnp.int32), n_used, n_valid, pos, wts


def _row_copy(src_hbm, row, dst, dst_row, sem):
    return pltpu.make_async_copy(src_hbm.at[pl.ds(row, 1)], dst.at[pl.ds(dst_row, 1)], sem)


_DMA_UNROLL = 8


def _gather_body(src_ref, nv_ref, h_hbm, o_ref, buf, sem):
    tm = o_ref.shape[0]
    i = pl.program_id(0)
    n = pl.num_programs(0)

    def issue(step, slot):
        def body(g, c):
            for u in range(_DMA_UNROLL):
                r = g * _DMA_UNROLL + u
                _row_copy(h_hbm, src_ref[step * tm + r], buf.at[slot], r, sem.at[slot]).start()
            return c
        lax.fori_loop(0, nv_ref[step] // _DMA_UNROLL, body, 0)

    @pl.when(i == 0)
    def _():
        buf[...] = jnp.zeros_like(buf)
        issue(0, 0)

    @pl.when(i + 1 < n)
    def _():
        issue(i + 1, (i + 1) % 2)

    slot = i % 2

    def wait(g, c):
        for u in range(_DMA_UNROLL):
            _row_copy(h_hbm, 0, buf.at[slot], g * _DMA_UNROLL + u, sem.at[slot]).wait()
        return c
    lax.fori_loop(0, nv_ref[i] // _DMA_UNROLL, wait, 0)
    half = buf.shape[2]
    lo, hi = _unpack_bf16_pair(buf[slot])
    o_ref[:, :half] = lo.astype(o_ref.dtype)
    o_ref[:, half:] = hi.astype(o_ref.dtype)


def _gather(h_packed, src_tok, n_valid, tm):
    half = h_packed.shape[1]
    n_blocks = src_tok.shape[0] // tm
    return _call(
        _gather_body, grid=(n_blocks,), nprefetch=2,
        in_specs=[pl.BlockSpec(memory_space=pl.ANY)],
        out_specs=pl.BlockSpec((tm, 2 * half), lambda i, src, nv: (i, 0)),
        out_shape=jax.ShapeDtypeStruct((n_blocks * tm, 2 * half), _BF16),
        scratch=[pltpu.VMEM((2, tm, half), jnp.uint32), pltpu.SemaphoreType.DMA((2,))], name="moe_gather")(
            src_tok, n_valid, h_packed)


def _expert_changed(be_ref, m):
    return jnp.logical_or(m == 0, be_ref[m] != be_ref[jnp.maximum(m - 1, 0)])


def _gateup_body(be_ref, nu_ref, x_ref, wg_ref, wu_ref, g_ref, wg_s, wu_s):
    m = pl.program_id(1)

    @pl.when(_expert_changed(be_ref, m))
    def _():
        wg_s[...] = wg_ref[...].astype(_BF16)
        wu_s[...] = wu_ref[...].astype(_BF16)

    @pl.when(m < nu_ref[0])
    def _():
        x = x_ref[...]
        gate = jnp.dot(x, wg_s[...], preferred_element_type=_F32)
        up = jnp.dot(x, wu_s[...], preferred_element_type=_F32)
        g_ref[...] = (_silu(gate) * up).astype(g_ref.dtype)

    @pl.when(m >= nu_ref[0])
    def _():
        g_ref[...] = jnp.zeros_like(g_ref)


def _gateup(xs, block_e, n_used, w_gate, w_up, l, tm):
    A_pad, D = xs.shape
    F = w_gate.shape[-1]
    tf = _pick_tile(F, (512, 256, 128))
    wspec = pl.BlockSpec((None, None, D, tf), lambda j, m, be, nu: (l, be[m], 0, j))
    return _call(
        _gateup_body, grid=(F // tf, A_pad // tm), nprefetch=2,
        in_specs=[pl.BlockSpec((tm, D), lambda j, m, be, nu: (m, 0)), wspec, wspec],
        out_specs=pl.BlockSpec((tm, tf), lambda j, m, be, nu: (m, j)),
        out_shape=jax.ShapeDtypeStruct((A_pad, F), _BF16),
        scratch=[pltpu.VMEM((D, tf), _BF16)] * 2, name="moe_gateup")(block_e, n_used, xs, w_gate, w_up)


def _down_body(be_ref, nu_ref, g_ref, wd_ref, y_ref, wd_s):
    m = pl.program_id(1)

    @pl.when(_expert_changed(be_ref, m))
    def _():
        wd_s[...] = wd_ref[...].astype(_BF16)

    @pl.when(m < nu_ref[0])
    def _():
        y = jnp.dot(g_ref[...], wd_s[...], preferred_element_type=_F32)
        half = y.shape[1] // 2
        y_ref[...] = _pack_bf16_pair(y[:, :half], y[:, half:])

    @pl.when(m >= nu_ref[0])
    def _():
        y_ref[...] = jnp.zeros_like(y_ref)


def _down_tile(D):
    return _pick_tile(D, (2048, 1024, 512, 256))


def _down(g, block_e, n_used, w_down, l, tm):
    A_pad, F = g.shape
    D = w_down.shape[-1]
    tn = _down_tile(D)
    return _call(
        _down_body, grid=(D // tn, A_pad // tm), nprefetch=2,
        in_specs=[pl.BlockSpec((tm, F), lambda j, m, be, nu: (m, 0)),
                  pl.BlockSpec((None, None, F, tn), lambda j, m, be, nu: (l, be[m], 0, j))],
        out_specs=pl.BlockSpec((tm, tn // 2), lambda j, m, be, nu: (m, j)),
        out_shape=jax.ShapeDtypeStruct((A_pad, D // 2), jnp.uint32),
        scratch=[pltpu.VMEM((F, tn), _BF16)], name="moe_down")(block_e, n_used, g, w_down)


def _combine_body(pos_ref, y_hbm, x1_ref, w_ref, g2_ref, lg_ref, lb_ref, *rest, alpha, ctx_tiles, final, tn):
    if final:
        xp_ref, xs_ref, x2_scr, buf, sem = rest
    else:
        sc_ref, sh_ref, xp_ref, xs_ref, hn_ref, x2_scr, buf, sem = rest
    tm = x1_ref.shape[0]
    i = pl.program_id(0)
    n = pl.num_programs(0)
    slot = i % 2
    nslot = (i + 1) % 2
    nstep = jnp.minimum(i + 1, n - 1)

    def start(step, sl, r):
        for k in range(TOP_K):
            _row_copy(y_hbm, pos_ref[(step * tm + r) * TOP_K + k], buf.at[sl, k], r, sem.at[sl]).start()

    def wait_all(sl):
        def wait(g, c):
            for u in range(_DMA_UNROLL // TOP_K):
                for k in range(TOP_K):
                    _row_copy(y_hbm, 0, buf.at[sl, k], g * (_DMA_UNROLL // TOP_K) + u, sem.at[sl]).wait()
            return c
        lax.fori_loop(0, tm * TOP_K // _DMA_UNROLL, wait, 0)

    @pl.when(i == 0)
    def _():
        def body(g, c):
            for u in range(_DMA_UNROLL // TOP_K):
                start(0, 0, g * (_DMA_UNROLL // TOP_K) + u)
            return c
        lax.fori_loop(0, tm * TOP_K // _DMA_UNROLL, body, 0)

    wait_all(slot)
    chunk = 32
    for c0 in range(0, tm, chunk):
        for r in range(c0, c0 + chunk):
            start(nstep, nslot, r)
        rows = slice(c0, c0 + chunk)
        w = w_ref[rows, :]
        ffn = None
        for k in range(TOP_K):
            parts = []
            for j in range(2 * buf.shape[3] // tn):
                parts += _unpack_bf16_pair(buf[slot, k, rows, j * (tn // 2):(j + 1) * (tn // 2)])
            term = w[:, k:k + 1] * jnp.concatenate(parts, axis=1)
            ffn = term if ffn is None else ffn + term
        x2 = _layer_norm(alpha * x1_ref[rows, :] + g2_ref[...] * ffn, lg_ref[...], lb_ref[...])
        x2_scr[rows, :] = x2
        if not final:
            hn_ref[rows, :] = (x2 * (1.0 + sc_ref[...]) + sh_ref[...]).astype(hn_ref.dtype)

    @pl.when(i < ctx_tiles)
    def _():
        xp_ref[...] = x2_scr[...]

    @pl.when(i >= ctx_tiles)
    def _():
        xs_ref[...] = x2_scr[...]

    @pl.when(i == n - 1)
    def _():
        wait_all(nslot)


def _combine(dm, y_packed, pos, wts, x1, mod5, ln_g, ln_b, l, alpha):
    tm = 256
    D = dm.D
    final = l == dm.depth - 1
    tile = pl.BlockSpec((tm, D), lambda i, p: (i, 0))
    row = pl.BlockSpec((1, D), lambda i, p: (0, 0))
    in_specs = [pl.BlockSpec(memory_space=pl.ANY), tile, pl.BlockSpec((tm, TOP_K), lambda i, p: (i, 0)),
                _mod_spec(dm, l, 5, tm), row, row]
    args = [pos, y_packed, x1, wts, mod5, ln_g.reshape(1, D), ln_b.reshape(1, D)]
    out_specs = _stream_specs(dm, tm, D)
    out_shape = [jax.ShapeDtypeStruct((dm.Tc, D), _F32), jax.ShapeDtypeStruct((dm.Tl, D), _F32)]
    if not final:
        in_specs += [_mod_spec(dm, l + 1, 1, tm), _mod_spec(dm, l + 1, 0, tm)]
        args += [mod5, mod5]
        out_specs = out_specs + [tile]
        out_shape = out_shape + [jax.ShapeDtypeStruct((dm.T, D), _BF16)]
    return _call(
        functools.partial(_combine_body, alpha=alpha, ctx_tiles=dm.Tc // tm, final=final, tn=_down_tile(D)),
        grid=(dm.T // tm,), nprefetch=1, in_specs=in_specs, out_specs=out_specs, out_shape=out_shape,
        scratch=[pltpu.VMEM((tm, D), _F32), pltpu.VMEM((2, TOP_K, tm, D // 2), jnp.uint32),
                 pltpu.SemaphoreType.DMA((2,))],
        name="moe_combine")(*args)


def kernel(x_prompt, x_sample, cache_k, cache_v, state_ssd, state_lru, c, c_ctx, w_ada, b_ada, w_in, w_out, attn_sink, ssd_conv_w, ssd_conv_b, ssd_a_log, ssd_dt_bias, ssd_d, ssd_norm_g, lru_conv_w, lru_conv_b, lru_lambda, lru_w_r, lru_b_r, lru_w_i, lru_b_i, pool_w, pool_b, pool_scale, ln1_g, ln1_b, ln2_g, ln2_b, w_router, router_bias, moe_w_gate, moe_w_up, moe_w_down):
    dm = _Dims(x_prompt, x_sample, cache_k, state_ssd, lru_w_r, w_router, moe_w_gate, w_ada)
    depth, D, Bw, KVw, GN, SH = dm.depth, dm.D, dm.Bw, dm.KVw, dm.GN, dm.SH
    alpha = (2 * depth) ** 0.25

    rows = -(-(1 + dm.DB) // _SUBLANES) * _SUBLANES
    cond = jnp.zeros((rows, D), _F32).at[0].set(c_ctx).at[1:1 + dm.DB].set(c)
    mod5 = _ada(cond, w_ada, b_ada).reshape(depth, rows, 6, 1, D)

    xc, xl = x_prompt.reshape(dm.Tc, D), x_sample.reshape(dm.Tl, D)
    h = _modulate(dm, xc, xl, mod5, 0)

    cos, sin = _rope_tables(dm.Ll)
    cache_k4 = cache_k.reshape(dm.DB, depth, dm.past, KVw)
    cache_v4 = cache_v.reshape(dm.DB, depth, dm.past, KVw)
    state_ssd5 = state_ssd.reshape(dm.DB, depth, 2, SH * SSD_HEAD_DIM, dm.N)
    w_router_pad = jnp.zeros((D, _LANES), _F32).at[:, :dm.E].set(w_router)

    o_k, o_v, o_z = Bw, Bw + KVw, Bw + 2 * KVw
    o_xs = o_z + Bw
    o_B = o_xs + Bw
    o_dt = o_B + 2 * GN
    o_xr = o_dt + 2 * SH
    o_gr, o_xp = o_xr + Bw, o_xr + 2 * Bw

    ks, vs, ssds, lrus = [], [], [], []
    for l in range(depth):
        wi = w_in[l]
        cols = lambda o, w: wi[:, o:o + w]
        w_slab = jnp.concatenate(
            [cols(0, Bw), cols(o_z, Bw), cols(o_xs, Bw), cols(o_xr, Bw), cols(o_gr, Bw), cols(o_xp, Bw),
             cols(o_k, KVw), cols(o_v, KVw), cols(o_B, 2 * GN), cols(o_dt, 2 * SH),
             jnp.zeros((D, dm.slab_w - dm.s_dt - 2 * SH), wi.dtype)], axis=1).astype(_BF16)
        slab = _matmul(h, w_slab, "in_proj")
        mix_in = jnp.zeros((dm.T, D), _BF16)

        mix_in = _attn_ctx(dm, slab, attn_sink[l], mix_in)
        qr, kr = _rope(dm, slab, cos, sin)
        mix_in = _attn_lat(dm, slab, qr, kr, cache_k4, cache_v4, attn_sink[l], l, mix_in)

        cw = ssd_conv_w[l]
        xs = _conv(dm, slab, dm.s_xs, Bw, cw[:, :Bw], ssd_conv_b[l][:Bw], True, _F32)
        bc = _conv(dm, slab, dm.s_B, 2 * GN, cw[:, Bw:], ssd_conv_b[l][Bw:], True, _BF16)
        pad = jnp.zeros((_LANES - 2 * SH,), _F32)
        dt_bias = jnp.concatenate([ssd_dt_bias[l].reshape(-1), pad]).reshape(1, _LANES)
        a_neg = jnp.concatenate([-jnp.exp(ssd_a_log[l].reshape(-1)), pad]).reshape(1, _LANES)
        yf, yb, s_f, s_b = _ssd(dm, slab, xs, bc, dt_bias, a_neg, state_ssd5, l)

        xf = _conv(dm, slab, dm.s_xr, Bw, lru_conv_w[l], lru_conv_b[l], False, _F32)
        hf, hb, fin_f, fin_b = _lru(dm, xf, lru_w_r[l].astype(_BF16), lru_w_i[l].astype(_BF16),
                                    lru_b_r[l], lru_b_i[l], lru_lambda[l], state_lru, l)

        d_exp = jnp.repeat(ssd_d[l], SSD_HEAD_DIM).reshape(1, Bw)
        mix_in = _finish(dm, slab, yf, yb, xs, d_exp, ssd_norm_g[l], hf, hb, mix_in)
        mix_in = _pool(dm, slab, pool_w[l], pool_b[l], pool_scale[l], mix_in)

        wo = w_out[l]
        w_mix = jnp.concatenate([wo[Bw:3 * Bw], wo[:Bw], wo[3 * Bw:]], axis=0).astype(_BF16)
        mix = _matmul(mix_in, w_mix, "out_proj")
        x1, h2, s_pad = _norm_router(dm, mix, xc, xl, mod5, ln1_g[l], ln1_b[l], w_router_pad, l, alpha)

        src_tok, block_e, n_used, n_valid, pos, wts = _route(s_pad[:, :dm.E], router_bias, _MOE_TM)
        xs_sorted = _gather(h2, src_tok, n_valid, _MOE_TM)
        g = _gateup(xs_sorted, block_e, n_used, moe_w_gate, moe_w_up, l, _MOE_TM)
        y = _down(g, block_e, n_used, moe_w_down, l, _MOE_TM)
        outs = _combine(dm, y, pos, wts, x1, mod5, ln2_g[l], ln2_b[l], l, alpha)
        xc, xl = outs[0], outs[1]
        if l + 1 < depth:
            h = outs[2]

        ks.append(slab[:dm.Tc, dm.s_k:dm.s_k + KVw].reshape(dm.B, dm.Lc, dm.KV, ATT_HEAD_DIM))
        vs.append(slab[:dm.Tc, dm.s_v:dm.s_v + KVw].reshape(dm.B, dm.Lc, dm.KV, ATT_HEAD_DIM))
        ssds.append(jnp.stack([s_f, s_b], axis=1).reshape(dm.B, 2, SH, SSD_HEAD_DIM, dm.N))
        lrus.append(jnp.concatenate([fin_f, fin_b], axis=1))

    y_prompt = xc.reshape(dm.B, dm.Lc, D)
    y_sample = xl.reshape(dm.DB, dm.Ll, D)
    return (y_prompt, y_sample, jnp.stack(ks, axis=1), jnp.stack(vs, axis=1),
            jnp.stack(ssds, axis=1), jnp.stack(lrus, axis=1))
```

```python
import functools
import math

import jax
import jax.numpy as jnp
import numpy as np
from jax import lax
from jax.experimental import pallas as pl
from jax.experimental.pallas import tpu as pltpu

_F32 = jnp.float32
_BF16 = jnp.bfloat16

GRID_W = 64
ATT_HEAD_DIM = 128
ATT_BLOCK = 128
ROPE_BASE = 10000.0
SSD_HEAD_DIM = 64
SSD_GROUPS = 2
SSD_CHUNK = 128
LRU_C = 8.0
POOL_WINDOWS = (2, 4, 8, 16)
N_EXPERT_GROUPS = 4
TOP_K = 2
EPS = 1e-5

_LANES = 128
_SUBLANES = 8
_VMEM_BYTES_V7X = 64 * 1024 * 1024
_VMEM_LIMIT = _VMEM_BYTES_V7X * 7 // 8

_ROW_BLOCK = 256
_MOE_TM = 512
_NEG = -1e30
_MIX_SSD_LRU, _MIX_ATT, _MIX_POOL = 0, 2, 3


def _call(body, *, grid, in_specs, out_specs, out_shape, scratch=(), nprefetch=0, aliases=None, name=None):
    grid_spec = pltpu.PrefetchScalarGridSpec(
        num_scalar_prefetch=nprefetch, grid=grid, in_specs=in_specs, out_specs=out_specs,
        scratch_shapes=list(scratch))
    return pl.pallas_call(
        body, grid_spec=grid_spec, out_shape=out_shape,
        compiler_params=pltpu.CompilerParams(
            dimension_semantics=("arbitrary",) * len(grid), vmem_limit_bytes=_VMEM_LIMIT),
        input_output_aliases=aliases or {}, name=name)


def _softplus(x):
    return jnp.maximum(x, 0.0) + jnp.log1p(jnp.exp(-jnp.abs(x)))


def _silu(x):
    return x * jax.nn.sigmoid(x)


class _Dims:
    def __init__(self, x_prompt, x_sample, cache_k, state_ssd, lru_w_r, w_router, moe_w_gate, w_ada):
        self.B, self.Lc, self.D = x_prompt.shape
        self.DB, self.Ll, _ = x_sample.shape
        self.depth = w_ada.shape[0]
        self.Tc, self.Tl = self.B * self.Lc, self.DB * self.Ll
        self.T = self.Tc + self.Tl
        self.Bw = self.D // 4
        self.past, self.KV = cache_k.shape[2], cache_k.shape[3]
        self.H = self.Bw // ATT_HEAD_DIM
        self.G = self.H // self.KV
        self.KVw = self.KV * ATT_HEAD_DIM
        self.SH = self.Bw // SSD_HEAD_DIM
        self.N = state_ssd.shape[-1]
        self.GN = SSD_GROUPS * self.N
        self.LH = lru_w_r.shape[2]
        self.E = w_router.shape[1]
        self.F = moe_w_gate.shape[-1]
        Bw, KVw, GN = self.Bw, self.KVw, self.GN
        self.s_q, self.s_z, self.s_xs, self.s_xr, self.s_gr, self.s_xp = (i * Bw for i in range(6))
        self.s_k = 6 * Bw
        self.s_v = self.s_k + KVw
        self.s_B = self.s_v + KVw
        self.s_C = self.s_B + GN
        self.s_dt = self.s_C + GN
        used = self.s_dt + _LANES
        self.slab_w = -(-used // 512) * 512
        assert 2 * self.SH <= _LANES and self.N == _LANES and ATT_HEAD_DIM == _LANES
        assert self.s_k % KVw == 0 and self.s_B % (2 * GN) == 0 and self.Bw % (2 * _LANES) == 0
        assert self.Lc % _ROW_BLOCK == 0 and self.Ll % _ROW_BLOCK == 0
        assert self.Lc % SSD_CHUNK == 0 and self.Ll % SSD_CHUNK == 0 and self.Ll % ATT_BLOCK == 0
        assert self.Bw // self.LH == _LANES and self.Bw % len(POOL_WINDOWS) == 0

    def seg_of_row(self, r0):
        return jnp.where(r0 < self.Tc, 0, 1 + (r0 - self.Tc) // self.Ll)

    def seq_pos(self, r0):
        is_ctx = r0 < self.Tc
        pos = jnp.where(is_ctx, r0 % self.Lc, (r0 - self.Tc) % self.Ll)
        return pos, jnp.where(is_ctx, self.Lc, self.Ll)


def _ada_body(c_ref, w_ref, b_ref, o_ref):
    s = _silu(c_ref[...]).astype(_BF16)
    o_ref[...] = jnp.dot(s, w_ref[...].astype(_BF16), preferred_element_type=_F32) + b_ref[...]


def _ada(cond, w_ada, b_ada):
    depth, D, n6 = w_ada.shape
    rows = cond.shape[0]
    tn = 512
    return _call(
        _ada_body, grid=(depth, n6 // tn),
        in_specs=[pl.BlockSpec((rows, D), lambda l, j: (0, 0)),
                  pl.BlockSpec((None, D, tn), lambda l, j: (l, 0, j)),
                  pl.BlockSpec((None, 1, tn), lambda l, j: (l, 0, j))],
        out_specs=pl.BlockSpec((None, rows, tn), lambda l, j: (l, 0, j)),
        out_shape=jax.ShapeDtypeStruct((depth, rows, n6), _F32), name="ada")(
            cond, w_ada, b_ada.reshape(depth, 1, n6))


def _mod_spec(dm, l, k, tm):
    return pl.BlockSpec((None, None, None, 1, dm.D), lambda i, *_: (l, dm.seg_of_row(i * tm), k, 0, 0))


def _modulate_body(xc_ref, xl_ref, sh_ref, sc_ref, o_ref, *, ctx_tiles):
    x = jnp.where(pl.program_id(0) < ctx_tiles, xc_ref[...], xl_ref[...])
    o_ref[...] = (x * (1.0 + sc_ref[...]) + sh_ref[...]).astype(o_ref.dtype)


def _modulate(dm, xc, xl, mod5, l):
    tm = 512
    return _call(
        functools.partial(_modulate_body, ctx_tiles=dm.Tc // tm), grid=(dm.T // tm,),
        in_specs=_stream_specs(dm, tm, dm.D) + [_mod_spec(dm, l, 0, tm), _mod_spec(dm, l, 1, tm)],
        out_specs=pl.BlockSpec((tm, dm.D), lambda i: (i, 0)),
        out_shape=jax.ShapeDtypeStruct((dm.T, dm.D), _BF16), name="modulate")(xc, xl, mod5, mod5)


def _mm_body(x_ref, w_ref, o_ref):
    o_ref[...] = jnp.dot(x_ref[...], w_ref[...], preferred_element_type=_F32).astype(o_ref.dtype)


def _pick_tile(n, candidates):
    for c in candidates:
        if n % c == 0:
            return c
    raise ValueError(f"no tile for {n}")


def _matmul(x, w, name, out_dtype=_F32):
    M, K = x.shape
    N = w.shape[1]
    tm = _pick_tile(M, (512, 256, 128))
    tn = _pick_tile(N, (1536, 1280, 1024, 768, 512))
    return _call(
        _mm_body, grid=(N // tn, M // tm),
        in_specs=[pl.BlockSpec((tm, K), lambda j, i: (i, 0)), pl.BlockSpec((K, tn), lambda j, i: (0, j))],
        out_specs=pl.BlockSpec((tm, tn), lambda j, i: (i, j)),
        out_shape=jax.ShapeDtypeStruct((M, N), out_dtype), name=name)(x, w)


def _rope_body(q_ref, k_ref, cos_ref, sin_ref, qo_ref, ko_ref):
    cos, sin = cos_ref[...], sin_ref[...]
    quarter = ATT_HEAD_DIM // 4
    lane = lax.broadcasted_iota(jnp.int32, cos.shape, 1)
    first = (lane % (2 * quarter)) < quarter

    def rot(x):
        partner = jnp.where(first, pltpu.roll(x, ATT_HEAD_DIM - quarter, 1), pltpu.roll(x, quarter, 1))
        return x * cos + partner * sin

    for ref, out in ((q_ref, qo_ref), (k_ref, ko_ref)):
        for h in range(ref.shape[1] // ATT_HEAD_DIM):
            sl = slice(h * ATT_HEAD_DIM, (h + 1) * ATT_HEAD_DIM)
            out[:, sl] = rot(ref[:, sl]).astype(out.dtype)


def _rope_tables(L):
    quarter = ATT_HEAD_DIM // 4
    t = jnp.arange(L)
    inv = ROPE_BASE ** (-jnp.arange(quarter, dtype=_F32) / quarter)
    ang_r = (t // GRID_W).astype(_F32)[:, None] * inv[None, :]
    ang_c = (t % GRID_W).astype(_F32)[:, None] * inv[None, :]
    cos = jnp.concatenate([jnp.cos(ang_r)] * 2 + [jnp.cos(ang_c)] * 2, -1)
    sin = jnp.concatenate([-jnp.sin(ang_r), jnp.sin(ang_r), -jnp.sin(ang_c), jnp.sin(ang_c)], -1)
    return cos, sin


def _rope(dm, slab, cos, sin):
    tr = _ROW_BLOCK
    r0, per_seq = dm.Tc // tr, dm.Ll // tr
    return _call(
        _rope_body, grid=(dm.Tl // tr,),
        in_specs=[pl.BlockSpec((tr, dm.Bw), lambda i: (r0 + i, dm.s_q // dm.Bw)),
                  pl.BlockSpec((tr, dm.KVw), lambda i: (r0 + i, dm.s_k // dm.KVw)),
                  pl.BlockSpec((tr, _LANES), lambda i: (i % per_seq, 0)),
                  pl.BlockSpec((tr, _LANES), lambda i: (i % per_seq, 0))],
        out_specs=[pl.BlockSpec((tr, dm.Bw), lambda i: (i, 0)), pl.BlockSpec((tr, dm.KVw), lambda i: (i, 0))],
        out_shape=[jax.ShapeDtypeStruct((dm.Tl, dm.Bw), _BF16), jax.ShapeDtypeStruct((dm.Tl, dm.KVw), _BF16)],
        name="rope")(slab, slab, cos, sin)


def _attend(sink_ref, q, k_all, v_all, bias, o_ref, kv, G):
    Q = q.shape[0]
    q_st = jnp.concatenate([q[:, g * ATT_HEAD_DIM:(g + 1) * ATT_HEAD_DIM] for g in range(G)], axis=0)
    s = lax.dot_general(q_st, k_all, (((1,), (1,)), ((), ())), preferred_element_type=_F32)
    s = s * (ATT_HEAD_DIM ** -0.5)
    if bias is not None:
        s = s + bias
    sk = jnp.concatenate([jnp.full((Q, 1), sink_ref[kv * G + g], _F32) for g in range(G)], axis=0)
    m = jnp.maximum(jnp.max(s, axis=-1, keepdims=True), sk)
    p = jnp.exp(s - m)
    denom = jnp.sum(p, axis=-1, keepdims=True) + jnp.exp(sk - m)
    o = jnp.dot(p.astype(_BF16), v_all, preferred_element_type=_F32) / denom
    for g in range(G):
        h = kv * G + g
        o_ref[:, h * ATT_HEAD_DIM:(h + 1) * ATT_HEAD_DIM] = o[g * Q:(g + 1) * Q].astype(o_ref.dtype)


def _attn_ctx_body(sink_ref, q_ref, k_ref, v_ref, _mix_in, o_ref, *, KV, G):
    for kv in range(KV):
        hs = slice(kv * ATT_HEAD_DIM, (kv + 1) * ATT_HEAD_DIM)
        q = q_ref[:, kv * G * ATT_HEAD_DIM:(kv + 1) * G * ATT_HEAD_DIM].astype(_BF16)
        _attend(sink_ref, q, k_ref[:, hs].astype(_BF16), v_ref[:, hs].astype(_BF16), None, o_ref, kv, G)


def _attn_ctx(dm, slab, sink, mix):
    Lc = dm.Lc
    return _call(
        functools.partial(_attn_ctx_body, KV=dm.KV, G=dm.G), grid=(dm.B,),
        in_specs=[pl.BlockSpec(memory_space=pltpu.SMEM),
                  pl.BlockSpec((Lc, dm.Bw), lambda b: (b, dm.s_q // dm.Bw)),
                  pl.BlockSpec((Lc, dm.KVw), lambda b: (b, dm.s_k // dm.KVw)),
                  pl.BlockSpec((Lc, dm.KVw), lambda b: (b, dm.s_v // dm.KVw)),
                  pl.BlockSpec(memory_space=pl.ANY)],
        out_specs=pl.BlockSpec((Lc, dm.Bw), lambda b: (b, _MIX_ATT)),
        out_shape=jax.ShapeDtypeStruct(mix.shape, mix.dtype), aliases={4: 0}, name="attn_ctx")(
            sink, slab, slab, slab, mix)


def _attn_lat_body(sink_ref, q_ref, kp_ref, kc_ref, kn_ref, vp_ref, vc_ref, vn_ref, ck_ref, cv_ref, _mix_in,
                   o_ref, *, KV, G, nb):
    n = pl.program_id(1)
    Q = ATT_BLOCK
    S = 3 * Q + ck_ref.shape[0]
    rows = G * Q
    c_i = lax.broadcasted_iota(jnp.int32, (rows, S), 1)
    qi = lax.broadcasted_iota(jnp.int32, (rows, S), 0) % Q
    bad_prev = c_i < jnp.maximum(qi, jnp.where(n == 0, Q, 0))
    bad_next = jnp.logical_and(c_i - 2 * Q > qi - jnp.where(n == nb - 1, Q, 0), c_i < 3 * Q)
    bias = jnp.where(jnp.logical_or(bad_prev, bad_next), _NEG, 0.0)
    for kv in range(KV):
        hs = slice(kv * ATT_HEAD_DIM, (kv + 1) * ATT_HEAD_DIM)
        k_all = jnp.concatenate([kp_ref[:, hs], kc_ref[:, hs], kn_ref[:, hs], ck_ref[:, hs].astype(_BF16)], axis=0)
        v_all = jnp.concatenate([vp_ref[:, hs].astype(_BF16), vc_ref[:, hs].astype(_BF16),
                                 vn_ref[:, hs].astype(_BF16), cv_ref[:, hs].astype(_BF16)], axis=0)
        q = q_ref[:, kv * G * ATT_HEAD_DIM:(kv + 1) * G * ATT_HEAD_DIM]
        _attend(sink_ref, q, k_all, v_all, bias, o_ref, kv, G)


def _attn_lat(dm, slab, qr, kr, cache_k4, cache_v4, sink, l, mix):
    Q = ATT_BLOCK
    nb = dm.Ll // Q
    r0 = dm.Tc // Q
    vcol = dm.s_v // dm.KVw
    prev = lambda n: jnp.maximum(n - 1, 0)
    nxt = lambda n: jnp.minimum(n + 1, nb - 1)
    cache_spec = pl.BlockSpec((None, None, dm.past, dm.KVw), lambda b, n: (b, l, 0, 0))
    return _call(
        functools.partial(_attn_lat_body, KV=dm.KV, G=dm.G, nb=nb), grid=(dm.DB, nb),
        in_specs=[pl.BlockSpec(memory_space=pltpu.SMEM),
                  pl.BlockSpec((Q, dm.Bw), lambda b, n: (b * nb + n, 0)),
                  pl.BlockSpec((Q, dm.KVw), lambda b, n: (b * nb + prev(n), 0)),
                  pl.BlockSpec((Q, dm.KVw), lambda b, n: (b * nb + n, 0)),
                  pl.BlockSpec((Q, dm.KVw), lambda b, n: (b * nb + nxt(n), 0)),
                  pl.BlockSpec((Q, dm.KVw), lambda b, n: (r0 + b * nb + prev(n), vcol)),
                  pl.BlockSpec((Q, dm.KVw), lambda b, n: (r0 + b * nb + n, vcol)),
                  pl.BlockSpec((Q, dm.KVw), lambda b, n: (r0 + b * nb + nxt(n), vcol)),
                  cache_spec, cache_spec, pl.BlockSpec(memory_space=pl.ANY)],
        out_specs=pl.BlockSpec((Q, dm.Bw), lambda b, n: (r0 + b * nb + n, _MIX_ATT)),
        out_shape=jax.ShapeDtypeStruct(mix.shape, mix.dtype), aliases={10: 0}, name="attn_lat")(
            sink, qr, kr, kr, kr, slab, slab, slab, cache_k4, cache_v4, mix)


def _halo_specs(dm, cw, colblk):
    R = _ROW_BLOCK
    per = R // _SUBLANES
    last = dm.T // _SUBLANES - 1
    return [pl.BlockSpec((R, cw), lambda i, *_: (i, colblk)),
            pl.BlockSpec((_SUBLANES, cw), lambda i, *_: (jnp.maximum(i * per - 1, 0), colblk)),
            pl.BlockSpec((_SUBLANES, cw), lambda i, *_: (jnp.minimum((i + 1) * per, last), colblk))]


def _fill_padded(dm, pad_ref, x_ref, prev_ref, next_ref):
    R = _ROW_BLOCK
    r0 = pl.program_id(0) * R
    pos, L = dm.seq_pos(r0)
    pad_ref[pl.ds(_SUBLANES, R), :] = x_ref[...].astype(_F32)
    pad_ref[pl.ds(0, _SUBLANES), :] = jnp.where(pos == 0, 0.0, prev_ref[...].astype(_F32))
    pad_ref[pl.ds(_SUBLANES + R, _SUBLANES), :] = jnp.where(pos + R == L, 0.0, next_ref[...].astype(_F32))
    return pos, L


def _conv_body(x_ref, prev_ref, next_ref, w_ref, b_ref, o_ref, pad_ref, *, dm, act):
    _fill_padded(dm, pad_ref, x_ref, prev_ref, next_ref)
    R = _ROW_BLOCK
    width = w_ref.shape[0]
    left = (width - 1) // 2
    out = pad_ref[pl.ds(_SUBLANES - left, R), :] * w_ref[0:1, :]
    for tap in range(1, width):
        out = out + pad_ref[pl.ds(_SUBLANES - left + tap, R), :] * w_ref[tap:tap + 1, :]
    out = out + b_ref[...]
    if act:
        out = _silu(out)
    o_ref[...] = out.astype(o_ref.dtype)


def _conv(dm, slab, col, cw, w, b, act, out_dtype):
    R = _ROW_BLOCK
    width = w.shape[0]
    return _call(
        functools.partial(_conv_body, dm=dm, act=act), grid=(dm.T // R,),
        in_specs=_halo_specs(dm, cw, col // cw) + [pl.BlockSpec((width, cw), lambda i: (0, 0)),
                                                   pl.BlockSpec((1, cw), lambda i: (0, 0))],
        out_specs=pl.BlockSpec((R, cw), lambda i: (i, 0)),
        out_shape=jax.ShapeDtypeStruct((dm.T, cw), out_dtype),
        scratch=[pltpu.VMEM((R + 2 * _SUBLANES, cw), _F32)], name="conv")(slab, slab, slab, w, b.reshape(1, cw))


def _pool_body(x_ref, prev_ref, next_ref, w_ref, b_ref, sc_ref, _mix_in, o_ref, pad_ref, *, dm):
    pos, L = _fill_padded(dm, pad_ref, x_ref, prev_ref, next_ref)
    R = _ROW_BLOCK
    gd = w_ref.shape[1]
    t = pos + lax.broadcasted_iota(jnp.int32, (R, 1), 0)
    for gi, win in enumerate(POOL_WINDOWS):
        cs = slice(gi * gd, (gi + 1) * gd)
        half = win // 2
        total = pad_ref[pl.ds(_SUBLANES - half, R), cs]
        for k in range(1, win):
            total = total + pad_ref[pl.ds(_SUBLANES - half + k, R), cs]
        cnt = (jnp.minimum(t + (win - half), L) - jnp.maximum(t - half, 0)).astype(_F32)
        pooled = total / cnt - pad_ref[pl.ds(_SUBLANES, R), cs]
        y = jnp.dot(pooled.astype(_BF16), w_ref[gi], preferred_element_type=_F32) + b_ref[:, cs]
        o_ref[:, cs] = (y * sc_ref[:, cs]).astype(o_ref.dtype)


def _pool(dm, slab, w, b, scale, mix):
    R = _ROW_BLOCK
    ng, gd, _ = w.shape
    assert max(POOL_WINDOWS) // 2 <= _SUBLANES
    row = pl.BlockSpec((1, dm.Bw), lambda i: (0, 0))
    return _call(
        functools.partial(_pool_body, dm=dm), grid=(dm.T // R,),
        in_specs=_halo_specs(dm, dm.Bw, dm.s_xp // dm.Bw) + [pl.BlockSpec((ng, gd, gd), lambda i: (0, 0, 0)), row, row,
                                                              pl.BlockSpec(memory_space=pl.ANY)],
        out_specs=pl.BlockSpec((R, dm.Bw), lambda i: (i, _MIX_POOL)),
        out_shape=jax.ShapeDtypeStruct(mix.shape, mix.dtype), aliases={6: 0},
        scratch=[pltpu.VMEM((R + 2 * _SUBLANES, dm.Bw), _F32)], name="pool")(
            slab, slab, slab, w.astype(_BF16), b.reshape(1, -1), scale.reshape(1, -1), mix)


def _chunk_pos(dm, ci):
    ncc, ncl = dm.Lc // SSD_CHUNK, dm.Ll // SSD_CHUNK
    nct = dm.Tc // SSD_CHUNK
    is_ctx = ci < nct
    pos = jnp.where(is_ctx, ci % ncc, (ci - nct) % ncl)
    nseq = jnp.where(is_ctx, ncc, ncl)
    lat = jnp.clip((ci - nct) // ncl, 0, dm.DB - 1)
    return is_ctx, pos, nseq, lat


def _spread_heads(v, xm):
    hi = v.astype(_BF16)
    lo = (v - hi.astype(_F32)).astype(_BF16)
    return jnp.dot(hi, xm, preferred_element_type=_F32) + jnp.dot(lo, xm, preferred_element_type=_F32)


def _ssd_dir(phase, dm, ci, rev, x_ref, bc_ref, dt_ref, bias_ref, a_ref, xm_ref, h0_ref, y_ref, s_ref, h_scr):
    Q, N, SH, P = SSD_CHUNK, dm.N, dm.SH, SSD_HEAD_DIM
    hg = SH // SSD_GROUPS
    gw = hg * P
    is_ctx, pos, nseq, _ = _chunk_pos(dm, ci)
    first = pos == (nseq - 1 if rev else 0)
    last = pos == (0 if rev else nseq - 1)

    if phase == "init":
        @pl.when(jnp.logical_and(first, is_ctx))
        def _():
            h_scr[...] = jnp.zeros_like(h_scr)

        @pl.when(jnp.logical_and(first, jnp.logical_not(is_ctx)))
        def _():
            for k in range(dm.Bw // _LANES):
                h_scr[:, k * _LANES:(k + 1) * _LANES] = h0_ref[k * _LANES:(k + 1) * _LANES, :].T
        return

    if phase == "final":
        @pl.when(jnp.logical_and(last, is_ctx))
        def _():
            for k in range(dm.Bw // _LANES):
                s_ref[k * _LANES:(k + 1) * _LANES, :] = h_scr[:, k * _LANES:(k + 1) * _LANES].T
        return

    lane0 = (SH if rev else 0)
    dt = _softplus(dt_ref[...] + bias_ref[...])
    a = dt * a_ref[...]
    ri = lax.broadcasted_iota(jnp.int32, (Q, Q), 0)
    cj = lax.broadcasted_iota(jnp.int32, (Q, Q), 1)
    tri = (ri <= cj) if rev else (ri >= cj)
    cum = jnp.dot(tri.astype(_F32), a, preferred_element_type=_F32, precision=lax.Precision.HIGHEST)
    cum_t, dt_t = cum.T, dt.T
    ecum = jnp.exp(cum)
    end = 0 if rev else Q - 1
    wend = jnp.exp(cum[end:end + 1, :] - cum) * dt
    lane = lax.broadcasted_iota(jnp.int32, (Q, _LANES), 1)
    low = lane < P
    xm = xm_ref[1 if rev else 0]
    e_all, w_all = _spread_heads(ecum, xm), _spread_heads(wend, xm)

    for g in range(SSD_GROUPS):
        b_g = bc_ref[:, g * N:(g + 1) * N]
        c_g = bc_ref[:, dm.GN + g * N:dm.GN + (g + 1) * N]
        cb = lax.dot_general(c_g, b_g, (((1,), (1,)), ((), ())), preferred_element_type=_F32)
        gs = slice(g * gw, (g + 1) * gw)
        y_diag = []
        for p in range(hg // 2):
            xs = slice(g * gw + p * _LANES, g * gw + (p + 1) * _LANES)
            xp = x_ref[:, xs].astype(_BF16)
            ys = []
            for s in range(2):
                ln = lane0 + g * hg + 2 * p + s
                seg = cum[:, ln:ln + 1] - cum_t[ln:ln + 1, :]
                m = cb * jnp.exp(jnp.where(tri, seg, _NEG)) * dt_t[ln:ln + 1, :]
                ys.append(jnp.dot(m.astype(_BF16), xp, preferred_element_type=_F32))
            ln = lane0 + g * hg + 2 * p
            y_diag.append(jnp.where(low, ys[0], ys[1]))
        y_diag = jnp.concatenate(y_diag, axis=1)
        e_exp, w_exp = e_all[:, gs], w_all[:, gs]
        h_in = h_scr[:, gs]
        y_off = jnp.dot(c_g, h_in.astype(_BF16), preferred_element_type=_F32) * e_exp
        y_ref[:, gs] = y_diag + y_off
        xw = (x_ref[:, gs] * w_exp).astype(_BF16)
        st = lax.dot_general(b_g, xw, (((0,), (0,)), ((), ())), preferred_element_type=_F32)
        h_scr[:, gs] = h_in * e_exp[end:end + 1, :] + st


def _ssd_body(xf_ref, bcf_ref, dtf_ref, xb_ref, bcb_ref, dtb_ref, bias_ref, a_ref, xm_ref, h0f_ref, h0b_ref,
              yf_ref, yb_ref, sf_ref, sb_ref, hf_scr, hb_scr, *, dm):
    i = pl.program_id(0)
    n = pl.num_programs(0)
    for phase in ("init", "compute", "final"):
        _ssd_dir(phase, dm, i, False, xf_ref, bcf_ref, dtf_ref, bias_ref, a_ref, xm_ref, h0f_ref, yf_ref, sf_ref,
                 hf_scr)
        _ssd_dir(phase, dm, n - 1 - i, True, xb_ref, bcb_ref, dtb_ref, bias_ref, a_ref, xm_ref, h0b_ref, yb_ref,
                 sb_ref, hb_scr)


def _ssd(dm, slab, xs, bc, dt_bias, a_neg, state_ssd5, l):
    Q = SSD_CHUNK
    nchunks = dm.T // Q
    ncc = dm.Lc // Q
    HP = dm.SH * SSD_HEAD_DIM
    dcol = dm.s_dt // _LANES
    fwd = lambda i: i
    bwd = lambda i: nchunks - 1 - i

    def seq_c(ci):
        return jnp.minimum(ci // ncc, dm.B - 1)

    def lat_of(ci):
        return _chunk_pos(dm, ci)[3]

    def specs(order):
        return [pl.BlockSpec((Q, dm.Bw), lambda i: (order(i), 0)),
                pl.BlockSpec((Q, 2 * dm.GN), lambda i: (order(i), 0)),
                pl.BlockSpec((Q, _LANES), lambda i: (order(i), dcol))]

    row = pl.BlockSpec((1, _LANES), lambda i: (0, 0))
    head_of_col = jnp.arange(dm.Bw) // SSD_HEAD_DIM
    spread = jnp.stack([(jnp.arange(_LANES)[:, None] == (d * dm.SH + head_of_col)[None, :]) for d in range(2)])
    spread = spread.astype(_BF16)
    h0 = lambda order, d: pl.BlockSpec((None, None, None, HP, dm.N), lambda i: (lat_of(order(i)), l, d, 0, 0))
    st = lambda order: pl.BlockSpec((None, HP, dm.N), lambda i: (seq_c(order(i)), 0, 0))
    ysd = jax.ShapeDtypeStruct((dm.T, dm.Bw), _F32)
    ssd = jax.ShapeDtypeStruct((dm.B, HP, dm.N), _F32)
    return _call(
        functools.partial(_ssd_body, dm=dm), grid=(nchunks,),
        in_specs=specs(fwd) + specs(bwd) + [row, row, pl.BlockSpec((2, _LANES, dm.Bw), lambda i: (0, 0, 0)),
                                            h0(fwd, 0), h0(bwd, 1)],
        out_specs=[pl.BlockSpec((Q, dm.Bw), lambda i: (fwd(i), 0)), pl.BlockSpec((Q, dm.Bw), lambda i: (bwd(i), 0)),
                   st(fwd), st(bwd)],
        out_shape=[ysd, ysd, ssd, ssd],
        scratch=[pltpu.VMEM((dm.N, dm.Bw), _F32), pltpu.VMEM((dm.N, dm.Bw), _F32)], name="ssd")(
            xs, bc, slab, xs, bc, slab, dt_bias, a_neg, spread, state_ssd5, state_ssd5)


def _lru_gates(dm, d, x_ref, wr_ref, wi_ref, br_ref, bi_ref, lam_ref, a_scr, u_scr):
    sp = _softplus(-lam_ref[d:d + 1, :])
    for h in range(dm.LH):
        hs = slice(h * _LANES, (h + 1) * _LANES)
        x = x_ref[:, hs]
        xb = x.astype(_BF16)
        r = jax.nn.sigmoid(jnp.dot(xb, wr_ref[d, h], preferred_element_type=_F32) + br_ref[d:d + 1, hs])
        ig = jax.nn.sigmoid(jnp.dot(xb, wi_ref[d, h], preferred_element_type=_F32) + bi_ref[d:d + 1, hs])
        log_a = -LRU_C * r * sp[:, hs]
        a = jnp.exp(log_a)
        a_scr[:, hs] = a
        u_scr[:, hs] = jnp.sqrt(-jnp.tanh(log_a) * (a * a + 1.0)) * (ig * x)


def _lru_body(xf_ref, xb_ref, wr_ref, wi_ref, br_ref, bi_ref, lam_ref, h0f_ref, h0b_ref,
              hf_ref, hb_ref, finf_ref, finb_ref, af, uf, ab, ub, carry, *, dm):
    R = _ROW_BLOCK
    i = pl.program_id(0)
    n = pl.num_programs(0)
    _lru_gates(dm, 0, xf_ref, wr_ref, wi_ref, br_ref, bi_ref, lam_ref, af, uf)
    _lru_gates(dm, 1, xb_ref, wr_ref, wi_ref, br_ref, bi_ref, lam_ref, ab, ub)

    posf, Lf = dm.seq_pos(i * R)
    posb, Lb = dm.seq_pos((n - 1 - i) * R)
    ctx_f = i * R < dm.Tc
    ctx_b = (n - 1 - i) * R < dm.Tc
    for d, start, is_ctx, h0_ref in ((0, posf == 0, ctx_f, h0f_ref), (1, posb + R == Lb, ctx_b, h0b_ref)):
        @pl.when(jnp.logical_and(start, is_ctx))
        def _():
            carry[d:d + 1, :] = jnp.zeros((1, dm.Bw), _F32)

        @pl.when(jnp.logical_and(start, jnp.logical_not(is_ctx)))
        def _():
            carry[d:d + 1, :] = h0_ref[d:d + 1, :]

    def tile(t, hs):
        hf, hb = hs
        base_f = pl.multiple_of(t * _SUBLANES, _SUBLANES)
        base_b = pl.multiple_of(R - _SUBLANES - t * _SUBLANES, _SUBLANES)
        for r in range(_SUBLANES):
            rf = base_f + r
            hf = af[pl.ds(rf, 1), :] * hf + uf[pl.ds(rf, 1), :]
            hf_ref[pl.ds(rf, 1), :] = hf
            rb = base_b + (_SUBLANES - 1 - r)
            hb = ab[pl.ds(rb, 1), :] * hb + ub[pl.ds(rb, 1), :]
            hb_ref[pl.ds(rb, 1), :] = hb
        return hf, hb

    hf, hb = lax.fori_loop(0, R // _SUBLANES, tile, (carry[0:1, :], carry[1:2, :]))
    carry[0:1, :] = hf
    carry[1:2, :] = hb

    @pl.when(jnp.logical_and(posf + R == Lf, ctx_f))
    def _():
        finf_ref[...] = hf

    @pl.when(jnp.logical_and(posb == 0, ctx_b))
    def _():
        finb_ref[...] = hb


def _lru(dm, xf, w_r, w_i, b_r, b_i, lam, state_lru, l):
    R = _ROW_BLOCK
    nblk = dm.T // R
    fwd = lambda i: i
    bwd = lambda i: nblk - 1 - i
    wspec = pl.BlockSpec((2, dm.LH, _LANES, _LANES), lambda i: (0, 0, 0, 0))
    vspec = pl.BlockSpec((2, dm.Bw), lambda i: (0, 0))
    lat_seq = lambda blk: jnp.clip((blk * R - dm.Tc) // dm.Ll, 0, dm.DB - 1)
    ctx_seq = lambda blk: jnp.minimum(blk * R // dm.Lc, dm.B - 1)
    h0 = lambda order: pl.BlockSpec((None, None, 2, dm.Bw), lambda i: (lat_seq(order(i)), l, 0, 0))
    fin = lambda order: pl.BlockSpec((None, 1, dm.Bw), lambda i: (ctx_seq(order(i)), 0, 0))
    hsd = jax.ShapeDtypeStruct((dm.T, dm.Bw), _F32)
    fsd = jax.ShapeDtypeStruct((dm.B, 1, dm.Bw), _F32)
    return _call(
        functools.partial(_lru_body, dm=dm), grid=(nblk,),
        in_specs=[pl.BlockSpec((R, dm.Bw), lambda i: (fwd(i), 0)), pl.BlockSpec((R, dm.Bw), lambda i: (bwd(i), 0)),
                  wspec, wspec, vspec, vspec, vspec, h0(fwd), h0(bwd)],
        out_specs=[pl.BlockSpec((R, dm.Bw), lambda i: (fwd(i), 0)), pl.BlockSpec((R, dm.Bw), lambda i: (bwd(i), 0)),
                   fin(fwd), fin(bwd)],
        out_shape=[hsd, hsd, fsd, fsd],
        scratch=[pltpu.VMEM((R, dm.Bw), _F32)] * 4 + [pltpu.VMEM((2, dm.Bw), _F32)], name="lru")(
            xf, xf, w_r, w_i, b_r, b_i, lam, state_lru, state_lru)


def _finish_body(yf_ref, yb_ref, xs_ref, z_ref, d_ref, ng_ref, hf_ref, hb_ref, gr_ref, _mix_in, o_ref):
    y = (yf_ref[...] + yb_ref[...] + d_ref[...] * xs_ref[...]) * _silu(z_ref[...])
    gw = y.shape[1] // SSD_GROUPS
    for g in range(SSD_GROUPS):
        gs = slice(g * gw, (g + 1) * gw)
        yg = y[:, gs]
        yg = yg * lax.rsqrt(jnp.mean(yg * yg, axis=-1, keepdims=True) + EPS)
        o_ref[:, gs] = (yg * ng_ref[:, gs]).astype(o_ref.dtype)
    bw = y.shape[1]
    o_ref[:, bw:] = ((hf_ref[...] + hb_ref[...]) * jax.nn.gelu(gr_ref[...], approximate=True)).astype(o_ref.dtype)


def _finish(dm, slab, yf, yb, xs, d_exp, norm_g, hf, hb, mix):
    R = _ROW_BLOCK
    blk = pl.BlockSpec((R, dm.Bw), lambda i: (i, 0))
    row = pl.BlockSpec((1, dm.Bw), lambda i: (0, 0))
    return _call(
        _finish_body, grid=(dm.T // R,),
        in_specs=[blk, blk, blk, pl.BlockSpec((R, dm.Bw), lambda i: (i, dm.s_z // dm.Bw)), row, row,
                  blk, blk, pl.BlockSpec((R, dm.Bw), lambda i: (i, dm.s_gr // dm.Bw)),
                  pl.BlockSpec(memory_space=pl.ANY)],
        out_specs=pl.BlockSpec((R, 2 * dm.Bw), lambda i: (i, _MIX_SSD_LRU)),
        out_shape=jax.ShapeDtypeStruct(mix.shape, mix.dtype), aliases={9: 0}, name="finish")(
            yf, yb, xs, slab, d_exp, norm_g.reshape(1, -1), hf, hb, slab, mix)


def _layer_norm(r, g, b):
    mu = jnp.mean(r, axis=-1, keepdims=True)
    c = r - mu
    var = jnp.mean(c * c, axis=-1, keepdims=True)
    return c * lax.rsqrt(var + EPS) * g + b


def _split_bf16(x):
    hi = x.astype(_BF16)
    return hi, (x - hi.astype(_F32)).astype(_BF16)


_HI16 = 0xFFFF0000


def _pack_bf16_pair(a, b):
    ua = pltpu.bitcast(a.astype(_BF16).astype(_F32), jnp.uint32)
    ub = pltpu.bitcast(b.astype(_BF16).astype(_F32), jnp.uint32)
    return (ua >> 16) | (ub & jnp.uint32(_HI16))


def _unpack_bf16_pair(p):
    return pltpu.bitcast(p << 16, _F32), pltpu.bitcast(p & jnp.uint32(_HI16), _F32)


def _norm_router_body(mix_ref, xc_ref, xl_ref, g1_ref, lg_ref, lb_ref, sc_ref, sh_ref, wr_ref, x1_ref, h_ref, s_ref,
                      *, alpha, ctx_tiles):
    x = jnp.where(pl.program_id(0) < ctx_tiles, xc_ref[...], xl_ref[...])
    x1 = _layer_norm(alpha * x + g1_ref[...] * mix_ref[...], lg_ref[...], lb_ref[...])
    x1_ref[...] = x1
    h = x1 * (1.0 + sc_ref[...]) + sh_ref[...]
    half = h.shape[1] // 2
    h_ref[...] = _pack_bf16_pair(h[:, :half], h[:, half:])
    hh, hl = _split_bf16(h)
    wh, wl = _split_bf16(wr_ref[...])
    logits = (jnp.dot(hh, wh, preferred_element_type=_F32) + jnp.dot(hh, wl, preferred_element_type=_F32)
              + jnp.dot(hl, wh, preferred_element_type=_F32))
    s_ref[...] = jax.nn.sigmoid(logits)


def _stream_specs(dm, tm, width):
    ctx_tiles = dm.Tc // tm
    return [pl.BlockSpec((tm, width), lambda i, *_: (jnp.minimum(i, ctx_tiles - 1), 0)),
            pl.BlockSpec((tm, width), lambda i, *_: (jnp.maximum(i - ctx_tiles, 0), 0))]


def _norm_router(dm, mix, xc, xl, mod5, ln_g, ln_b, w_router_pad, l, alpha):
    tm = 256
    D = dm.D
    tile = pl.BlockSpec((tm, D), lambda i: (i, 0))
    row = pl.BlockSpec((1, D), lambda i: (0, 0))
    return _call(
        functools.partial(_norm_router_body, alpha=alpha, ctx_tiles=dm.Tc // tm), grid=(dm.T // tm,),
        in_specs=[tile] + _stream_specs(dm, tm, D) + [
            _mod_spec(dm, l, 2, tm), row, row, _mod_spec(dm, l, 4, tm), _mod_spec(dm, l, 3, tm),
            pl.BlockSpec((D, _LANES), lambda i: (0, 0))],
        out_specs=[tile, pl.BlockSpec((tm, D // 2), lambda i: (i, 0)), pl.BlockSpec((tm, _LANES), lambda i: (i, 0))],
        out_shape=[jax.ShapeDtypeStruct((dm.T, D), _F32), jax.ShapeDtypeStruct((dm.T, D // 2), jnp.uint32),
                   jax.ShapeDtypeStruct((dm.T, _LANES), _F32)], name="norm_router")(
            mix, xc, xl, mod5, ln_g.reshape(1, D), ln_b.reshape(1, D), mod5, mod5, w_router_pad)


def _top2(v):
    n = v.shape[-1]
    i1 = jnp.argmax(v, axis=-1)
    m1 = jnp.max(v, axis=-1)
    rest = jnp.where(jnp.arange(n) == i1[..., None], -jnp.inf, v)
    return (m1, jnp.max(rest, axis=-1)), (i1, jnp.argmax(rest, axis=-1))


def _route(s, router_bias, tm):
    T, E = s.shape
    per = E // N_EXPERT_GROUPS
    assert TOP_K == 2
    sel = s + router_bias.astype(_F32)
    grp_score = sum(_top2(sel.reshape(T, N_EXPERT_GROUPS, per))[0])
    grp = jnp.argmax(grp_score, axis=-1)
    in_grp = (jnp.arange(E) // per)[None, :] == grp[:, None]
    idx = jnp.stack(_top2(jnp.where(in_grp, sel, -jnp.inf))[1], axis=-1)
    wts = jnp.take_along_axis(s, idx, axis=-1)
    wts = wts / jnp.sum(wts, -1, keepdims=True)
    A = T * TOP_K
    onehot = (idx.reshape(A, 1) == jnp.arange(E, dtype=idx.dtype)[None, :]).astype(jnp.int32)
    csum = jnp.cumsum(onehot, axis=0)
    counts = csum[-1]
    padded = (counts + tm - 1) // tm * tm
    pend = jnp.cumsum(padded)
    pstart = pend - padded
    pos = jnp.sum(onehot * (csum - 1 + pstart[None, :]), axis=1).astype(jnp.int32)
    n_blocks = -(-A // tm) + E
    src_tok = jnp.zeros((n_blocks * tm,), jnp.int32).at[pos].set(jnp.arange(A, dtype=jnp.int32) // TOP_K)
    blk0 = jnp.arange(n_blocks, dtype=jnp.int32) * tm
    block_e = jnp.minimum(jnp.sum((blk0[:, None] >= pend[None, :]).astype(jnp.int32), axis=1), E - 1)
    n_used = (pend[-1] // tm).astype(jnp.int32).reshape(1)
    n_valid = jnp.clip(counts[block_e] - (blk0 - pstart[block_e]), 0, tm)
    n_valid = -(-n_valid // _DMA_UNROLL) * _DMA_UNROLL
    n_valid = jnp.where(blk0 < pend[-1], n_valid, 0).astype(jnp.int32)
    return src_tok, block_e.astype(jnp.int32), n_used, n_valid, pos, wts


def _row_copy(src_hbm, row, dst, dst_row, sem):
    return pltpu.make_async_copy(src_hbm.at[pl.ds(row, 1)], dst.at[pl.ds(dst_row, 1)], sem)


_DMA_UNROLL = 8


def _gather_body(src_ref, nv_ref, h_hbm, o_ref, buf, sem):
    tm = o_ref.shape[0]
    i = pl.program_id(0)
    n = pl.num_programs(0)

    def issue(step, slot):
        def body(g, c):
            for u in range(_DMA_UNROLL):
                r = g * _DMA_UNROLL + u
                _row_copy(h_hbm, src_ref[step * tm + r], buf.at[slot], r, sem.at[slot]).start()
            return c
        lax.fori_loop(0, nv_ref[step] // _DMA_UNROLL, body, 0)

    @pl.when(i == 0)
    def _():
        buf[...] = jnp.zeros_like(buf)
        issue(0, 0)

    @pl.when(i + 1 < n)
    def _():
        issue(i + 1, (i + 1) % 2)

    slot = i % 2

    def wait(g, c):
        for u in range(_DMA_UNROLL):
            _row_copy(h_hbm, 0, buf.at[slot], g * _DMA_UNROLL + u, sem.at[slot]).wait()
        return c
    lax.fori_loop(0, nv_ref[i] // _DMA_UNROLL, wait, 0)
    half = buf.shape[2]
    lo, hi = _unpack_bf16_pair(buf[slot])
    o_ref[:, :half] = lo.astype(o_ref.dtype)
    o_ref[:, half:] = hi.astype(o_ref.dtype)


def _gather(h_packed, src_tok, n_valid, tm):
    half = h_packed.shape[1]
    n_blocks = src_tok.shape[0] // tm
    return _call(
        _gather_body, grid=(n_blocks,), nprefetch=2,
        in_specs=[pl.BlockSpec(memory_space=pl.ANY)],
        out_specs=pl.BlockSpec((tm, 2 * half), lambda i, src, nv: (i, 0)),
        out_shape=jax.ShapeDtypeStruct((n_blocks * tm, 2 * half), _BF16),
        scratch=[pltpu.VMEM((2, tm, half), jnp.uint32), pltpu.SemaphoreType.DMA((2,))], name="moe_gather")(
            src_tok, n_valid, h_packed)


def _expert_changed(be_ref, m):
    return jnp.logical_or(m == 0, be_ref[m] != be_ref[jnp.maximum(m - 1, 0)])


def _gateup_body(be_ref, nu_ref, x_ref, wg_ref, wu_ref, g_ref, wg_s, wu_s):
    m = pl.program_id(1)

    @pl.when(_expert_changed(be_ref, m))
    def _():
        wg_s[...] = wg_ref[...].astype(_BF16)
        wu_s[...] = wu_ref[...].astype(_BF16)

    @pl.when(m < nu_ref[0])
    def _():
        x = x_ref[...]
        gate = jnp.dot(x, wg_s[...], preferred_element_type=_F32)
        up = jnp.dot(x, wu_s[...], preferred_element_type=_F32)
        g_ref[...] = (_silu(gate) * up).astype(g_ref.dtype)

    @pl.when(m >= nu_ref[0])
    def _():
        g_ref[...] = jnp.zeros_like(g_ref)


def _gateup(xs, block_e, n_used, w_gate, w_up, l, tm):
    A_pad, D = xs.shape
    F = w_gate.shape[-1]
    tf = _pick_tile(F, (512, 256, 128))
    wspec = pl.BlockSpec((None, None, D, tf), lambda j, m, be, nu: (l, be[m], 0, j))
    return _call(
        _gateup_body, grid=(F // tf, A_pad // tm), nprefetch=2,
        in_specs=[pl.BlockSpec((tm, D), lambda j, m, be, nu: (m, 0)), wspec, wspec],
        out_specs=pl.BlockSpec((tm, tf), lambda j, m, be, nu: (m, j)),
        out_shape=jax.ShapeDtypeStruct((A_pad, F), _BF16),
        scratch=[pltpu.VMEM((D, tf), _BF16)] * 2, name="moe_gateup")(block_e, n_used, xs, w_gate, w_up)


def _down_body(be_ref, nu_ref, g_ref, wd_ref, y_ref, wd_s):
    m = pl.program_id(1)

    @pl.when(_expert_changed(be_ref, m))
    def _():
        wd_s[...] = wd_ref[...].astype(_BF16)

    @pl.when(m < nu_ref[0])
    def _():
        y = jnp.dot(g_ref[...], wd_s[...], preferred_element_type=_F32)
        half = y.shape[1] // 2
        y_ref[...] = _pack_bf16_pair(y[:, :half], y[:, half:])

    @pl.when(m >= nu_ref[0])
    def _():
        y_ref[...] = jnp.zeros_like(y_ref)


def _down_tile(D):
    return _pick_tile(D, (2048, 1024, 512, 256))


def _down(g, block_e, n_used, w_down, l, tm):
    A_pad, F = g.shape
    D = w_down.shape[-1]
    tn = _down_tile(D)
    return _call(
        _down_body, grid=(D // tn, A_pad // tm), nprefetch=2,
        in_specs=[pl.BlockSpec((tm, F), lambda j, m, be, nu: (m, 0)),
                  pl.BlockSpec((None, None, F, tn), lambda j, m, be, nu: (l, be[m], 0, j))],
        out_specs=pl.BlockSpec((tm, tn // 2), lambda j, m, be, nu: (m, j)),
        out_shape=jax.ShapeDtypeStruct((A_pad, D // 2), jnp.uint32),
        scratch=[pltpu.VMEM((F, tn), _BF16)], name="moe_down")(block_e, n_used, g, w_down)


def _combine_body(pos_ref, y_hbm, x1_ref, w_ref, g2_ref, lg_ref, lb_ref, *rest, alpha, ctx_tiles, final, tn):
    if final:
        xp_ref, xs_ref, x2_scr, buf, sem = rest
    else:
        sc_ref, sh_ref, xp_ref, xs_ref, hn_ref, x2_scr, buf, sem = rest
    tm = x1_ref.shape[0]
    i = pl.program_id(0)
    n = pl.num_programs(0)
    slot = i % 2
    nslot = (i + 1) % 2
    nstep = jnp.minimum(i + 1, n - 1)

    def start(step, sl, r):
        for k in range(TOP_K):
            _row_copy(y_hbm, pos_ref[(step * tm + r) * TOP_K + k], buf.at[sl, k], r, sem.at[sl]).start()

    def wait_all(sl):
        def wait(g, c):
            for u in range(_DMA_UNROLL // TOP_K):
                for k in range(TOP_K):
                    _row_copy(y_hbm, 0, buf.at[sl, k], g * (_DMA_UNROLL // TOP_K) + u, sem.at[sl]).wait()
            return c
        lax.fori_loop(0, tm * TOP_K // _DMA_UNROLL, wait, 0)

    @pl.when(i == 0)
    def _():
        def body(g, c):
            for u in range(_DMA_UNROLL // TOP_K):
                start(0, 0, g * (_DMA_UNROLL // TOP_K) + u)
            return c
        lax.fori_loop(0, tm * TOP_K // _DMA_UNROLL, body, 0)

    wait_all(slot)
    chunk = 32
    for c0 in range(0, tm, chunk):
        for r in range(c0, c0 + chunk):
            start(nstep, nslot, r)
        rows = slice(c0, c0 + chunk)
        w = w_ref[rows, :]
        ffn = None
        for k in range(TOP_K):
            parts = []
            for j in range(2 * buf.shape[3] // tn):
                parts += _unpack_bf16_pair(buf[slot, k, rows, j * (tn // 2):(j + 1) * (tn // 2)])
            term = w[:, k:k + 1] * jnp.concatenate(parts, axis=1)
            ffn = term if ffn is None else ffn + term
        x2 = _layer_norm(alpha * x1_ref[rows, :] + g2_ref[...] * ffn, lg_ref[...], lb_ref[...])
        x2_scr[rows, :] = x2
        if not final:
            hn_ref[rows, :] = (x2 * (1.0 + sc_ref[...]) + sh_ref[...]).astype(hn_ref.dtype)

    @pl.when(i < ctx_tiles)
    def _():
        xp_ref[...] = x2_scr[...]

    @pl.when(i >= ctx_tiles)
    def _():
        xs_ref[...] = x2_scr[...]

    @pl.when(i == n - 1)
    def _():
        wait_all(nslot)


def _combine(dm, y_packed, pos, wts, x1, mod5, ln_g, ln_b, l, alpha):
    tm = 256
    D = dm.D
    final = l == dm.depth - 1
    tile = pl.BlockSpec((tm, D), lambda i, p: (i, 0))
    row = pl.BlockSpec((1, D), lambda i, p: (0, 0))
    in_specs = [pl.BlockSpec(memory_space=pl.ANY), tile, pl.BlockSpec((tm, TOP_K), lambda i, p: (i, 0)),
                _mod_spec(dm, l, 5, tm), row, row]
    args = [pos, y_packed, x1, wts, mod5, ln_g.reshape(1, D), ln_b.reshape(1, D)]
    out_specs = _stream_specs(dm, tm, D)
    out_shape = [jax.ShapeDtypeStruct((dm.Tc, D), _F32), jax.ShapeDtypeStruct((dm.Tl, D), _F32)]
    if not final:
        in_specs += [_mod_spec(dm, l + 1, 1, tm), _mod_spec(dm, l + 1, 0, tm)]
        args += [mod5, mod5]
        out_specs = out_specs + [tile]
        out_shape = out_shape + [jax.ShapeDtypeStruct((dm.T, D), _BF16)]
    return _call(
        functools.partial(_combine_body, alpha=alpha, ctx_tiles=dm.Tc // tm, final=final, tn=_down_tile(D)),
        grid=(dm.T // tm,), nprefetch=1, in_specs=in_specs, out_specs=out_specs, out_shape=out_shape,
        scratch=[pltpu.VMEM((tm, D), _F32), pltpu.VMEM((2, TOP_K, tm, D // 2), jnp.uint32),
                 pltpu.SemaphoreType.DMA((2,))],
        name="moe_combine")(*args)


def kernel(x_prompt, x_sample, cache_k, cache_v, state_ssd, state_lru, c, c_ctx, w_ada, b_ada, w_in, w_out, attn_sink, ssd_conv_w, ssd_conv_b, ssd_a_log, ssd_dt_bias, ssd_d, ssd_norm_g, lru_conv_w, lru_conv_b, lru_lambda, lru_w_r, lru_b_r, lru_w_i, lru_b_i, pool_w, pool_b, pool_scale, ln1_g, ln1_b, ln2_g, ln2_b, w_router, router_bias, moe_w_gate, moe_w_up, moe_w_down):
    dm = _Dims(x_prompt, x_sample, cache_k, state_ssd, lru_w_r, w_router, moe_w_gate, w_ada)
    depth, D, Bw, KVw, GN, SH = dm.depth, dm.D, dm.Bw, dm.KVw, dm.GN, dm.SH
    alpha = (2 * depth) ** 0.25

    rows = -(-(1 + dm.DB) // _SUBLANES) * _SUBLANES
    cond = jnp.zeros((rows, D), _F32).at[0].set(c_ctx).at[1:1 + dm.DB].set(c)
    mod5 = _ada(cond, w_ada, b_ada).reshape(depth, rows, 6, 1, D)

    xc, xl = x_prompt.reshape(dm.Tc, D), x_sample.reshape(dm.Tl, D)
    h = _modulate(dm, xc, xl, mod5, 0)

    cos, sin = _rope_tables(dm.Ll)
    cache_k4 = cache_k.reshape(dm.DB, depth, dm.past, KVw)
    cache_v4 = cache_v.reshape(dm.DB, depth, dm.past, KVw)
    state_ssd5 = state_ssd.reshape(dm.DB, depth, 2, SH * SSD_HEAD_DIM, dm.N)
    w_router_pad = jnp.zeros((D, _LANES), _F32).at[:, :dm.E].set(w_router)

    o_k, o_v, o_z = Bw, Bw + KVw, Bw + 2 * KVw
    o_xs = o_z + Bw
    o_B = o_xs + Bw
    o_dt = o_B + 2 * GN
    o_xr = o_dt + 2 * SH
    o_gr, o_xp = o_xr + Bw, o_xr + 2 * Bw

    ks, vs, ssds, lrus = [], [], [], []
    for l in range(depth):
        wi = w_in[l]
        cols = lambda o, w: wi[:, o:o + w]
        w_slab = jnp.concatenate(
            [cols(0, Bw), cols(o_z, Bw), cols(o_xs, Bw), cols(o_xr, Bw), cols(o_gr, Bw), cols(o_xp, Bw),
             cols(o_k, KVw), cols(o_v, KVw), cols(o_B, 2 * GN), cols(o_dt, 2 * SH),
             jnp.zeros((D, dm.slab_w - dm.s_dt - 2 * SH), wi.dtype)], axis=1).astype(_BF16)
        slab = _matmul(h, w_slab, "in_proj")
        mix_in = jnp.zeros((dm.T, D), _BF16)

        mix_in = _attn_ctx(dm, slab, attn_sink[l], mix_in)
        qr, kr = _rope(dm, slab, cos, sin)
        mix_in = _attn_lat(dm, slab, qr, kr, cache_k4, cache_v4, attn_sink[l], l, mix_in)

        cw = ssd_conv_w[l]
        xs = _conv(dm, slab, dm.s_xs, Bw, cw[:, :Bw], ssd_conv_b[l][:Bw], True, _F32)
        bc = _conv(dm, slab, dm.s_B, 2 * GN, cw[:, Bw:], ssd_conv_b[l][Bw:], True, _BF16)
        pad = jnp.zeros((_LANES - 2 * SH,), _F32)
        dt_bias = jnp.concatenate([ssd_dt_bias[l].reshape(-1), pad]).reshape(1, _LANES)
        a_neg = jnp.concatenate([-jnp.exp(ssd_a_log[l].reshape(-1)), pad]).reshape(1, _LANES)
        yf, yb, s_f, s_b = _ssd(dm, slab, xs, bc, dt_bias, a_neg, state_ssd5, l)

        xf = _conv(dm, slab, dm.s_xr, Bw, lru_conv_w[l], lru_conv_b[l], False, _F32)
        hf, hb, fin_f, fin_b = _lru(dm, xf, lru_w_r[l].astype(_BF16), lru_w_i[l].astype(_BF16),
                                    lru_b_r[l], lru_b_i[l], lru_lambda[l], state_lru, l)

        d_exp = jnp.repeat(ssd_d[l], SSD_HEAD_DIM).reshape(1, Bw)
        mix_in = _finish(dm, slab, yf, yb, xs, d_exp, ssd_norm_g[l], hf, hb, mix_in)
        mix_in = _pool(dm, slab, pool_w[l], pool_b[l], pool_scale[l], mix_in)

        wo = w_out[l]
        w_mix = jnp.concatenate([wo[Bw:3 * Bw], wo[:Bw], wo[3 * Bw:]], axis=0).astype(_BF16)
        mix = _matmul(mix_in, w_mix, "out_proj")
        x1, h2, s_pad = _norm_router(dm, mix, xc, xl, mod5, ln1_g[l], ln1_b[l], w_router_pad, l, alpha)

        src_tok, block_e, n_used, n_valid, pos, wts = _route(s_pad[:, :dm.E], router_bias, _MOE_TM)
        xs_sorted = _gather(h2, src_tok, n_valid, _MOE_TM)
        g = _gateup(xs_sorted, block_e, n_used, moe_w_gate, moe_w_up, l, _MOE_TM)
        y = _down(g, block_e, n_used, moe_w_down, l, _MOE_TM)
        outs = _combine(dm, y, pos, wts, x1, mod5, ln2_g[l], ln2_b[l], l, alpha)
        xc, xl = outs[0], outs[1]
        if l + 1 < depth:
            h = outs[2]

        ks.append(slab[:dm.Tc, dm.s_k:dm.s_k + KVw].reshape(dm.B, dm.Lc, dm.KV, ATT_HEAD_DIM))
        vs.append(slab[:dm.Tc, dm.s_v:dm.s_v + KVw].reshape(dm.B, dm.Lc, dm.KV, ATT_HEAD_DIM))
        ssds.append(jnp.stack([s_f, s_b], axis=1).reshape(dm.B, 2, SH, SSD_HEAD_DIM, dm.N))
        lrus.append(jnp.concatenate([fin_f, fin_b], axis=1))

    y_prompt = xc.reshape(dm.B, dm.Lc, D)
    y_sample = xl.reshape(dm.DB, dm.Ll, D)
    return (y_prompt, y_sample, jnp.stack(ks, axis=1), jnp.stack(vs, axis=1),
            jnp.stack(ssds, axis=1), jnp.stack(lrus, axis=1))
```

```python
import functools
import math

import jax
import jax.numpy as jnp
import numpy as np
from jax import lax
from jax.experimental import pallas as pl
from jax.experimental.pallas import tpu as pltpu

_F32 = jnp.float32
_BF16 = jnp.bfloat16

GRID_W = 64
ATT_HEAD_DIM = 128
ATT_BLOCK = 128
ROPE_BASE = 10000.0
SSD_HEAD_DIM = 64
SSD_GROUPS = 2
SSD_CHUNK = 128
LRU_C = 8.0
POOL_WINDOWS = (2, 4, 8, 16)
N_EXPERT_GROUPS = 4
TOP_K = 2
EPS = 1e-5

_LANES = 128
_SUBLANES = 8
_VMEM_BYTES_V7X = 64 * 1024 * 1024
_VMEM_LIMIT = _VMEM_BYTES_V7X * 7 // 8

_ROW_BLOCK = 256
_MOE_TM = 512
_NEG = -1e30
_MIX_SSD_LRU, _MIX_ATT, _MIX_POOL = 0, 2, 3


def _call(body, *, grid, in_specs, out_specs, out_shape, scratch=(), nprefetch=0, aliases=None, name=None):
    grid_spec = pltpu.PrefetchScalarGridSpec(
        num_scalar_prefetch=nprefetch, grid=grid, in_specs=in_specs, out_specs=out_specs,
        scratch_shapes=list(scratch))
    return pl.pallas_call(
        body, grid_spec=grid_spec, out_shape=out_shape,
        compiler_params=pltpu.CompilerParams(
            dimension_semantics=("arbitrary",) * len(grid), vmem_limit_bytes=_VMEM_LIMIT),
        input_output_aliases=aliases or {}, name=name)


def _softplus(x):
    return jnp.maximum(x, 0.0) + jnp.log1p(jnp.exp(-jnp.abs(x)))


def _silu(x):
    return x * jax.nn.sigmoid(x)


class _Dims:
    def __init__(self, x_prompt, x_sample, cache_k, state_ssd, lru_w_r, w_router, moe_w_gate, w_ada):
        self.B, self.Lc, self.D = x_prompt.shape
        self.DB, self.Ll, _ = x_sample.shape
        self.depth = w_ada.shape[0]
        self.Tc, self.Tl = self.B * self.Lc, self.DB * self.Ll
        self.T = self.Tc + self.Tl
        self.Bw = self.D // 4
        self.past, self.KV = cache_k.shape[2], cache_k.shape[3]
        self.H = self.Bw // ATT_HEAD_DIM
        self.G = self.H // self.KV
        self.KVw = self.KV * ATT_HEAD_DIM
        self.SH = self.Bw // SSD_HEAD_DIM
        self.N = state_ssd.shape[-1]
        self.GN = SSD_GROUPS * self.N
        self.LH = lru_w_r.shape[2]
        self.E = w_router.shape[1]
        self.F = moe_w_gate.shape[-1]
        Bw, KVw, GN = self.Bw, self.KVw, self.GN
        self.s_q, self.s_z, self.s_xs, self.s_xr, self.s_gr, self.s_xp = (i * Bw for i in range(6))
        self.s_k = 6 * Bw
        self.s_v = self.s_k + KVw
        self.s_B = self.s_v + KVw
        self.s_C = self.s_B + GN
        self.s_dt = self.s_C + GN
        used = self.s_dt + _LANES
        self.slab_w = -(-used // 512) * 512
        assert 2 * self.SH <= _LANES and self.N == _LANES and ATT_HEAD_DIM == _LANES
        assert self.s_k % KVw == 0 and self.s_B % (2 * GN) == 0 and self.Bw % (2 * _LANES) == 0
        assert self.Lc % _ROW_BLOCK == 0 and self.Ll % _ROW_BLOCK == 0
        assert self.Lc % SSD_CHUNK == 0 and self.Ll % SSD_CHUNK == 0 and self.Ll % ATT_BLOCK == 0
        assert self.Bw // self.LH == _LANES and self.Bw % len(POOL_WINDOWS) == 0

    def seg_of_row(self, r0):
        return jnp.where(r0 < self.Tc, 0, 1 + (r0 - self.Tc) // self.Ll)

    def seq_pos(self, r0):
        is_ctx = r0 < self.Tc
        pos = jnp.where(is_ctx, r0 % self.Lc, (r0 - self.Tc) % self.Ll)
        return pos, jnp.where(is_ctx, self.Lc, self.Ll)


def _ada_body(c_ref, w_ref, b_ref, o_ref):
    s = _silu(c_ref[...]).astype(_BF16)
    o_ref[...] = jnp.dot(s, w_ref[...].astype(_BF16), preferred_element_type=_F32) + b_ref[...]


def _ada(cond, w_ada, b_ada):
    depth, D, n6 = w_ada.shape
    rows = cond.shape[0]
    tn = 512
    return _call(
        _ada_body, grid=(depth, n6 // tn),
        in_specs=[pl.BlockSpec((rows, D), lambda l, j: (0, 0)),
                  pl.BlockSpec((None, D, tn), lambda l, j: (l, 0, j)),
                  pl.BlockSpec((None, 1, tn), lambda l, j: (l, 0, j))],
        out_specs=pl.BlockSpec((None, rows, tn), lambda l, j: (l, 0, j)),
        out_shape=jax.ShapeDtypeStruct((depth, rows, n6), _F32), name="ada")(
            cond, w_ada, b_ada.reshape(depth, 1, n6))


def _mod_spec(dm, l, k, tm):
    return pl.BlockSpec((None, None, None, 1, dm.D), lambda i, *_: (l, dm.seg_of_row(i * tm), k, 0, 0))


def _modulate_body(xc_ref, xl_ref, sh_ref, sc_ref, o_ref, *, ctx_tiles):
    x = jnp.where(pl.program_id(0) < ctx_tiles, xc_ref[...], xl_ref[...])
    o_ref[...] = (x * (1.0 + sc_ref[...]) + sh_ref[...]).astype(o_ref.dtype)


def _modulate(dm, xc, xl, mod5, l):
    tm = 512
    return _call(
        functools.partial(_modulate_body, ctx_tiles=dm.Tc // tm), grid=(dm.T // tm,),
        in_specs=_stream_specs(dm, tm, dm.D) + [_mod_spec(dm, l, 0, tm), _mod_spec(dm, l, 1, tm)],
        out_specs=pl.BlockSpec((tm, dm.D), lambda i: (i, 0)),
        out_shape=jax.ShapeDtypeStruct((dm.T, dm.D), _BF16), name="modulate")(xc, xl, mod5, mod5)


def _mm_body(x_ref, w_ref, o_ref):
    o_ref[...] = jnp.dot(x_ref[...], w_ref[...], preferred_element_type=_F32).astype(o_ref.dtype)


def _pick_tile(n, candidates):
    for c in candidates:
        if n % c == 0:
            return c
    raise ValueError(f"no tile for {n}")


def _matmul(x, w, name, out_dtype=_F32):
    M, K = x.shape
    N = w.shape[1]
    tm = _pick_tile(M, (512, 256, 128))
    tn = _pick_tile(N, (1536, 1280, 1024, 768, 512))
    return _call(
        _mm_body, grid=(N // tn, M // tm),
        in_specs=[pl.BlockSpec((tm, K), lambda j, i: (i, 0)), pl.BlockSpec((K, tn), lambda j, i: (0, j))],
        out_specs=pl.BlockSpec((tm, tn), lambda j, i: (i, j)),
        out_shape=jax.ShapeDtypeStruct((M, N), out_dtype), name=name)(x, w)


def _rope_body(q_ref, k_ref, cos_ref, sin_ref, qo_ref, ko_ref):
    cos, sin = cos_ref[...], sin_ref[...]
    quarter = ATT_HEAD_DIM // 4
    lane = lax.broadcasted_iota(jnp.int32, cos.shape, 1)
    first = (lane % (2 * quarter)) < quarter

    def rot(x):
        partner = jnp.where(first, pltpu.roll(x, ATT_HEAD_DIM - quarter, 1), pltpu.roll(x, quarter, 1))
        return x * cos + partner * sin

    for ref, out in ((q_ref, qo_ref), (k_ref, ko_ref)):
        for h in range(ref.shape[1] // ATT_HEAD_DIM):
            sl = slice(h * ATT_HEAD_DIM, (h + 1) * ATT_HEAD_DIM)
            out[:, sl] = rot(ref[:, sl]).astype(out.dtype)


def _rope_tables(L):
    quarter = ATT_HEAD_DIM // 4
    t = jnp.arange(L)
    inv = ROPE_BASE ** (-jnp.arange(quarter, dtype=_F32) / quarter)
    ang_r = (t // GRID_W).astype(_F32)[:, None] * inv[None, :]
    ang_c = (t % GRID_W).astype(_F32)[:, None] * inv[None, :]
    cos = jnp.concatenate([jnp.cos(ang_r)] * 2 + [jnp.cos(ang_c)] * 2, -1)
    sin = jnp.concatenate([-jnp.sin(ang_r), jnp.sin(ang_r), -jnp.sin(ang_c), jnp.sin(ang_c)], -1)
    return cos, sin


def _rope(dm, slab, cos, sin):
    tr = _ROW_BLOCK
    r0, per_seq = dm.Tc // tr, dm.Ll // tr
    return _call(
        _rope_body, grid=(dm.Tl // tr,),
        in_specs=[pl.BlockSpec((tr, dm.Bw), lambda i: (r0 + i, dm.s_q // dm.Bw)),
                  pl.BlockSpec((tr, dm.KVw), lambda i: (r0 + i, dm.s_k // dm.KVw)),
                  pl.BlockSpec((tr, _LANES), lambda i: (i % per_seq, 0)),
                  pl.BlockSpec((tr, _LANES), lambda i: (i % per_seq, 0))],
        out_specs=[pl.BlockSpec((tr, dm.Bw), lambda i: (i, 0)), pl.BlockSpec((tr, dm.KVw), lambda i: (i, 0))],
        out_shape=[jax.ShapeDtypeStruct((dm.Tl, dm.Bw), _BF16), jax.ShapeDtypeStruct((dm.Tl, dm.KVw), _BF16)],
        name="rope")(slab, slab, cos, sin)


def _attend(sink_ref, q, k_all, v_all, bias, o_ref, kv, G):
    Q = q.shape[0]
    q_st = jnp.concatenate([q[:, g * ATT_HEAD_DIM:(g + 1) * ATT_HEAD_DIM] for g in range(G)], axis=0)
    s = lax.dot_general(q_st, k_all, (((1,), (1,)), ((), ())), preferred_element_type=_F32)
    s = s * (ATT_HEAD_DIM ** -0.5)
    if bias is not None:
        s = s + bias
    sk = jnp.concatenate([jnp.full((Q, 1), sink_ref[kv * G + g], _F32) for g in range(G)], axis=0)
    m = jnp.maximum(jnp.max(s, axis=-1, keepdims=True), sk)
    p = jnp.exp(s - m)
    denom = jnp.sum(p, axis=-1, keepdims=True) + jnp.exp(sk - m)
    o = jnp.dot(p.astype(_BF16), v_all, preferred_element_type=_F32) / denom
    for g in range(G):
        h = kv * G + g
        o_ref[:, h * ATT_HEAD_DIM:(h + 1) * ATT_HEAD_DIM] = o[g * Q:(g + 1) * Q].astype(o_ref.dtype)


def _attn_ctx_body(sink_ref, q_ref, k_ref, v_ref, _mix_in, o_ref, *, KV, G):
    for kv in range(KV):
        hs = slice(kv * ATT_HEAD_DIM, (kv + 1) * ATT_HEAD_DIM)
        q = q_ref[:, kv * G * ATT_HEAD_DIM:(kv + 1) * G * ATT_HEAD_DIM].astype(_BF16)
        _attend(sink_ref, q, k_ref[:, hs].astype(_BF16), v_ref[:, hs].astype(_BF16), None, o_ref, kv, G)


def _attn_ctx(dm, slab, sink, mix):
    Lc = dm.Lc
    return _call(
        functools.partial(_attn_ctx_body, KV=dm.KV, G=dm.G), grid=(dm.B,),
        in_specs=[pl.BlockSpec(memory_space=pltpu.SMEM),
                  pl.BlockSpec((Lc, dm.Bw), lambda b: (b, dm.s_q // dm.Bw)),
                  pl.BlockSpec((Lc, dm.KVw), lambda b: (b, dm.s_k // dm.KVw)),
                  pl.BlockSpec((Lc, dm.KVw), lambda b: (b, dm.s_v // dm.KVw)),
                  pl.BlockSpec(memory_space=pl.ANY)],
        out_specs=pl.BlockSpec((Lc, dm.Bw), lambda b: (b, _MIX_ATT)),
        out_shape=jax.ShapeDtypeStruct(mix.shape, mix.dtype), aliases={4: 0}, name="attn_ctx")(
            sink, slab, slab, slab, mix)


def _attn_lat_body(sink_ref, q_ref, kp_ref, kc_ref, kn_ref, vp_ref, vc_ref, vn_ref, ck_ref, cv_ref, _mix_in,
                   o_ref, *, KV, G, nb):
    n = pl.program_id(1)
    Q = ATT_BLOCK
    S = 3 * Q + ck_ref.shape[0]
    rows = G * Q
    c_i = lax.broadcasted_iota(jnp.int32, (rows, S), 1)
    qi = lax.broadcasted_iota(jnp.int32, (rows, S), 0) % Q
    bad_prev = c_i < jnp.maximum(qi, jnp.where(n == 0, Q, 0))
    bad_next = jnp.logical_and(c_i - 2 * Q > qi - jnp.where(n == nb - 1, Q, 0), c_i < 3 * Q)
    bias = jnp.where(jnp.logical_or(bad_prev, bad_next), _NEG, 0.0)
    for kv in range(KV):
        hs = slice(kv * ATT_HEAD_DIM, (kv + 1) * ATT_HEAD_DIM)
        k_all = jnp.concatenate([kp_ref[:, hs], kc_ref[:, hs], kn_ref[:, hs], ck_ref[:, hs].astype(_BF16)], axis=0)
        v_all = jnp.concatenate([vp_ref[:, hs].astype(_BF16), vc_ref[:, hs].astype(_BF16),
                                 vn_ref[:, hs].astype(_BF16), cv_ref[:, hs].astype(_BF16)], axis=0)
        q = q_ref[:, kv * G * ATT_HEAD_DIM:(kv + 1) * G * ATT_HEAD_DIM]
        _attend(sink_ref, q, k_all, v_all, bias, o_ref, kv, G)


def _attn_lat(dm, slab, qr, kr, cache_k4, cache_v4, sink, l, mix):
    Q = ATT_BLOCK
    nb = dm.Ll // Q
    r0 = dm.Tc // Q
    vcol = dm.s_v // dm.KVw
    prev = lambda n: jnp.maximum(n - 1, 0)
    nxt = lambda n: jnp.minimum(n + 1, nb - 1)
    cache_spec = pl.BlockSpec((None, None, dm.past, dm.KVw), lambda b, n: (b, l, 0, 0))
    return _call(
        functools.partial(_attn_lat_body, KV=dm.KV, G=dm.G, nb=nb), grid=(dm.DB, nb),
        in_specs=[pl.BlockSpec(memory_space=pltpu.SMEM),
                  pl.BlockSpec((Q, dm.Bw), lambda b, n: (b * nb + n, 0)),
                  pl.BlockSpec((Q, dm.KVw), lambda b, n: (b * nb + prev(n), 0)),
                  pl.BlockSpec((Q, dm.KVw), lambda b, n: (b * nb + n, 0)),
                  pl.BlockSpec((Q, dm.KVw), lambda b, n: (b * nb + nxt(n), 0)),
                  pl.BlockSpec((Q, dm.KVw), lambda b, n: (r0 + b * nb + prev(n), vcol)),
                  pl.BlockSpec((Q, dm.KVw), lambda b, n: (r0 + b * nb + n, vcol)),
                  pl.BlockSpec((Q, dm.KVw), lambda b, n: (r0 + b * nb + nxt(n), vcol)),
                  cache_spec, cache_spec, pl.BlockSpec(memory_space=pl.ANY)],
        out_specs=pl.BlockSpec((Q, dm.Bw), lambda b, n: (r0 + b * nb + n, _MIX_ATT)),
        out_shape=jax.ShapeDtypeStruct(mix.shape, mix.dtype), aliases={10: 0}, name="attn_lat")(
            sink, qr, kr, kr, kr, slab, slab, slab, cache_k4, cache_v4, mix)


def _halo_specs(dm, cw, colblk):
    R = _ROW_BLOCK
    per = R // _SUBLANES
    last = dm.T // _SUBLANES - 1
    return [pl.BlockSpec((R, cw), lambda i, *_: (i, colblk)),
            pl.BlockSpec((_SUBLANES, cw), lambda i, *_: (jnp.maximum(i * per - 1, 0), colblk)),
            pl.BlockSpec((_SUBLANES, cw), lambda i, *_: (jnp.minimum((i + 1) * per, last), colblk))]


def _fill_padded(dm, pad_ref, x_ref, prev_ref, next_ref):
    R = _ROW_BLOCK
    r0 = pl.program_id(0) * R
    pos, L = dm.seq_pos(r0)
    pad_ref[pl.ds(_SUBLANES, R), :] = x_ref[...].astype(_F32)
    pad_ref[pl.ds(0, _SUBLANES), :] = jnp.where(pos == 0, 0.0, prev_ref[...].astype(_F32))
    pad_ref[pl.ds(_SUBLANES + R, _SUBLANES), :] = jnp.where(pos + R == L, 0.0, next_ref[...].astype(_F32))
    return pos, L


def _conv_body(x_ref, prev_ref, next_ref, w_ref, b_ref, o_ref, pad_ref, *, dm, act):
    _fill_padded(dm, pad_ref, x_ref, prev_ref, next_ref)
    R = _ROW_BLOCK
    width = w_ref.shape[0]
    left = (width - 1) // 2
    out = pad_ref[pl.ds(_SUBLANES - left, R), :] * w_ref[0:1, :]
    for tap in range(1, width):
        out = out + pad_ref[pl.ds(_SUBLANES - left + tap, R), :] * w_ref[tap:tap + 1, :]
    out = out + b_ref[...]
    if act:
        out = _silu(out)
    o_ref[...] = out.astype(o_ref.dtype)


def _conv(dm, slab, col, cw, w, b, act, out_dtype):
    R = _ROW_BLOCK
    width = w.shape[0]
    return _call(
        functools.partial(_conv_body, dm=dm, act=act), grid=(dm.T // R,),
        in_specs=_halo_specs(dm, cw, col // cw) + [pl.BlockSpec((width, cw), lambda i: (0, 0)),
                                                   pl.BlockSpec((1, cw), lambda i: (0, 0))],
        out_specs=pl.BlockSpec((R, cw), lambda i: (i, 0)),
        out_shape=jax.ShapeDtypeStruct((dm.T, cw), out_dtype),
        scratch=[pltpu.VMEM((R + 2 * _SUBLANES, cw), _F32)], name="conv")(slab, slab, slab, w, b.reshape(1, cw))


def _pool_body(x_ref, prev_ref, next_ref, w_ref, b_ref, sc_ref, _mix_in, o_ref, pad_ref, *, dm):
    pos, L = _fill_padded(dm, pad_ref, x_ref, prev_ref, next_ref)
    R = _ROW_BLOCK
    gd = w_ref.shape[1]
    t = pos + lax.broadcasted_iota(jnp.int32, (R, 1), 0)
    for gi, win in enumerate(POOL_WINDOWS):
        cs = slice(gi * gd, (gi + 1) * gd)
        half = win // 2
        total = pad_ref[pl.ds(_SUBLANES - half, R), cs]
        for k in range(1, win):
            total = total + pad_ref[pl.ds(_SUBLANES - half + k, R), cs]
        cnt = (jnp.minimum(t + (win - half), L) - jnp.maximum(t - half, 0)).astype(_F32)
        pooled = total / cnt - pad_ref[pl.ds(_SUBLANES, R), cs]
        y = jnp.dot(pooled.astype(_BF16), w_ref[gi], preferred_element_type=_F32) + b_ref[:, cs]
        o_ref[:, cs] = (y * sc_ref[:, cs]).astype(o_ref.dtype)


def _pool(dm, slab, w, b, scale, mix):
    R = _ROW_BLOCK
    ng, gd, _ = w.shape
    assert max(POOL_WINDOWS) // 2 <= _SUBLANES
    row = pl.BlockSpec((1, dm.Bw), lambda i: (0, 0))
    return _call(
        functools.partial(_pool_body, dm=dm), grid=(dm.T // R,),
        in_specs=_halo_specs(dm, dm.Bw, dm.s_xp // dm.Bw) + [pl.BlockSpec((ng, gd, gd), lambda i: (0, 0, 0)), row, row,
                                                              pl.BlockSpec(memory_space=pl.ANY)],
        out_specs=pl.BlockSpec((R, dm.Bw), lambda i: (i, _MIX_POOL)),
        out_shape=jax.ShapeDtypeStruct(mix.shape, mix.dtype), aliases={6: 0},
        scratch=[pltpu.VMEM((R + 2 * _SUBLANES, dm.Bw), _F32)], name="pool")(
            slab, slab, slab, w.astype(_BF16), b.reshape(1, -1), scale.reshape(1, -1), mix)


def _chunk_pos(dm, ci):
    ncc, ncl = dm.Lc // SSD_CHUNK, dm.Ll // SSD_CHUNK
    nct = dm.Tc // SSD_CHUNK
    is_ctx = ci < nct
    pos = jnp.where(is_ctx, ci % ncc, (ci - nct) % ncl)
    nseq = jnp.where(is_ctx, ncc, ncl)
    lat = jnp.clip((ci - nct) // ncl, 0, dm.DB - 1)
    return is_ctx, pos, nseq, lat


def _spread_heads(v, xm):
    hi = v.astype(_BF16)
    lo = (v - hi.astype(_F32)).astype(_BF16)
    return jnp.dot(hi, xm, preferred_element_type=_F32) + jnp.dot(lo, xm, preferred_element_type=_F32)


def _ssd_dir(phase, dm, ci, rev, x_ref, bc_ref, dt_ref, bias_ref, a_ref, xm_ref, h0_ref, y_ref, s_ref, h_scr):
    Q, N, SH, P = SSD_CHUNK, dm.N, dm.SH, SSD_HEAD_DIM
    hg = SH // SSD_GROUPS
    gw = hg * P
    is_ctx, pos, nseq, _ = _chunk_pos(dm, ci)
    first = pos == (nseq - 1 if rev else 0)
    last = pos == (0 if rev else nseq - 1)

    if phase == "init":
        @pl.when(jnp.logical_and(first, is_ctx))
        def _():
            h_scr[...] = jnp.zeros_like(h_scr)

        @pl.when(jnp.logical_and(first, jnp.logical_not(is_ctx)))
        def _():
            for k in range(dm.Bw // _LANES):
                h_scr[:, k * _LANES:(k + 1) * _LANES] = h0_ref[k * _LANES:(k + 1) * _LANES, :].T
        return

    if phase == "final":
        @pl.when(jnp.logical_and(last, is_ctx))
        def _():
            for k in range(dm.Bw // _LANES):
                s_ref[k * _LANES:(k + 1) * _LANES, :] = h_scr[:, k * _LANES:(k + 1) * _LANES].T
        return

    lane0 = (SH if rev else 0)
    dt = _softplus(dt_ref[...] + bias_ref[...])
    a = dt * a_ref[...]
    ri = lax.broadcasted_iota(jnp.int32, (Q, Q), 0)
    cj = lax.broadcasted_iota(jnp.int32, (Q, Q), 1)
    tri = (ri <= cj) if rev else (ri >= cj)
    cum = jnp.dot(tri.astype(_F32), a, preferred_element_type=_F32, precision=lax.Precision.HIGHEST)
    cum_t, dt_t = cum.T, dt.T
    ecum = jnp.exp(cum)
    end = 0 if rev else Q - 1
    wend = jnp.exp(cum[end:end + 1, :] - cum) * dt
    lane = lax.broadcasted_iota(jnp.int32, (Q, _LANES), 1)
    low = lane < P
    xm = xm_ref[1 if rev else 0]
    e_all, w_all = _spread_heads(ecum, xm), _spread_heads(wend, xm)

    for g in range(SSD_GROUPS):
        b_g = bc_ref[:, g * N:(g + 1) * N]
        c_g = bc_ref[:, dm.GN + g * N:dm.GN + (g + 1) * N]
        cb = lax.dot_general(c_g, b_g, (((1,), (1,)), ((), ())), preferred_element_type=_F32)
        gs = slice(g * gw, (g + 1) * gw)
        y_diag = []
        for p in range(hg // 2):
            xs = slice(g * gw + p * _LANES, g * gw + (p + 1) * _LANES)
            xp = x_ref[:, xs].astype(_BF16)
            ys = []
            for s in range(2):
                ln = lane0 + g * hg + 2 * p + s
                seg = cum[:, ln:ln + 1] - cum_t[ln:ln + 1, :]
                m = cb * jnp.exp(jnp.where(tri, seg, _NEG)) * dt_t[ln:ln + 1, :]
                ys.append(jnp.dot(m.astype(_BF16), xp, preferred_element_type=_F32))
            ln = lane0 + g * hg + 2 * p
            y_diag.append(jnp.where(low, ys[0], ys[1]))
        y_diag = jnp.concatenate(y_diag, axis=1)
        e_exp, w_exp = e_all[:, gs], w_all[:, gs]
        h_in = h_scr[:, gs]
        y_off = jnp.dot(c_g, h_in.astype(_BF16), preferred_element_type=_F32) * e_exp
        y_ref[:, gs] = y_diag + y_off
        xw = (x_ref[:, gs] * w_exp).astype(_BF16)
        st = lax.dot_general(b_g, xw, (((0,), (0,)), ((), ())), preferred_element_type=_F32)
        h_scr[:, gs] = h_in * e_exp[end:end + 1, :] + st


def _ssd_body(xf_ref, bcf_ref, dtf_ref, xb_ref, bcb_ref, dtb_ref, bias_ref, a_ref, xm_ref, h0f_ref, h0b_ref,
              yf_ref, yb_ref, sf_ref, sb_ref, hf_scr, hb_scr, *, dm):
    i = pl.program_id(0)
    n = pl.num_programs(0)
    for phase in ("init", "compute", "final"):
        _ssd_dir(phase, dm, i, False, xf_ref, bcf_ref, dtf_ref, bias_ref, a_ref, xm_ref, h0f_ref, yf_ref, sf_ref,
                 hf_scr)
        _ssd_dir(phase, dm, n - 1 - i, True, xb_ref, bcb_ref, dtb_ref, bias_ref, a_ref, xm_ref, h0b_ref, yb_ref,
                 sb_ref, hb_scr)


def _ssd(dm, slab, xs, bc, dt_bias, a_neg, state_ssd5, l):
    Q = SSD_CHUNK
    nchunks = dm.T // Q
    ncc = dm.Lc // Q
    HP = dm.SH * SSD_HEAD_DIM
    dcol = dm.s_dt // _LANES
    fwd = lambda i: i
    bwd = lambda i: nchunks - 1 - i

    def seq_c(ci):
        return jnp.minimum(ci // ncc, dm.B - 1)

    def lat_of(ci):
        return _chunk_pos(dm, ci)[3]

    def specs(order):
        return [pl.BlockSpec((Q, dm.Bw), lambda i: (order(i), 0)),
                pl.BlockSpec((Q, 2 * dm.GN), lambda i: (order(i), 0)),
                pl.BlockSpec((Q, _LANES), lambda i: (order(i), dcol))]

    row = pl.BlockSpec((1, _LANES), lambda i: (0, 0))
    head_of_col = jnp.arange(dm.Bw) // SSD_HEAD_DIM
    spread = jnp.stack([(jnp.arange(_LANES)[:, None] == (d * dm.SH + head_of_col)[None, :]) for d in range(2)])
    spread = spread.astype(_BF16)
    h0 = lambda order, d: pl.BlockSpec((None, None, None, HP, dm.N), lambda i: (lat_of(order(i)), l, d, 0, 0))
    st = lambda order: pl.BlockSpec((None, HP, dm.N), lambda i: (seq_c(order(i)), 0, 0))
    ysd = jax.ShapeDtypeStruct((dm.T, dm.Bw), _F32)
    ssd = jax.ShapeDtypeStruct((dm.B, HP, dm.N), _F32)
    return _call(
        functools.partial(_ssd_body, dm=dm), grid=(nchunks,),
        in_specs=specs(fwd) + specs(bwd) + [row, row, pl.BlockSpec((2, _LANES, dm.Bw), lambda i: (0, 0, 0)),
                                            h0(fwd, 0), h0(bwd, 1)],
        out_specs=[pl.BlockSpec((Q, dm.Bw), lambda i: (fwd(i), 0)), pl.BlockSpec((Q, dm.Bw), lambda i: (bwd(i), 0)),
                   st(fwd), st(bwd)],
        out_shape=[ysd, ysd, ssd, ssd],
        scratch=[pltpu.VMEM((dm.N, dm.Bw), _F32), pltpu.VMEM((dm.N, dm.Bw), _F32)], name="ssd")(
            xs, bc, slab, xs, bc, slab, dt_bias, a_neg, spread, state_ssd5, state_ssd5)


def _lru_gates(dm, d, x_ref, wr_ref, wi_ref, br_ref, bi_ref, lam_ref, a_scr, u_scr):
    sp = _softplus(-lam_ref[d:d + 1, :])
    for h in range(dm.LH):
        hs = slice(h * _LANES, (h + 1) * _LANES)
        x = x_ref[:, hs]
        xb = x.astype(_BF16)
        r = jax.nn.sigmoid(jnp.dot(xb, wr_ref[d, h], preferred_element_type=_F32) + br_ref[d:d + 1, hs])
        ig = jax.nn.sigmoid(jnp.dot(xb, wi_ref[d, h], preferred_element_type=_F32) + bi_ref[d:d + 1, hs])
        log_a = -LRU_C * r * sp[:, hs]
        a = jnp.exp(log_a)
        a_scr[:, hs] = a
        u_scr[:, hs] = jnp.sqrt(-jnp.tanh(log_a) * (a * a + 1.0)) * (ig * x)


def _lru_body(xf_ref, xb_ref, wr_ref, wi_ref, br_ref, bi_ref, lam_ref, h0f_ref, h0b_ref,
              hf_ref, hb_ref, finf_ref, finb_ref, af, uf, ab, ub, carry, *, dm):
    R = _ROW_BLOCK
    i = pl.program_id(0)
    n = pl.num_programs(0)
    _lru_gates(dm, 0, xf_ref, wr_ref, wi_ref, br_ref, bi_ref, lam_ref, af, uf)
    _lru_gates(dm, 1, xb_ref, wr_ref, wi_ref, br_ref, bi_ref, lam_ref, ab, ub)

    posf, Lf = dm.seq_pos(i * R)
    posb, Lb = dm.seq_pos((n - 1 - i) * R)
    ctx_f = i * R < dm.Tc
    ctx_b = (n - 1 - i) * R < dm.Tc
    for d, start, is_ctx, h0_ref in ((0, posf == 0, ctx_f, h0f_ref), (1, posb + R == Lb, ctx_b, h0b_ref)):
        @pl.when(jnp.logical_and(start, is_ctx))
        def _():
            carry[d:d + 1, :] = jnp.zeros((1, dm.Bw), _F32)

        @pl.when(jnp.logical_and(start, jnp.logical_not(is_ctx)))
        def _():
            carry[d:d + 1, :] = h0_ref[d:d + 1, :]

    def tile(t, hs):
        hf, hb = hs
        base_f = pl.multiple_of(t * _SUBLANES, _SUBLANES)
        base_b = pl.multiple_of(R - _SUBLANES - t * _SUBLANES, _SUBLANES)
        for r in range(_SUBLANES):
            rf = base_f + r
            hf = af[pl.ds(rf, 1), :] * hf + uf[pl.ds(rf, 1), :]
            hf_ref[pl.ds(rf, 1), :] = hf
            rb = base_b + (_SUBLANES - 1 - r)
            hb = ab[pl.ds(rb, 1), :] * hb + ub[pl.ds(rb, 1), :]
            hb_ref[pl.ds(rb, 1), :] = hb
        return hf, hb

    hf, hb = lax.fori_loop(0, R // _SUBLANES, tile, (carry[0:1, :], carry[1:2, :]))
    carry[0:1, :] = hf
    carry[1:2, :] = hb

    @pl.when(jnp.logical_and(posf + R == Lf, ctx_f))
    def _():
        finf_ref[...] = hf

    @pl.when(jnp.logical_and(posb == 0, ctx_b))
    def _():
        finb_ref[...] = hb


def _lru(dm, xf, w_r, w_i, b_r, b_i, lam, state_lru, l):
    R = _ROW_BLOCK
    nblk = dm.T // R
    fwd = lambda i: i
    bwd = lambda i: nblk - 1 - i
    wspec = pl.BlockSpec((2, dm.LH, _LANES, _LANES), lambda i: (0, 0, 0, 0))
    vspec = pl.BlockSpec((2, dm.Bw), lambda i: (0, 0))
    lat_seq = lambda blk: jnp.clip((blk * R - dm.Tc) // dm.Ll, 0, dm.DB - 1)
    ctx_seq = lambda blk: jnp.minimum(blk * R // dm.Lc, dm.B - 1)
    h0 = lambda order: pl.BlockSpec((None, None, 2, dm.Bw), lambda i: (lat_seq(order(i)), l, 0, 0))
    fin = lambda order: pl.BlockSpec((None, 1, dm.Bw), lambda i: (ctx_seq(order(i)), 0, 0))
    hsd = jax.ShapeDtypeStruct((dm.T, dm.Bw), _F32)
    fsd = jax.ShapeDtypeStruct((dm.B, 1, dm.Bw), _F32)
    return _call(
        functools.partial(_lru_body, dm=dm), grid=(nblk,),
        in_specs=[pl.BlockSpec((R, dm.Bw), lambda i: (fwd(i), 0)), pl.BlockSpec((R, dm.Bw), lambda i: (bwd(i), 0)),
                  wspec, wspec, vspec, vspec, vspec, h0(fwd), h0(bwd)],
        out_specs=[pl.BlockSpec((R, dm.Bw), lambda i: (fwd(i), 0)), pl.BlockSpec((R, dm.Bw), lambda i: (bwd(i), 0)),
                   fin(fwd), fin(bwd)],
        out_shape=[hsd, hsd, fsd, fsd],
        scratch=[pltpu.VMEM((R, dm.Bw), _F32)] * 4 + [pltpu.VMEM((2, dm.Bw), _F32)], name="lru")(
            xf, xf, w_r, w_i, b_r, b_i, lam, state_lru, state_lru)


def _finish_body(yf_ref, yb_ref, xs_ref, z_ref, d_ref, ng_ref, hf_ref, hb_ref, gr_ref, _mix_in, o_ref):
    y = (yf_ref[...] + yb_ref[...] + d_ref[...] * xs_ref[...]) * _silu(z_ref[...])
    gw = y.shape[1] // SSD_GROUPS
    for g in range(SSD_GROUPS):
        gs = slice(g * gw, (g + 1) * gw)
        yg = y[:, gs]
        yg = yg * lax.rsqrt(jnp.mean(yg * yg, axis=-1, keepdims=True) + EPS)
        o_ref[:, gs] = (yg * ng_ref[:, gs]).astype(o_ref.dtype)
    bw = y.shape[1]
    o_ref[:, bw:] = ((hf_ref[...] + hb_ref[...]) * jax.nn.gelu(gr_ref[...], approximate=True)).astype(o_ref.dtype)


def _finish(dm, slab, yf, yb, xs, d_exp, norm_g, hf, hb, mix):
    R = _ROW_BLOCK
    blk = pl.BlockSpec((R, dm.Bw), lambda i: (i, 0))
    row = pl.BlockSpec((1, dm.Bw), lambda i: (0, 0))
    return _call(
        _finish_body, grid=(dm.T // R,),
        in_specs=[blk, blk, blk, pl.BlockSpec((R, dm.Bw), lambda i: (i, dm.s_z // dm.Bw)), row, row,
                  blk, blk, pl.BlockSpec((R, dm.Bw), lambda i: (i, dm.s_gr // dm.Bw)),
                  pl.BlockSpec(memory_space=pl.ANY)],
        out_specs=pl.BlockSpec((R, 2 * dm.Bw), lambda i: (i, _MIX_SSD_LRU)),
        out_shape=jax.ShapeDtypeStruct(mix.shape, mix.dtype), aliases={9: 0}, name="finish")(
            yf, yb, xs, slab, d_exp, norm_g.reshape(1, -1), hf, hb, slab, mix)


def _layer_norm(r, g, b):
    mu = jnp.mean(r, axis=-1, keepdims=True)
    c = r - mu
    var = jnp.mean(c * c, axis=-1, keepdims=True)
    return c * lax.rsqrt(var + EPS) * g + b


def _split_bf16(x):
    hi = x.astype(_BF16)
    return hi, (x - hi.astype(_F32)).astype(_BF16)


_HI16 = 0xFFFF0000


def _pack_bf16_pair(a, b):
    ua = pltpu.bitcast(a.astype(_BF16).astype(_F32), jnp.uint32)
    ub = pltpu.bitcast(b.astype(_BF16).astype(_F32), jnp.uint32)
    return (ua >> 16) | (ub & jnp.uint32(_HI16))


def _unpack_bf16_pair(p):
    return pltpu.bitcast(p << 16, _F32), pltpu.bitcast(p & jnp.uint32(_HI16), _F32)


def _norm_router_body(mix_ref, xc_ref, xl_ref, g1_ref, lg_ref, lb_ref, sc_ref, sh_ref, wr_ref, x1_ref, h_ref, s_ref,
                      *, alpha, ctx_tiles):
    x = jnp.where(pl.program_id(0) < ctx_tiles, xc_ref[...], xl_ref[...])
    x1 = _layer_norm(alpha * x + g1_ref[...] * mix_ref[...], lg_ref[...], lb_ref[...])
    x1_ref[...] = x1
    h = x1 * (1.0 + sc_ref[...]) + sh_ref[...]
    half = h.shape[1] // 2
    h_ref[...] = _pack_bf16_pair(h[:, :half], h[:, half:])
    hh, hl = _split_bf16(h)
    wh, wl = _split_bf16(wr_ref[...])
    logits = (jnp.dot(hh, wh, preferred_element_type=_F32) + jnp.dot(hh, wl, preferred_element_type=_F32)
              + jnp.dot(hl, wh, preferred_element_type=_F32))
    s_ref[...] = jax.nn.sigmoid(logits)


def _stream_specs(dm, tm, width):
    ctx_tiles = dm.Tc // tm
    return [pl.BlockSpec((tm, width), lambda i, *_: (jnp.minimum(i, ctx_tiles - 1), 0)),
            pl.BlockSpec((tm, width), lambda i, *_: (jnp.maximum(i - ctx_tiles, 0), 0))]


def _norm_router(dm, mix, xc, xl, mod5, ln_g, ln_b, w_router_pad, l, alpha):
    tm = 256
    D = dm.D
    tile = pl.BlockSpec((tm, D), lambda i: (i, 0))
    row = pl.BlockSpec((1, D), lambda i: (0, 0))
    return _call(
        functools.partial(_norm_router_body, alpha=alpha, ctx_tiles=dm.Tc // tm), grid=(dm.T // tm,),
        in_specs=[tile] + _stream_specs(dm, tm, D) + [
            _mod_spec(dm, l, 2, tm), row, row, _mod_spec(dm, l, 4, tm), _mod_spec(dm, l, 3, tm),
            pl.BlockSpec((D, _LANES), lambda i: (0, 0))],
        out_specs=[tile, pl.BlockSpec((tm, D // 2), lambda i: (i, 0)), pl.BlockSpec((tm, _LANES), lambda i: (i, 0))],
        out_shape=[jax.ShapeDtypeStruct((dm.T, D), _F32), jax.ShapeDtypeStruct((dm.T, D // 2), jnp.uint32),
                   jax.ShapeDtypeStruct((dm.T, _LANES), _F32)], name="norm_router")(
            mix, xc, xl, mod5, ln_g.reshape(1, D), ln_b.reshape(1, D), mod5, mod5, w_router_pad)


def _top2(v):
    n = v.shape[-1]
    i1 = jnp.argmax(v, axis=-1)
    m1 = jnp.max(v, axis=-1)
    rest = jnp.where(jnp.arange(n) == i1[..., None], -jnp.inf, v)
    return (m1, jnp.max(rest, axis=-1)), (i1, jnp.argmax(rest, axis=-1))


def _route(s, router_bias, tm):
    T, E = s.shape
    per = E // N_EXPERT_GROUPS
    assert TOP_K == 2
    sel = s + router_bias.astype(_F32)
    grp_score = sum(_top2(sel.reshape(T, N_EXPERT_GROUPS, per))[0])
    grp = jnp.argmax(grp_score, axis=-1)
    in_grp = (jnp.arange(E) // per)[None, :] == grp[:, None]
    idx = jnp.stack(_top2(jnp.where(in_grp, sel, -jnp.inf))[1], axis=-1)
    wts = jnp.take_along_axis(s, idx, axis=-1)
    wts = wts / jnp.sum(wts, -1, keepdims=True)
    A = T * TOP_K
    onehot = (idx.reshape(A, 1) == jnp.arange(E, dtype=idx.dtype)[None, :]).astype(jnp.int32)
    csum = jnp.cumsum(onehot, axis=0)
    counts = csum[-1]
    padded = (counts + tm - 1) // tm * tm
    pend = jnp.cumsum(padded)
    pstart = pend - padded
    pos = jnp.sum(onehot * (csum - 1 + pstart[None, :]), axis=1).astype(jnp.int32)
    n_blocks = -(-A // tm) + E
    src_tok = jnp.zeros((n_blocks * tm,), jnp.int32).at[pos].set(jnp.arange(A, dtype=jnp.int32) // TOP_K)
    blk0 = jnp.arange(n_blocks, dtype=jnp.int32) * tm
    block_e = jnp.minimum(jnp.sum((blk0[:, None] >= pend[None, :]).astype(jnp.int32), axis=1), E - 1)
    n_used = (pend[-1] // tm).astype(jnp.int32).reshape(1)
    n_valid = jnp.clip(counts[block_e] - (blk0 - pstart[block_e]), 0, tm)
    n_valid = -(-n_valid // _DMA_UNROLL) * _DMA_UNROLL
    n_valid = jnp.where(blk0 < pend[-1], n_valid, 0).astype(jnp.int32)
    return src_tok, block_e.astype(jnp.int32), n_used, n_valid, counts > 0, pos, wts


def _row_copy(src_hbm, row, dst, dst_row, sem):
    return pltpu.make_async_copy(src_hbm.at[pl.ds(row, 1)], dst.at[pl.ds(dst_row, 1)], sem)


_DMA_UNROLL = 8


def _gather_body(src_ref, nv_ref, h_hbm, o_ref, buf, sem):
    tm = o_ref.shape[0]
    i = pl.program_id(0)
    n = pl.num_programs(0)

    def issue(step, slot):
        def body(g, c):
            for u in range(_DMA_UNROLL):
                r = g * _DMA_UNROLL + u
                _row_copy(h_hbm, src_ref[step * tm + r], buf.at[slot], r, sem.at[slot]).start()
            return c
        lax.fori_loop(0, nv_ref[step] // _DMA_UNROLL, body, 0)

    @pl.when(i == 0)
    def _():
        buf[...] = jnp.zeros_like(buf)
        issue(0, 0)

    @pl.when(i + 1 < n)
    def _():
        issue(i + 1, (i + 1) % 2)

    slot = i % 2

    def wait(g, c):
        for u in range(_DMA_UNROLL):
            _row_copy(h_hbm, 0, buf.at[slot], g * _DMA_UNROLL + u, sem.at[slot]).wait()
        return c
    lax.fori_loop(0, nv_ref[i] // _DMA_UNROLL, wait, 0)
    half = buf.shape[2]
    lo, hi = _unpack_bf16_pair(buf[slot])
    o_ref[:, :half] = lo.astype(o_ref.dtype)
    o_ref[:, half:] = hi.astype(o_ref.dtype)


def _gather(h_packed, src_tok, n_valid, tm):
    half = h_packed.shape[1]
    n_blocks = src_tok.shape[0] // tm
    return _call(
        _gather_body, grid=(n_blocks,), nprefetch=2,
        in_specs=[pl.BlockSpec(memory_space=pl.ANY)],
        out_specs=pl.BlockSpec((tm, 2 * half), lambda i, src, nv: (i, 0)),
        out_shape=jax.ShapeDtypeStruct((n_blocks * tm, 2 * half), _BF16),
        scratch=[pltpu.VMEM((2, tm, half), jnp.uint32), pltpu.SemaphoreType.DMA((2,))], name="moe_gather")(
            src_tok, n_valid, h_packed)


def _expert_changed(be_ref, m):
    return jnp.logical_or(m == 0, be_ref[m] != be_ref[jnp.maximum(m - 1, 0)])


def _gateup_body(be_ref, nu_ref, x_ref, wg_ref, wu_ref, g_ref, wg_s, wu_s):
    m = pl.program_id(1)

    @pl.when(_expert_changed(be_ref, m))
    def _():
        wg_s[...] = wg_ref[...].astype(_BF16)
        wu_s[...] = wu_ref[...].astype(_BF16)

    @pl.when(m < nu_ref[0])
    def _():
        x = x_ref[...]
        gate = jnp.dot(x, wg_s[...], preferred_element_type=_F32)
        up = jnp.dot(x, wu_s[...], preferred_element_type=_F32)
        g_ref[...] = (_silu(gate) * up).astype(g_ref.dtype)

    @pl.when(m >= nu_ref[0])
    def _():
        g_ref[...] = jnp.zeros_like(g_ref)


def _gateup(xs, block_e, n_used, w_gate, w_up, l, tm):
    A_pad, D = xs.shape
    F = w_gate.shape[-1]
    tf = _pick_tile(F, (512, 256, 128))
    wspec = pl.BlockSpec((None, None, D, tf), lambda j, m, be, nu: (l, be[m], 0, j))
    return _call(
        _gateup_body, grid=(F // tf, A_pad // tm), nprefetch=2,
        in_specs=[pl.BlockSpec((tm, D), lambda j, m, be, nu: (m, 0)), wspec, wspec],
        out_specs=pl.BlockSpec((tm, tf), lambda j, m, be, nu: (m, j)),
        out_shape=jax.ShapeDtypeStruct((A_pad, F), _BF16),
        scratch=[pltpu.VMEM((D, tf), _BF16)] * 2, name="moe_gateup")(block_e, n_used, xs, w_gate, w_up)


_W_RING = 3


def _down_body(be_ref, nu_ref, rank_ref, sege_ref, segj_ref, ns_ref, g_ref, wd_hbm, y_ref, wbuf, wd_s, sem, *, l):
    j = pl.program_id(0)
    m = pl.program_id(1)
    tn = wd_s.shape[1]
    n_seg, n_present = ns_ref[0], ns_ref[1]

    def seg_copy(s, slot):
        col = pl.multiple_of(segj_ref[s] * tn, tn)
        return pltpu.make_async_copy(wd_hbm.at[l, sege_ref[s], :, pl.ds(col, tn)], wbuf.at[slot], sem.at[slot])

    @pl.when(jnp.logical_and(j == 0, m == 0))
    def _():
        for s in range(_W_RING - 1):
            @pl.when(s < n_seg)
            def _():
                seg_copy(s, s).start()

    r_cur = rank_ref[be_ref[m]]
    r_prev = rank_ref[be_ref[jnp.maximum(m - 1, 0)]]

    @pl.when(jnp.logical_or(m == 0, r_cur != r_prev))
    def _():
        s = j * n_present + r_cur
        slot = lax.rem(s, _W_RING)
        seg_copy(s, slot).wait()
        wd_s[...] = wbuf[slot].astype(_BF16)
        ahead = s + (_W_RING - 1)

        @pl.when(ahead < n_seg)
        def _():
            seg_copy(ahead, lax.rem(ahead, _W_RING)).start()

    @pl.when(m < nu_ref[0])
    def _():
        y = jnp.dot(g_ref[...], wd_s[...], preferred_element_type=_F32)
        half = y.shape[1] // 2
        y_ref[...] = _pack_bf16_pair(y[:, :half], y[:, half:])

    @pl.when(m >= nu_ref[0])
    def _():
        y_ref[...] = jnp.zeros_like(y_ref)


def _down_tile(D):
    return _pick_tile(D, (2048, 1024, 512, 256))


def _down_segments(present, n_tiles):
    E = present.shape[0]
    rank = jnp.maximum(jnp.cumsum(present.astype(jnp.int32)) - 1, 0)
    n_present = jnp.sum(present.astype(jnp.int32))
    hit = jnp.logical_and(present[None, :], rank[None, :] == jnp.arange(E)[:, None])
    expert_of_rank = jnp.sum(jnp.where(hit, jnp.arange(E)[None, :], 0), axis=1)
    s = jnp.arange(n_tiles * E, dtype=jnp.int32)
    n_seg = n_tiles * n_present
    live = s < n_seg
    seg_e = jnp.where(live, expert_of_rank[s % n_present], 0).astype(jnp.int32)
    seg_j = jnp.where(live, s // n_present, 0).astype(jnp.int32)
    return rank.astype(jnp.int32), seg_e, seg_j, jnp.stack([n_seg, n_present]).astype(jnp.int32)


def _down(g, block_e, n_used, present, w_down, l, tm):
    A_pad, F = g.shape
    D = w_down.shape[-1]
    tn = _down_tile(D)
    rank, seg_e, seg_j, ns = _down_segments(present, D // tn)
    return _call(
        functools.partial(_down_body, l=l), grid=(D // tn, A_pad // tm), nprefetch=6,
        in_specs=[pl.BlockSpec((tm, F), lambda j, m, *_: (m, 0)), pl.BlockSpec(memory_space=pl.ANY)],
        out_specs=pl.BlockSpec((tm, tn // 2), lambda j, m, *_: (m, j)),
        out_shape=jax.ShapeDtypeStruct((A_pad, D // 2), jnp.uint32),
        scratch=[pltpu.VMEM((_W_RING, F, tn), _F32), pltpu.VMEM((F, tn), _BF16), pltpu.SemaphoreType.DMA((_W_RING,))],
        name="moe_down")(block_e, n_used, rank, seg_e, seg_j, ns, g, w_down)


def _combine_body(pos_ref, y_hbm, x1_ref, w_ref, g2_ref, lg_ref, lb_ref, *rest, alpha, ctx_tiles, final, tn):
    if final:
        xp_ref, xs_ref, x2_scr, buf, sem = rest
    else:
        sc_ref, sh_ref, xp_ref, xs_ref, hn_ref, x2_scr, buf, sem = rest
    tm = x1_ref.shape[0]
    i = pl.program_id(0)
    n = pl.num_programs(0)
    slot = i % 2
    nslot = (i + 1) % 2
    nstep = jnp.minimum(i + 1, n - 1)

    def start(step, sl, r):
        for k in range(TOP_K):
            _row_copy(y_hbm, pos_ref[(step * tm + r) * TOP_K + k], buf.at[sl, k], r, sem.at[sl]).start()

    def wait_all(sl):
        def wait(g, c):
            for u in range(_DMA_UNROLL // TOP_K):
                for k in range(TOP_K):
                    _row_copy(y_hbm, 0, buf.at[sl, k], g * (_DMA_UNROLL // TOP_K) + u, sem.at[sl]).wait()
            return c
        lax.fori_loop(0, tm * TOP_K // _DMA_UNROLL, wait, 0)

    @pl.when(i == 0)
    def _():
        def body(g, c):
            for u in range(_DMA_UNROLL // TOP_K):
                start(0, 0, g * (_DMA_UNROLL // TOP_K) + u)
            return c
        lax.fori_loop(0, tm * TOP_K // _DMA_UNROLL, body, 0)

    wait_all(slot)
    chunk = 32
    for c0 in range(0, tm, chunk):
        for r in range(c0, c0 + chunk):
            start(nstep, nslot, r)
        rows = slice(c0, c0 + chunk)
        w = w_ref[rows, :]
        ffn = None
        for k in range(TOP_K):
            parts = []
            for j in range(2 * buf.shape[3] // tn):
                parts += _unpack_bf16_pair(buf[slot, k, rows, j * (tn // 2):(j + 1) * (tn // 2)])
            term = w[:, k:k + 1] * jnp.concatenate(parts, axis=1)
            ffn = term if ffn is None else ffn + term
        x2 = _layer_norm(alpha * x1_ref[rows, :] + g2_ref[...] * ffn, lg_ref[...], lb_ref[...])
        x2_scr[rows, :] = x2
        if not final:
            hn_ref[rows, :] = (x2 * (1.0 + sc_ref[...]) + sh_ref[...]).astype(hn_ref.dtype)

    @pl.when(i < ctx_tiles)
    def _():
        xp_ref[...] = x2_scr[...]

    @pl.when(i >= ctx_tiles)
    def _():
        xs_ref[...] = x2_scr[...]

    @pl.when(i == n - 1)
    def _():
        wait_all(nslot)


def _combine(dm, y_packed, pos, wts, x1, mod5, ln_g, ln_b, l, alpha):
    tm = 256
    D = dm.D
    final = l == dm.depth - 1
    tile = pl.BlockSpec((tm, D), lambda i, p: (i, 0))
    row = pl.BlockSpec((1, D), lambda i, p: (0, 0))
    in_specs = [pl.BlockSpec(memory_space=pl.ANY), tile, pl.BlockSpec((tm, TOP_K), lambda i, p: (i, 0)),
                _mod_spec(dm, l, 5, tm), row, row]
    args = [pos, y_packed, x1, wts, mod5, ln_g.reshape(1, D), ln_b.reshape(1, D)]
    out_specs = _stream_specs(dm, tm, D)
    out_shape = [jax.ShapeDtypeStruct((dm.Tc, D), _F32), jax.ShapeDtypeStruct((dm.Tl, D), _F32)]
    if not final:
        in_specs += [_mod_spec(dm, l + 1, 1, tm), _mod_spec(dm, l + 1, 0, tm)]
        args += [mod5, mod5]
        out_specs = out_specs + [tile]
        out_shape = out_shape + [jax.ShapeDtypeStruct((dm.T, D), _BF16)]
    return _call(
        functools.partial(_combine_body, alpha=alpha, ctx_tiles=dm.Tc // tm, final=final, tn=_down_tile(D)),
        grid=(dm.T // tm,), nprefetch=1, in_specs=in_specs, out_specs=out_specs, out_shape=out_shape,
        scratch=[pltpu.VMEM((tm, D), _F32), pltpu.VMEM((2, TOP_K, tm, D // 2), jnp.uint32),
                 pltpu.SemaphoreType.DMA((2,))],
        name="moe_combine")(*args)


def kernel(x_prompt, x_sample, cache_k, cache_v, state_ssd, state_lru, c, c_ctx, w_ada, b_ada, w_in, w_out, attn_sink, ssd_conv_w, ssd_conv_b, ssd_a_log, ssd_dt_bias, ssd_d, ssd_norm_g, lru_conv_w, lru_conv_b, lru_lambda, lru_w_r, lru_b_r, lru_w_i, lru_b_i, pool_w, pool_b, pool_scale, ln1_g, ln1_b, ln2_g, ln2_b, w_router, router_bias, moe_w_gate, moe_w_up, moe_w_down):
    dm = _Dims(x_prompt, x_sample, cache_k, state_ssd, lru_w_r, w_router, moe_w_gate, w_ada)
    depth, D, Bw, KVw, GN, SH = dm.depth, dm.D, dm.Bw, dm.KVw, dm.GN, dm.SH
    alpha = (2 * depth) ** 0.25

    rows = -(-(1 + dm.DB) // _SUBLANES) * _SUBLANES
    cond = jnp.zeros((rows, D), _F32).at[0].set(c_ctx).at[1:1 + dm.DB].set(c)
    mod5 = _ada(cond, w_ada, b_ada).reshape(depth, rows, 6, 1, D)

    xc, xl = x_prompt.reshape(dm.Tc, D), x_sample.reshape(dm.Tl, D)
    h = _modulate(dm, xc, xl, mod5, 0)

    cos, sin = _rope_tables(dm.Ll)
    cache_k4 = cache_k.reshape(dm.DB, depth, dm.past, KVw)
    cache_v4 = cache_v.reshape(dm.DB, depth, dm.past, KVw)
    state_ssd5 = state_ssd.reshape(dm.DB, depth, 2, SH * SSD_HEAD_DIM, dm.N)
    w_router_pad = jnp.zeros((D, _LANES), _F32).at[:, :dm.E].set(w_router)

    o_k, o_v, o_z = Bw, Bw + KVw, Bw + 2 * KVw
    o_xs = o_z + Bw
    o_B = o_xs + Bw
    o_dt = o_B + 2 * GN
    o_xr = o_dt + 2 * SH
    o_gr, o_xp = o_xr + Bw, o_xr + 2 * Bw

    ks, vs, ssds, lrus = [], [], [], []
    for l in range(depth):
        wi = w_in[l]
        cols = lambda o, w: wi[:, o:o + w]
        w_slab = jnp.concatenate(
            [cols(0, Bw), cols(o_z, Bw), cols(o_xs, Bw), cols(o_xr, Bw), cols(o_gr, Bw), cols(o_xp, Bw),
             cols(o_k, KVw), cols(o_v, KVw), cols(o_B, 2 * GN), cols(o_dt, 2 * SH),
             jnp.zeros((D, dm.slab_w - dm.s_dt - 2 * SH), wi.dtype)], axis=1).astype(_BF16)
        slab = _matmul(h, w_slab, "in_proj")
        mix_in = jnp.zeros((dm.T, D), _BF16)

        mix_in = _attn_ctx(dm, slab, attn_sink[l], mix_in)
        qr, kr = _rope(dm, slab, cos, sin)
        mix_in = _attn_lat(dm, slab, qr, kr, cache_k4, cache_v4, attn_sink[l], l, mix_in)

        cw = ssd_conv_w[l]
        xs = _conv(dm, slab, dm.s_xs, Bw, cw[:, :Bw], ssd_conv_b[l][:Bw], True, _F32)
        bc = _conv(dm, slab, dm.s_B, 2 * GN, cw[:, Bw:], ssd_conv_b[l][Bw:], True, _BF16)
        pad = jnp.zeros((_LANES - 2 * SH,), _F32)
        dt_bias = jnp.concatenate([ssd_dt_bias[l].reshape(-1), pad]).reshape(1, _LANES)
        a_neg = jnp.concatenate([-jnp.exp(ssd_a_log[l].reshape(-1)), pad]).reshape(1, _LANES)
        yf, yb, s_f, s_b = _ssd(dm, slab, xs, bc, dt_bias, a_neg, state_ssd5, l)

        xf = _conv(dm, slab, dm.s_xr, Bw, lru_conv_w[l], lru_conv_b[l], False, _F32)
        hf, hb, fin_f, fin_b = _lru(dm, xf, lru_w_r[l].astype(_BF16), lru_w_i[l].astype(_BF16),
                                    lru_b_r[l], lru_b_i[l], lru_lambda[l], state_lru, l)

        d_exp = jnp.repeat(ssd_d[l], SSD_HEAD_DIM).reshape(1, Bw)
        mix_in = _finish(dm, slab, yf, yb, xs, d_exp, ssd_norm_g[l], hf, hb, mix_in)
        mix_in = _pool(dm, slab, pool_w[l], pool_b[l], pool_scale[l], mix_in)

        wo = w_out[l]
        w_mix = jnp.concatenate([wo[Bw:3 * Bw], wo[:Bw], wo[3 * Bw:]], axis=0).astype(_BF16)
        mix = _matmul(mix_in, w_mix, "out_proj")
        x1, h2, s_pad = _norm_router(dm, mix, xc, xl, mod5, ln1_g[l], ln1_b[l], w_router_pad, l, alpha)

        src_tok, block_e, n_used, n_valid, present, pos, wts = _route(s_pad[:, :dm.E], router_bias, _MOE_TM)
        xs_sorted = _gather(h2, src_tok, n_valid, _MOE_TM)
        g = _gateup(xs_sorted, block_e, n_used, moe_w_gate, moe_w_up, l, _MOE_TM)
        y = _down(g, block_e, n_used, present, moe_w_down, l, _MOE_TM)
        outs = _combine(dm, y, pos, wts, x1, mod5, ln2_g[l], ln2_b[l], l, alpha)
        xc, xl = outs[0], outs[1]
        if l + 1 < depth:
            h = outs[2]

        ks.append(slab[:dm.Tc, dm.s_k:dm.s_k + KVw].reshape(dm.B, dm.Lc, dm.KV, ATT_HEAD_DIM))
        vs.append(slab[:dm.Tc, dm.s_v:dm.s_v + KVw].reshape(dm.B, dm.Lc, dm.KV, ATT_HEAD_DIM))
        ssds.append(jnp.stack([s_f, s_b], axis=1).reshape(dm.B, 2, SH, SSD_HEAD_DIM, dm.N))
        lrus.append(jnp.concatenate([fin_f, fin_b], axis=1))

    y_prompt = xc.reshape(dm.B, dm.Lc, D)
    y_sample = xl.reshape(dm.DB, dm.Ll, D)
    return (y_prompt, y_sample, jnp.stack(ks, axis=1), jnp.stack(vs, axis=1),
            jnp.stack(ssds, axis=1), jnp.stack(lrus, axis=1))
```
